```python
import math
import jax, jax.numpy as jnp
from jax import lax
import numpy as np

D_MODEL = 2048
BATCH = 16
SEQ = 2048
DEPTH = 4

N_MIXERS = 4
N_SWA = (DEPTH + 3) // N_MIXERS
N_RWKV = (DEPTH + 2) // N_MIXERS
N_MLSTM = (DEPTH + 1) // N_MIXERS
N_FOX = DEPTH // N_MIXERS

SWA_HEAD_DIM = 64
SWA_HEADS = D_MODEL // SWA_HEAD_DIM
SWA_KV_HEADS = SWA_HEADS // 8
SWA_GROUP = SWA_HEADS // SWA_KV_HEADS
SWA_WIDTH = SWA_HEADS * SWA_HEAD_DIM
SWA_KV_WIDTH = SWA_KV_HEADS * SWA_HEAD_DIM
SWA_IN = 2 * SWA_WIDTH + 2 * SWA_KV_WIDTH
SWA_WINDOW = 128
SWA_BLOCK = 128
REL_BUCKETS = 32
REL_MAX_DIST = 128
RWKV_HEAD_DIM = 64
RWKV_HEADS = D_MODEL // RWKV_HEAD_DIM
RWKV_DECAY_LORA = 96
RWKV_AAA_LORA = 96
RWKV_N_MIX = 6
MLSTM_HEADS = 8
MLSTM_V_DIM = D_MODEL // MLSTM_HEADS
MLSTM_QK_DIM = MLSTM_V_DIM // 2
MLSTM_QK_WIDTH = MLSTM_HEADS * MLSTM_QK_DIM
MLSTM_V_WIDTH = MLSTM_HEADS * MLSTM_V_DIM
MLSTM_IN = 2 * MLSTM_QK_WIDTH + 3 * MLSTM_V_WIDTH + 2 * MLSTM_HEADS
MLSTM_CHUNK = 128
FOX_HEAD_DIM = 64
FOX_HEADS = D_MODEL // FOX_HEAD_DIM
FOX_WIDTH = FOX_HEADS * FOX_HEAD_DIM
FOX_IN = 4 * FOX_WIDTH + FOX_HEADS
FOX_BLOCK = 128

RMS_EPS = 1e-6
GN_EPS = 64e-5

kernel_name = "hybrid_interleaved_swa_rwkv7_mlstm_fox"


def rms_norm(x, g):
    xf = x.astype(jnp.float32)
    y = xf * lax.rsqrt(jnp.mean(xf * xf, axis=-1, keepdims=True) + RMS_EPS)
    return (y * g.astype(jnp.float32)).astype(x.dtype)


def t5_bucket(dist):
    max_exact = REL_BUCKETS // 2
    d_f = jnp.maximum(dist, 1).astype(jnp.float32)
    large = max_exact + (jnp.log(d_f / max_exact) / math.log(REL_MAX_DIST / max_exact)
                         * (REL_BUCKETS - max_exact)).astype(jnp.int32)
    large = jnp.minimum(large, REL_BUCKETS - 1)
    return jnp.where(dist < max_exact, dist, large)


def swa_sink_mixer(h, w_in, q_gain, k_gain, sinks, rel_table, w_out):
    bsz, seq, _ = h.shape
    L = SWA_BLOCK
    nb = seq // L
    q, k, v, gate = jnp.split(h @ w_in, [SWA_WIDTH, SWA_WIDTH + SWA_KV_WIDTH,
                                         SWA_WIDTH + 2 * SWA_KV_WIDTH], axis=-1)
    q = rms_norm(q.reshape(bsz, seq, SWA_KV_HEADS, SWA_GROUP, SWA_HEAD_DIM), q_gain)
    k = rms_norm(k.reshape(bsz, seq, SWA_KV_HEADS, SWA_HEAD_DIM), k_gain)
    v = v.reshape(bsz, seq, SWA_KV_HEADS, SWA_HEAD_DIM)
    pad = ((0, 0), (L, 0), (0, 0), (0, 0))
    k_pad = jnp.pad(k, pad)
    v_pad = jnp.pad(v, pad)
    qi = jnp.arange(L)[:, None]
    kj = jnp.arange(2 * L)[None, :]
    dist = qi + L - kj
    in_band = (dist >= 0) & (dist < SWA_WINDOW)
    bias = rel_table.astype(jnp.float32)[t5_bucket(jnp.maximum(dist, 0))]
    bias = bias.transpose(2, 0, 1).reshape(SWA_KV_HEADS, SWA_GROUP, L, 2 * L)
    sink = sinks.astype(jnp.float32).reshape(SWA_KV_HEADS, SWA_GROUP, 1, 1)
    scale = SWA_HEAD_DIM ** -0.5
    q_blocks = q.reshape(bsz, nb, L, SWA_KV_HEADS, SWA_GROUP, SWA_HEAD_DIM).transpose(1, 0, 2, 3, 4, 5)

    def one_block(args):
        j, qj = args
        start = j * L
        kb = lax.dynamic_slice_in_dim(k_pad, start, 2 * L, axis=1)
        vb = lax.dynamic_slice_in_dim(v_pad, start, 2 * L, axis=1)
        s = jnp.einsum('bqhgd,bkhd->bhgqk', qj, kb).astype(jnp.float32) * scale + bias
        valid = in_band & (start - L + kj >= 0)
        s = jnp.where(valid, s, -jnp.inf)
        m = jnp.maximum(jnp.max(s, axis=-1, keepdims=True), sink)
        p = jnp.exp(s - m)
        p = p / (jnp.sum(p, axis=-1, keepdims=True) + jnp.exp(sink - m))
        return jnp.einsum('bhgqk,bkhd->bqhgd', p.astype(vb.dtype), vb)

    o = lax.map(one_block, (jnp.arange(nb), q_blocks))
    o = o.transpose(1, 0, 2, 3, 4, 5).reshape(bsz, seq, SWA_WIDTH)
    return (o * jax.nn.silu(gate)) @ w_out


def rwkv7_mixer(h, mix, w_in, w0, w_lora1, w_lora2, a0, a_lora1, a_lora2,
                k_k, k_a, r_k, gn_w, gn_b, w_out):
    bsz, seq, d = h.shape
    H, N = RWKV_HEADS, RWKV_HEAD_DIM
    f32 = jnp.float32
    xx = jnp.pad(h, ((0, 0), (1, 0), (0, 0)))[:, :-1] - h
    xm = h[None] + xx[None] * mix[:, None, None, :]
    r, k, v, g = jnp.einsum('cbsd,cde->cbse', xm[:4], w_in)
    w_pre = (w0 + jnp.tanh(xm[4] @ w_lora1) @ w_lora2).astype(f32)
    log_w = -jnp.exp(-jax.nn.softplus(-w_pre) - 0.5)
    a = jax.nn.sigmoid((a0 + (xm[5] @ a_lora1) @ a_lora2).astype(f32))

    def heads(t):
        return t.astype(f32).reshape(bsz, seq, H, N)

    r, k, v, a, log_w = heads(r), heads(k), heads(v), heads(a), heads(log_w)
    kk = k * k_k.astype(f32).reshape(H, N)
    kk = kk / jnp.maximum(jnp.sqrt(jnp.sum(kk * kk, axis=-1, keepdims=True)), 1e-12)
    k = k * (1.0 + (a - 1.0) * k_a.astype(f32).reshape(H, N))

    def step(state, inp):
        r_t, w_t, k_t, v_t, kk_t, a_t = inp
        sa = jnp.einsum('bhvk,bhk->bhv', state, -kk_t)
        state = (state * w_t[:, :, None, :]
                 + sa[..., None] * (kk_t * a_t)[:, :, None, :]
                 + v_t[..., None] * k_t[:, :, None, :])
        return state, jnp.einsum('bhvk,bhk->bhv', state, r_t)

    def tm(t):
        return t.transpose(1, 0, 2, 3)

    state0 = jnp.zeros((bsz, H, N, N), f32)
    _, y = lax.scan(step, state0, (tm(r), tm(jnp.exp(log_w)), tm(k), tm(v), tm(kk), tm(a)))
    y = tm(y)
    mu = jnp.mean(y, axis=-1, keepdims=True)
    var = jnp.mean(jnp.square(y - mu), axis=-1, keepdims=True)
    y = ((y - mu) * lax.rsqrt(var + GN_EPS)).reshape(bsz, seq, d) * gn_w + gn_b
    bonus = jnp.sum(r * k * r_k.astype(f32), axis=-1, keepdims=True) * v
    y = y + bonus.reshape(bsz, seq, d)
    return (y.astype(h.dtype) * jax.nn.silu(g)) @ w_out


def mlstm_mixer(h, w_in, b_i, b_f, h_gain, w_out):
    bsz, seq, _ = h.shape
    H, dk, dv, L = MLSTM_HEADS, MLSTM_QK_DIM, MLSTM_V_DIM, MLSTM_CHUNK
    nc = seq // L
    f32 = jnp.float32
    s1 = MLSTM_QK_WIDTH
    s2 = s1 + MLSTM_QK_WIDTH
    s3 = s2 + MLSTM_V_WIDTH
    s4 = s3 + MLSTM_V_WIDTH
    s5 = s4 + MLSTM_V_WIDTH
    s6 = s5 + H
    q, k, v, o_pre, gate, i_pre, f_pre = jnp.split(h @ w_in, [s1, s2, s3, s4, s5, s6], axis=-1)
    q = q.astype(f32).reshape(bsz, seq, H, dk) * dk ** -0.5
    k = k.astype(f32).reshape(bsz, seq, H, dk)
    v = v.astype(f32).reshape(bsz, seq, H, dv)
    i_log = (i_pre + b_i).astype(f32)
    f_log = jax.nn.log_sigmoid((f_pre + b_f).astype(f32))

    def chunks(t):
        t = t.reshape((bsz, nc, L) + t.shape[2:])
        return jnp.moveaxis(t, (1, 3), (0, 2))

    tri = jnp.tril(jnp.ones((L, L), dtype=bool))

    def chunk_step(carry, inp):
        C, n, m = carry
        qc, kc, vc, ic, fc = inp
        b = jnp.cumsum(fc, axis=-1)
        g = b[..., -1]
        dmat = jnp.where(tri, b[..., :, None] - b[..., None, :] + ic[..., None, :], -jnp.inf)
        inter = b + m[..., None]
        m_t = jnp.maximum(inter, jnp.max(dmat, axis=-1))
        w_intra = jnp.exp(dmat - m_t[..., None])
        w_inter = jnp.exp(inter - m_t)
        sw = jnp.einsum('bhtd,bhsd->bhts', qc, kc) * w_intra
        num = (w_inter[..., None] * jnp.einsum('bhvk,bhtk->bhtv', C, qc)
               + jnp.einsum('bhts,bhsv->bhtv', sw, vc))
        den = w_inter * jnp.einsum('bhk,bhtk->bht', n, qc) + jnp.sum(sw, axis=-1)
        h_t = num / jnp.maximum(jnp.abs(den), jnp.exp(-m_t))[..., None]
        decay_s = g[..., None] - b + ic
        m_new = jnp.maximum(g + m, jnp.max(decay_s, axis=-1))
        ws = jnp.exp(decay_s - m_new[..., None])
        carry_scale = jnp.exp(g + m - m_new)
        C = carry_scale[..., None, None] * C + jnp.einsum('bhs,bhsv,bhsk->bhvk', ws, vc, kc)
        n = carry_scale[..., None] * n + jnp.einsum('bhs,bhsk->bhk', ws, kc)
        return (C, n, m_new), h_t

    carry0 = (jnp.zeros((bsz, H, dv, dk), f32), jnp.zeros((bsz, H, dk), f32), jnp.zeros((bsz, H), f32))
    _, hs = lax.scan(chunk_step, carry0, (chunks(q), chunks(k), chunks(v), chunks(i_log), chunks(f_log)))
    hs = jnp.moveaxis(hs, (0, 2), (1, 3)).reshape(bsz, seq, H, dv)
    hs = rms_norm(hs, h_gain).reshape(bsz, seq, MLSTM_V_WIDTH).astype(h.dtype)
    return (hs * jax.nn.sigmoid(o_pre) * jax.nn.silu(gate)) @ w_out


def fox_mixer(h, w_in, b_f, q_gain, k_gain, w_out):
    bsz, seq, _ = h.shape
    H, dh, L = FOX_HEADS, FOX_HEAD_DIM, FOX_BLOCK
    nb = seq // L
    W = FOX_WIDTH
    q, k, v, gate, f_pre = jnp.split(h @ w_in, [W, 2 * W, 3 * W, 4 * W], axis=-1)
    q = rms_norm(q.reshape(bsz, seq, H, dh), q_gain)
    k = rms_norm(k.reshape(bsz, seq, H, dh), k_gain)
    v = v.reshape(bsz, seq, H, dh)
    log_f = jax.nn.log_sigmoid((f_pre + b_f).astype(jnp.float32))
    F = jnp.cumsum(log_f, axis=1).transpose(0, 2, 1)
    q_blocks = q.reshape(bsz, nb, L, H, dh).transpose(1, 0, 2, 3, 4)
    F_blocks = F.reshape(bsz, H, nb, L).transpose(2, 0, 1, 3)
    key_pos = jnp.arange(seq)
    scale = dh ** -0.5

    def one_block(args):
        j, qj, Fj = args
        s = (jnp.einsum('bqhd,bkhd->bhqk', qj, k).astype(jnp.float32) * scale
             + (Fj[..., :, None] - F[:, :, None, :]))
        q_pos = j * L + jnp.arange(L)
        s = jnp.where(key_pos[None, :] <= q_pos[:, None], s, -jnp.inf)
        p = jax.nn.softmax(s, axis=-1)
        return jnp.einsum('bhqk,bkhd->bqhd', p.astype(v.dtype), v)

    o = lax.map(one_block, (jnp.arange(nb), q_blocks, F_blocks))
    o = o.transpose(1, 0, 2, 3, 4).reshape(bsz, seq, W)
    return (o * jax.nn.silu(gate)) @ w_out


def setup_inputs(seed: int = 0) -> dict:
    key = jax.random.key(seed)
    keys = jax.random.split(key, 48)
    counter = [0]

    def next_key():
        kk = keys[counter[0]]
        counter[0] += 1
        return kk

    def nrm(shape, scale):
        return jax.random.normal(next_key(), shape, jnp.float32) * scale

    def gain(shape):
        return 1.0 + nrm(shape, 0.05)

    D = D_MODEL
    return {
        "x": nrm((BATCH, SEQ, D), 1.0),
        "rel_bias": nrm((REL_BUCKETS, SWA_HEADS), 0.2),
        "swa_norm": gain((N_SWA, D)),
        "swa_w_in": nrm((N_SWA, D, SWA_IN), D ** -0.5),
        "swa_q_gain": gain((N_SWA, SWA_HEAD_DIM)),
        "swa_k_gain": gain((N_SWA, SWA_HEAD_DIM)),
        "swa_sinks": nrm((N_SWA, SWA_HEADS), 0.5),
        "swa_w_out": nrm((N_SWA, SWA_WIDTH, D), SWA_WIDTH ** -0.5),
        "rwkv_norm": gain((N_RWKV, D)),
        "rwkv_mix": jax.random.uniform(next_key(), (N_RWKV, RWKV_N_MIX, D), jnp.float32),
        "rwkv_w_in": nrm((N_RWKV, 4, D, D), D ** -0.5),
        "rwkv_w0": -2.0 + nrm((N_RWKV, D), 1.0),
        "rwkv_w_lora1": nrm((N_RWKV, D, RWKV_DECAY_LORA), D ** -0.5),
        "rwkv_w_lora2": nrm((N_RWKV, RWKV_DECAY_LORA, D), 0.5 * RWKV_DECAY_LORA ** -0.5),
        "rwkv_a0": nrm((N_RWKV, D), 0.1),
        "rwkv_a_lora1": nrm((N_RWKV, D, RWKV_AAA_LORA), D ** -0.5),
        "rwkv_a_lora2": nrm((N_RWKV, RWKV_AAA_LORA, D), 0.5 * RWKV_AAA_LORA ** -0.5),
        "rwkv_k_k": 0.85 + nrm((N_RWKV, D), 0.05),
        "rwkv_k_a": 1.0 + nrm((N_RWKV, D), 0.05),
        "rwkv_r_k": nrm((N_RWKV, RWKV_HEADS, RWKV_HEAD_DIM), 0.1),
        "rwkv_gn_w": gain((N_RWKV, D)),
        "rwkv_gn_b": nrm((N_RWKV, D), 0.02),
        "rwkv_w_out": nrm((N_RWKV, D, D), D ** -0.5),
        "mlstm_norm": gain((N_MLSTM, D)),
        "mlstm_w_in": nrm((N_MLSTM, D, MLSTM_IN), D ** -0.5),
        "mlstm_b_i": nrm((N_MLSTM, MLSTM_HEADS), 0.1),
        "mlstm_b_f": 3.0 + nrm((N_MLSTM, MLSTM_HEADS), 0.5),
        "mlstm_h_gain": gain((N_MLSTM, MLSTM_HEADS, MLSTM_V_DIM)),
        "mlstm_w_out": nrm((N_MLSTM, MLSTM_V_WIDTH, D), MLSTM_V_WIDTH ** -0.5),
        "fox_norm": gain((N_FOX, D)),
        "fox_w_in": nrm((N_FOX, D, FOX_IN), D ** -0.5),
        "fox_b_f": 3.0 + nrm((N_FOX, FOX_HEADS), 0.5),
        "fox_q_gain": gain((N_FOX, FOX_HEAD_DIM)),
        "fox_k_gain": gain((N_FOX, FOX_HEAD_DIM)),
        "fox_w_out": nrm((N_FOX, FOX_WIDTH, D), FOX_WIDTH ** -0.5),
    }


def reference(x, rel_bias, swa_norm, swa_w_in, swa_q_gain, swa_k_gain, swa_sinks, swa_w_out,
              rwkv_norm, rwkv_mix, rwkv_w_in, rwkv_w0, rwkv_w_lora1, rwkv_w_lora2, rwkv_a0,
              rwkv_a_lora1, rwkv_a_lora2, rwkv_k_k, rwkv_k_a, rwkv_r_k, rwkv_gn_w, rwkv_gn_b,
              rwkv_w_out, mlstm_norm, mlstm_w_in, mlstm_b_i, mlstm_b_f, mlstm_h_gain, mlstm_w_out,
              fox_norm, fox_w_in, fox_b_f, fox_q_gain, fox_k_gain, fox_w_out):
    for layer in range(DEPTH):
        kind, idx = layer % N_MIXERS, layer // N_MIXERS
        if kind == 0:
            x = x + swa_sink_mixer(rms_norm(x, swa_norm[idx]), swa_w_in[idx], swa_q_gain[idx],
                                   swa_k_gain[idx], swa_sinks[idx], rel_bias, swa_w_out[idx])
        elif kind == 1:
            x = x + rwkv7_mixer(rms_norm(x, rwkv_norm[idx]), rwkv_mix[idx], rwkv_w_in[idx],
                                rwkv_w0[idx], rwkv_w_lora1[idx], rwkv_w_lora2[idx], rwkv_a0[idx],
                                rwkv_a_lora1[idx], rwkv_a_lora2[idx], rwkv_k_k[idx], rwkv_k_a[idx],
                                rwkv_r_k[idx], rwkv_gn_w[idx], rwkv_gn_b[idx], rwkv_w_out[idx])
        elif kind == 2:
            x = x + mlstm_mixer(rms_norm(x, mlstm_norm[idx]), mlstm_w_in[idx], mlstm_b_i[idx],
                                mlstm_b_f[idx], mlstm_h_gain[idx], mlstm_w_out[idx])
        else:
            x = x + fox_mixer(rms_norm(x, fox_norm[idx]), fox_w_in[idx], fox_b_f[idx],
                              fox_q_gain[idx], fox_k_gain[idx], fox_w_out[idx])
    return x
```

```python
import functools
import math

import jax
import jax.numpy as jnp
from jax import lax
from jax.experimental import pallas as pl
from jax.experimental.pallas import tpu as pltpu

F32 = jnp.float32
BF16 = jnp.bfloat16
HIGHEST = lax.Precision.HIGHEST

D_MODEL = 2048
RMS_EPS = 1e-6
GN_EPS = 64e-5

HEAD_DIM = 64
N_HEADS = D_MODEL // HEAD_DIM
LANES = 128
N_PAIRS = D_MODEL // LANES

SWA_KV_HEADS = 4
SWA_GROUP = N_HEADS // SWA_KV_HEADS
SWA_KV_WIDTH = SWA_KV_HEADS * HEAD_DIM
SWA_BLOCK = 128
REL_BUCKETS = 32
REL_MAX_DIST = 128

RWKV_CHUNK = 64
RWKV_LORA_PAD = 128

MLSTM_HEADS = 8
MLSTM_V_DIM = 256
MLSTM_QK_DIM = 128
MLSTM_CHUNK = 128
MLSTM_MAIN = 2 * MLSTM_HEADS * MLSTM_QK_DIM + 3 * MLSTM_HEADS * MLSTM_V_DIM

FOX_BLOCK = 256

VMEM_LIMIT = 56 * 1024 * 1024


def _params(*sem):
    return pltpu.CompilerParams(dimension_semantics=sem, vmem_limit_bytes=VMEM_LIMIT)


def _nt(a, b, precision=None):
    return lax.dot_general(a, b, (((1,), (1,)), ((), ())), precision=precision,
                           preferred_element_type=F32)


def _tn(a, b, precision=None):
    return lax.dot_general(a, b, (((0,), (0,)), ((), ())), precision=precision,
                           preferred_element_type=F32)


def _mm(a, b, precision=None):
    return jnp.dot(a, b, precision=precision, preferred_element_type=F32)


def _rms_rows(x, g):
    return x * lax.rsqrt(jnp.mean(x * x, axis=-1, keepdims=True) + RMS_EPS) * g


def _silu(x):
    return x * jax.nn.sigmoid(x)


def _norm_matmul_body(x_ref, g_ref, w_ref, o_ref, xn_ref):
    @pl.when(pl.program_id(1) == 0)
    def _():
        xn_ref[...] = _rms_rows(x_ref[...], g_ref[...]).astype(BF16)

    o_ref[...] = _mm(xn_ref[...], w_ref[...]).astype(o_ref.dtype)


def _norm_matmul(x, g, w, *, tn, out_dtype=F32, name="norm_matmul"):
    m, k = x.shape
    n = w.shape[1]
    tm = min(512, m)
    return pl.pallas_call(
        _norm_matmul_body,
        grid=(m // tm, n // tn),
        in_specs=[
            pl.BlockSpec((tm, k), lambda i, j: (i, 0)),
            pl.BlockSpec((1, k), lambda i, j: (0, 0)),
            pl.BlockSpec((k, tn), lambda i, j: (0, j)),
        ],
        out_specs=pl.BlockSpec((tm, tn), lambda i, j: (i, j)),
        out_shape=jax.ShapeDtypeStruct((m, n), out_dtype),
        scratch_shapes=[pltpu.VMEM((tm, k), BF16)],
        compiler_params=_params("parallel", "arbitrary"),
        name=name,
    )(x, g.reshape(1, k), w)


def _gate_proj_body(x_ref, g_ref, w_ref, wt_ref, cols_ref, rows_ref):
    xn = _rms_rows(x_ref[...], g_ref[...]).astype(BF16)
    cols_ref[...] = _mm(xn, w_ref[...])
    rows_ref[...] = _nt(wt_ref[...], xn)


def _gate_proj(x, g, w_gates):
    m, k = x.shape
    n_g = w_gates.shape[1]
    tm = min(512, m)
    w_pad = jnp.zeros((k, LANES), BF16).at[:, :n_g].set(w_gates.astype(BF16))
    w_t = w_gates.T.astype(BF16)
    return pl.pallas_call(
        _gate_proj_body,
        grid=(m // tm,),
        in_specs=[
            pl.BlockSpec((tm, k), lambda i: (i, 0)),
            pl.BlockSpec((1, k), lambda i: (0, 0)),
            pl.BlockSpec((k, LANES), lambda i: (0, 0)),
            pl.BlockSpec((n_g, k), lambda i: (0, 0)),
        ],
        out_specs=[
            pl.BlockSpec((tm, LANES), lambda i: (i, 0)),
            pl.BlockSpec((n_g, tm), lambda i: (0, i)),
        ],
        out_shape=[
            jax.ShapeDtypeStruct((m, LANES), F32),
            jax.ShapeDtypeStruct((n_g, m), F32),
        ],
        compiler_params=_params("parallel"),
        name="gate_proj",
    )(x, g.reshape(1, k), w_pad, w_t)


def _out_proj_body(a_ref, w_ref, r_ref, o_ref):
    o_ref[...] = r_ref[...] + _mm(a_ref[...], w_ref[...])


def _out_proj(a, w, resid):
    m, k = a.shape
    n = w.shape[1]
    tm = min(512, m)
    return pl.pallas_call(
        _out_proj_body,
        grid=(m // tm,),
        in_specs=[
            pl.BlockSpec((tm, k), lambda i: (i, 0)),
            pl.BlockSpec((k, n), lambda i: (0, 0)),
            pl.BlockSpec((tm, n), lambda i: (i, 0)),
        ],
        out_specs=pl.BlockSpec((tm, n), lambda i: (i, 0)),
        out_shape=jax.ShapeDtypeStruct((m, n), F32),
        compiler_params=_params("parallel"),
        name="out_proj",
    )(a, w, resid)


def _t5_bucket(dist):
    max_exact = REL_BUCKETS // 2
    d_f = jnp.maximum(dist, 1).astype(F32)
    large = max_exact + (jnp.log(d_f / max_exact) / math.log(REL_MAX_DIST / max_exact)
                         * (REL_BUCKETS - max_exact)).astype(jnp.int32)
    large = jnp.minimum(large, REL_BUCKETS - 1)
    return jnp.where(dist < max_exact, dist, large)


def _swa_body(nb, q_ref, gate_ref, kc_ref, kp_ref, vc_ref, vp_ref, bias_ref, sink_ref,
              qg_ref, kg_ref, o_ref):
    L = SWA_BLOCK
    j = pl.program_id(0) % nb
    qi = lax.broadcasted_iota(jnp.int32, (L, 2 * L), 0)
    kj = lax.broadcasted_iota(jnp.int32, (L, 2 * L), 1)
    dist = qi + L - kj
    valid = (dist >= 0) & (dist < L) & ((kj >= L) | (j > 0))
    scale = HEAD_DIM ** -0.5
    outs = []
    for kh in range(SWA_KV_HEADS):
        ks = slice(kh * HEAD_DIM, (kh + 1) * HEAD_DIM)
        k2 = jnp.concatenate([kp_ref[:, ks], kc_ref[:, ks]], axis=0)
        kn = _rms_rows(k2, kg_ref[...]).astype(BF16)
        v2 = jnp.concatenate([vp_ref[:, ks], vc_ref[:, ks]], axis=0).astype(BF16)
        for g in range(SWA_GROUP):
            h = kh * SWA_GROUP + g
            qh = q_ref[:, h * HEAD_DIM:(h + 1) * HEAD_DIM]
            qn = (_rms_rows(qh, qg_ref[...]) * scale).astype(BF16)
            s = _nt(qn, kn) + bias_ref[h]
            s = jnp.where(valid, s, -jnp.inf)
            sink = sink_ref[h]
            m = jnp.maximum(jnp.max(s, axis=-1, keepdims=True), sink)
            p = jnp.exp(s - m)
            denom = jnp.sum(p, axis=-1, keepdims=True) + jnp.exp(sink - m)
            outs.append(_mm(p.astype(BF16), v2) / denom)
    o = jnp.concatenate(outs, axis=-1)
    o_ref[...] = (o * _silu(gate_ref[...])).astype(o_ref.dtype)


def _swa_attention(proj, bias, sinks, q_gain, k_gain, bsz, seq):
    L = SWA_BLOCK
    nb = seq // L
    m = bsz * seq
    kcol = 2 * D_MODEL // SWA_KV_WIDTH
    cur = lambda r: r
    prev = lambda r: jnp.where(r % nb == 0, r, r - 1)
    return pl.pallas_call(
        functools.partial(_swa_body, nb),
        grid=(bsz * nb,),
        in_specs=[
            pl.BlockSpec((L, D_MODEL), lambda r: (r, 0)),
            pl.BlockSpec((L, D_MODEL), lambda r: (r, 1)),
            pl.BlockSpec((L, SWA_KV_WIDTH), lambda r: (cur(r), kcol)),
            pl.BlockSpec((L, SWA_KV_WIDTH), lambda r: (prev(r), kcol)),
            pl.BlockSpec((L, SWA_KV_WIDTH), lambda r: (cur(r), kcol + 1)),
            pl.BlockSpec((L, SWA_KV_WIDTH), lambda r: (prev(r), kcol + 1)),
            pl.BlockSpec((N_HEADS, L, 2 * L), lambda r: (0, 0, 0)),
            pl.BlockSpec(memory_space=pltpu.SMEM),
            pl.BlockSpec((1, HEAD_DIM), lambda r: (0, 0)),
            pl.BlockSpec((1, HEAD_DIM), lambda r: (0, 0)),
        ],
        out_specs=pl.BlockSpec((L, D_MODEL), lambda r: (r, 0)),
        out_shape=jax.ShapeDtypeStruct((m, D_MODEL), BF16),
        compiler_params=_params("parallel"),
        name="swa_attention",
    )(proj, proj, proj, proj, proj, proj, bias, sinks,
      q_gain.reshape(1, HEAD_DIM), k_gain.reshape(1, HEAD_DIM))


def _swa_layer(x2, bsz, seq, rel_bias, norm, w_in, q_gain, k_gain, sinks, w_out):
    W = D_MODEL
    kvw = SWA_KV_WIDTH
    w_perm = jnp.concatenate([w_in[:, :W], w_in[:, W + 2 * kvw:], w_in[:, W:W + 2 * kvw]], axis=1)
    proj = _norm_matmul(x2, norm, w_perm.astype(BF16), tn=512, name="swa_in_proj")
    L = SWA_BLOCK
    qi = jnp.arange(L)[:, None]
    kj = jnp.arange(2 * L)[None, :]
    dist = qi + L - kj
    bias = rel_bias.astype(F32)[_t5_bucket(jnp.maximum(dist, 0))].transpose(2, 0, 1)
    a = _swa_attention(proj, bias, sinks.astype(F32), q_gain, k_gain, bsz, seq)
    return _out_proj(a, w_out.astype(BF16), x2)


def _shifted_mix(x_ref, xp_ref, g_ref, seq_start):
    h = _rms_rows(x_ref[...], g_ref[...])
    h_prev_row = _rms_rows(xp_ref[7:8, :], g_ref[...])
    h_prev_row = jnp.where(seq_start, 0.0, h_prev_row)
    rolled = pltpu.roll(h, 1, axis=0)
    row = lax.broadcasted_iota(jnp.int32, h.shape, 0)
    shifted = jnp.where(row == 0, h_prev_row, rolled)
    return h, shifted - h


def _rwkv_in_body(tiles_per_seq, x_ref, xp_ref, g_ref, mix_ref, w_ref, o_ref, xm_ref):
    i = pl.program_id(0)

    @pl.when((pl.program_id(1) == 0) & (pl.program_id(2) == 0))
    def _():
        h, xx = _shifted_mix(x_ref, xp_ref, g_ref, i % tiles_per_seq == 0)
        for c in range(4):
            xm_ref[c] = (h + xx * mix_ref[c:c + 1, :]).astype(BF16)

    c = pl.program_id(1)
    o_ref[0] = _mm(xm_ref[c], w_ref[0])


def _rwkv_in_proj(x2, seq, norm, mix8, w4):
    m, k = x2.shape
    tm = min(512, seq)
    tn = 512
    rows8 = tm // 8
    return pl.pallas_call(
        functools.partial(_rwkv_in_body, seq // tm),
        grid=(m // tm, 4, k // tn),
        in_specs=[
            pl.BlockSpec((tm, k), lambda i, c, j: (i, 0)),
            pl.BlockSpec((8, k), lambda i, c, j: (jnp.maximum(i * rows8 - 1, 0), 0)),
            pl.BlockSpec((1, k), lambda i, c, j: (0, 0)),
            pl.BlockSpec((8, k), lambda i, c, j: (0, 0)),
            pl.BlockSpec((1, k, tn), lambda i, c, j: (c, 0, j)),
        ],
        out_specs=pl.BlockSpec((1, tm, tn), lambda i, c, j: (c, i, j)),
        out_shape=jax.ShapeDtypeStruct((4, m, k), F32),
        scratch_shapes=[pltpu.VMEM((4, tm, k), BF16)],
        compiler_params=_params("parallel", "arbitrary", "arbitrary"),
        name="rwkv_in_proj",
    )(x2, x2, norm.reshape(1, k), mix8, w4)


def _rwkv_lora_body(tiles_per_seq, x_ref, xp_ref, g_ref, mix_ref, w1_ref, w2_ref, a1_ref, a2_ref,
                    w0_ref, a0_ref, lw_ref, a_ref):
    i = pl.program_id(0)
    h, xx = _shifted_mix(x_ref, xp_ref, g_ref, i % tiles_per_seq == 0)
    xw = (h + xx * mix_ref[4:5, :]).astype(BF16)
    xa = (h + xx * mix_ref[5:6, :]).astype(BF16)
    t = jnp.tanh(_mm(xw, w1_ref[...])).astype(BF16)
    w_pre = w0_ref[...] + _mm(t, w2_ref[...])
    lw_ref[...] = -jnp.exp(-jax.nn.softplus(-w_pre) - 0.5)
    u = _mm(xa, a1_ref[...]).astype(BF16)
    a_ref[...] = jax.nn.sigmoid(a0_ref[...] + _mm(u, a2_ref[...]))


def _rwkv_lora(x2, seq, norm, mix8, w1, w2, a1, a2, w0, a0):
    m, k = x2.shape
    tm = min(512, seq)
    rows8 = tm // 8
    P = RWKV_LORA_PAD
    full = lambda shape: pl.BlockSpec(shape, lambda i: tuple(0 for _ in shape))
    return pl.pallas_call(
        functools.partial(_rwkv_lora_body, seq // tm),
        grid=(m // tm,),
        in_specs=[
            pl.BlockSpec((tm, k), lambda i: (i, 0)),
            pl.BlockSpec((8, k), lambda i: (jnp.maximum(i * rows8 - 1, 0), 0)),
            full((1, k)), full((8, k)),
            full((k, P)), full((P, k)), full((k, P)), full((P, k)),
            full((1, k)), full((1, k)),
        ],
        out_specs=[pl.BlockSpec((tm, k), lambda i: (i, 0)), pl.BlockSpec((tm, k), lambda i: (i, 0))],
        out_shape=[jax.ShapeDtypeStruct((m, k), F32), jax.ShapeDtypeStruct((m, k), F32)],
        compiler_params=_params("parallel"),
        name="rwkv_lora",
    )(x2, x2, norm.reshape(1, k), mix8, w1, w2, a1, a2, w0.reshape(1, k), a0.reshape(1, k))


def _stack2(z, m0):
    zero = jnp.zeros_like(z)
    return jnp.concatenate([jnp.where(m0, z, zero), jnp.where(m0, zero, z)], axis=0)


def _rwkv_scan_body(r_ref, k_ref, v_ref, g_ref, lw_ref, a_ref, kk_ref, ka_ref, rk_ref,
                    gnw_ref, gnb_ref, o_ref, ht_ref):
    C = RWKV_CHUNK

    @pl.when(pl.program_id(1) == 0)
    def _():
        ht_ref[...] = jnp.zeros_like(ht_ref)

    lane = lax.broadcasted_iota(jnp.int32, (C, LANES), 1)
    tok = lax.broadcasted_iota(jnp.int32, (C, LANES), 0)
    m0 = lane < HEAD_DIM
    col = lane % HEAD_DIM
    strict = tok > col
    incl = tok >= col
    ti = lax.broadcasted_iota(jnp.int32, (C, C), 0)
    tj = lax.broadcasted_iota(jnp.int32, (C, C), 1)
    tri = (ti >= tj).astype(F32)
    bi = lax.broadcasted_iota(jnp.int32, (LANES, LANES), 0) // HEAD_DIM
    bj = lax.broadcasted_iota(jnp.int32, (LANES, LANES), 1) // HEAD_DIM
    bd = bi == bj
    bd_ones = bd.astype(F32)
    m0_2 = lax.broadcasted_iota(jnp.int32, (2 * C, LANES), 1) < HEAD_DIM

    def pair(p, carry):
        sl = pl.ds(pl.multiple_of(p * LANES, LANES), LANES)
        r = r_ref[0, :, sl]
        k = k_ref[0, :, sl]
        v = v_ref[0, :, sl]
        lw = lw_ref[:, sl]
        a = a_ref[:, sl]
        kkp = k * kk_ref[:, sl]
        ss = _mm(kkp * kkp, bd_ones, HIGHEST)
        kk = kkp / jnp.maximum(jnp.sqrt(ss), 1e-12)
        kp = k * (1.0 + (a - 1.0) * ka_ref[:, sl])
        cum = _mm(tri, lw, HIGHEST)
        cum_end = cum[C - 1:C, :]
        g_in = jnp.exp(cum)
        g_prev = jnp.exp(cum - lw)
        g_inv = jnp.exp(-cum)
        g_end = jnp.exp(cum_end - cum)
        kka = kk * a
        a_t = -kk * g_prev
        r_t = r * g_in
        b_t = kka * g_inv
        k_t = kp * g_inv

        x1 = jnp.concatenate([a_t, r_t], axis=0).astype(BF16)
        gb = _nt(x1, _stack2(b_t, m0).astype(BF16))
        gk = _nt(x1, _stack2(k_t, m0).astype(BF16))
        zero = jnp.zeros((C, LANES), F32)
        l_ab = jnp.where(strict, gb[:C], zero)
        l_ak = jnp.where(strict, gk[:C], zero)
        p_rb = jnp.where(incl, gb[C:], zero)
        p_rk = jnp.where(incl, gk[C:], zero)

        t_inv = jnp.where(tok == col, 1.0, zero) + jnp.where(
            (tok // 2 == col // 2) & (tok % 2 == 1) & (col % 2 == 0), l_ab, zero)
        s = 2
        while s < C:
            sel = (tok // (2 * s) == col // (2 * s)) & (tok % (2 * s) >= s) & (col % (2 * s) < s)
            l_s = jnp.where(sel, l_ab, zero)
            tl = _mm(t_inv.astype(BF16), _stack2(l_s, m0).astype(BF16))
            t_inv = t_inv + _mm(tl.astype(BF16), _stack2(t_inv, m0).astype(BF16))
            s *= 2

        ht = ht_ref[p]
        xh = _nt(x1, ht.astype(BF16))
        v2 = _stack2(v, m0).astype(BF16)
        rhs = xh[:C] + _mm(l_ak.astype(BF16), v2)
        u = _mm(t_inv.astype(BF16), _stack2(rhs, m0).astype(BF16))
        u2 = _stack2(u, m0).astype(BF16)
        y = (xh[C:] + _mm(p_rb.astype(BF16), u2) + _mm(p_rk.astype(BF16), v2))
        uv = jnp.concatenate([u, v], axis=0).astype(BF16)
        bk_end = jnp.concatenate([kka * g_end, kp * g_end], axis=0).astype(BF16)
        upd = _tn(uv, bk_end)
        ht_ref[p] = ht * jnp.exp(cum_end) + jnp.where(bd, upd, 0.0)

        mu = _mm(y, bd_ones, HIGHEST) * (1.0 / HEAD_DIM)
        dy = y - mu
        var = _mm(dy * dy, bd_ones, HIGHEST) * (1.0 / HEAD_DIM)
        yn = dy * lax.rsqrt(var + GN_EPS) * gnw_ref[:, sl] + gnb_ref[:, sl]
        bonus = _mm(r * kp * rk_ref[:, sl], bd_ones, HIGHEST) * v
        o_ref[:, sl] = ((yn + bonus) * _silu(g_ref[0, :, sl])).astype(o_ref.dtype)
        return carry

    lax.fori_loop(0, N_PAIRS, pair, 0)


def _rwkv_scan(rkvg, lw, a, k_k, k_a, r_k, gn_w, gn_b, bsz, seq):
    C = RWKV_CHUNK
    nc = seq // C
    m = bsz * seq
    k = D_MODEL
    part = lambda c: pl.BlockSpec((1, C, k), lambda b, j, c=c: (c, b * nc + j, 0))
    tile = pl.BlockSpec((C, k), lambda b, j: (b * nc + j, 0))
    vec = pl.BlockSpec((1, k), lambda b, j: (0, 0))
    return pl.pallas_call(
        _rwkv_scan_body,
        grid=(bsz, nc),
        in_specs=[part(0), part(1), part(2), part(3), tile, tile, vec, vec, vec, vec, vec],
        out_specs=tile,
        out_shape=jax.ShapeDtypeStruct((m, k), BF16),
        scratch_shapes=[pltpu.VMEM((N_PAIRS, LANES, LANES), F32)],
        compiler_params=_params("parallel", "arbitrary"),
        name="rwkv_scan",
    )(rkvg, rkvg, rkvg, rkvg, lw, a, k_k.reshape(1, k), k_a.reshape(1, k), r_k.reshape(1, k),
      gn_w.reshape(1, k), gn_b.reshape(1, k))


def _rwkv_layer(x2, bsz, seq, norm, mix, w_in, w0, w_lora1, w_lora2, a0, a_lora1, a_lora2,
                k_k, k_a, r_k, gn_w, gn_b, w_out):
    k = D_MODEL
    P = RWKV_LORA_PAD
    mix8 = jnp.zeros((8, k), F32).at[:mix.shape[0]].set(mix)
    pad_cols = lambda w: jnp.zeros((k, P), BF16).at[:, :w.shape[1]].set(w.astype(BF16))
    pad_rows = lambda w: jnp.zeros((P, k), BF16).at[:w.shape[0]].set(w.astype(BF16))
    rkvg = _rwkv_in_proj(x2, seq, norm, mix8, w_in.astype(BF16))
    lw, a = _rwkv_lora(x2, seq, norm, mix8, pad_cols(w_lora1), pad_rows(w_lora2),
                       pad_cols(a_lora1), pad_rows(a_lora2), w0, a0)
    y = _rwkv_scan(rkvg, lw, a, k_k, k_a, r_k, gn_w, gn_b, bsz, seq)
    return _out_proj(y, w_out.astype(BF16), x2)


def _mlstm_body(q_ref, k_ref, v_ref, op_ref, gate_ref, gc_ref, gr_ref, bi_ref, bf_ref, hg_ref,
                o_ref, ct_ref, m_ref):
    L = MLSTM_CHUNK
    H = MLSTM_HEADS
    dk = MLSTM_QK_DIM
    dv = MLSTM_V_DIM

    @pl.when(pl.program_id(1) == 0)
    def _():
        ct_ref[...] = jnp.zeros_like(ct_ref)
        m_ref[...] = jnp.zeros_like(m_ref)

    ti = lax.broadcasted_iota(jnp.int32, (L, L), 0)
    tj = lax.broadcasted_iota(jnp.int32, (L, L), 1)
    tril = ti >= tj
    tri_f = tril.astype(F32)
    gc = gc_ref[...] + bi_ref[...]
    gr = gr_ref[...] + bf_ref[...]
    lane = lax.broadcasted_iota(jnp.int32, gc.shape, 1)
    f_cols = jnp.where((lane >= H) & (lane < 2 * H), jax.nn.log_sigmoid(gc), 0.0)
    b_cols = _mm(tri_f, f_cols, HIGHEST)
    f_rows = jax.nn.log_sigmoid(gr)
    b_rows = _nt(f_rows, tri_f, HIGHEST)
    ones_blk = jnp.ones((L, LANES), BF16)

    for h in range(H):
        q = (q_ref[:, h * dk:(h + 1) * dk] * dk ** -0.5).astype(BF16)
        k = k_ref[:, h * dk:(h + 1) * dk].astype(BF16)
        v = v_ref[:, h * dv:(h + 1) * dv].astype(BF16)
        v_aug = jnp.concatenate([v, ones_blk], axis=-1)
        i_col = gc[:, h:h + 1]
        i_row = gr[h:h + 1, :]
        b_col = b_cols[:, H + h:H + h + 1]
        b_row = b_rows[H + h:H + h + 1, :]
        g_tot = b_col[L - 1:L, :]
        m_prev = m_ref[h][:, :1]
        dmat = jnp.where(tril, b_col - b_row + i_row, -jnp.inf)
        inter = b_col + m_prev
        m_t = jnp.maximum(inter, jnp.max(dmat, axis=-1, keepdims=True))
        w_intra = jnp.exp(dmat - m_t)
        w_inter = jnp.exp(inter - m_t)
        sw = _nt(q, k) * w_intra
        ct = ct_ref[h]
        num = w_inter * _mm(q, ct.astype(BF16)) + _mm(sw.astype(BF16), v_aug)
        den = num[:, dv:dv + 1]
        h_t = num[:, :dv] / jnp.maximum(jnp.abs(den), jnp.exp(-m_t))
        decay = g_tot - b_col + i_col
        m_new = jnp.maximum(g_tot + m_prev, jnp.max(decay, axis=0, keepdims=True))
        ws = jnp.exp(decay - m_new)
        carry_scale = jnp.exp(g_tot + m_prev - m_new)
        wv = (ws * v_aug.astype(F32)).astype(BF16)
        ct_ref[h] = carry_scale * ct + _tn(k, wv)
        m_ref[h] = jnp.broadcast_to(m_new, (1, LANES))

        hn = _rms_rows(h_t, hg_ref[:, h * dv:(h + 1) * dv])
        osl = slice(h * dv, (h + 1) * dv)
        o_ref[:, osl] = (hn * jax.nn.sigmoid(op_ref[:, osl]) * _silu(gate_ref[:, osl])).astype(o_ref.dtype)


def _mlstm_scan(proj, g_cols, g_rows, b_i, b_f, h_gain, bsz, seq):
    L = MLSTM_CHUNK
    H = MLSTM_HEADS
    nc = seq // L
    m = bsz * seq
    qkw = H * MLSTM_QK_DIM
    vw = H * MLSTM_V_DIM
    bias = jnp.concatenate([b_i, b_f]).astype(F32)
    bias_row = jnp.zeros((1, LANES), F32).at[0, :2 * H].set(bias)
    bias_col = bias.reshape(2 * H, 1)
    row = lambda b, j: b * nc + j
    return pl.pallas_call(
        _mlstm_body,
        grid=(bsz, nc),
        in_specs=[
            pl.BlockSpec((L, qkw), lambda b, j: (row(b, j), 0)),
            pl.BlockSpec((L, qkw), lambda b, j: (row(b, j), 1)),
            pl.BlockSpec((L, vw), lambda b, j: (row(b, j), 1)),
            pl.BlockSpec((L, vw), lambda b, j: (row(b, j), 2)),
            pl.BlockSpec((L, vw), lambda b, j: (row(b, j), 3)),
            pl.BlockSpec((L, LANES), lambda b, j: (row(b, j), 0)),
            pl.BlockSpec((2 * H, L), lambda b, j: (0, row(b, j))),
            pl.BlockSpec((1, LANES), lambda b, j: (0, 0)),
            pl.BlockSpec((2 * H, 1), lambda b, j: (0, 0)),
            pl.BlockSpec((1, vw), lambda b, j: (0, 0)),
        ],
        out_specs=pl.BlockSpec((L, vw), lambda b, j: (row(b, j), 0)),
        out_shape=jax.ShapeDtypeStruct((m, vw), BF16),
        scratch_shapes=[
            pltpu.VMEM((H, MLSTM_QK_DIM, MLSTM_V_DIM + LANES), F32),
            pltpu.VMEM((H, 1, LANES), F32),
        ],
        compiler_params=_params("parallel", "arbitrary"),
        name="mlstm_scan",
    )(proj, proj, proj, proj, proj, g_cols, g_rows, bias_row, bias_col, h_gain.reshape(1, vw))


def _mlstm_layer(x2, bsz, seq, norm, w_in, b_i, b_f, h_gain, w_out):
    proj = _norm_matmul(x2, norm, w_in[:, :MLSTM_MAIN].astype(BF16), tn=512, name="mlstm_in_proj")
    g_cols, g_rows = _gate_proj(x2, norm, w_in[:, MLSTM_MAIN:])
    a = _mlstm_scan(proj, g_cols, g_rows, b_i, b_f, h_gain, bsz, seq)
    return _out_proj(a, w_out.astype(BF16), x2)


def _head_norm_body(x_ref, g_ref, o_ref):
    x = x_ref[...]
    bi = lax.broadcasted_iota(jnp.int32, (LANES, LANES), 0) // HEAD_DIM
    bj = lax.broadcasted_iota(jnp.int32, (LANES, LANES), 1) // HEAD_DIM
    bd_mean = jnp.where(bi == bj, 1.0 / HEAD_DIM, 0.0).astype(F32)
    for c in range(x.shape[1] // LANES):
        xc = x[:, c * LANES:(c + 1) * LANES]
        ms = _mm(xc * xc, bd_mean, HIGHEST)
        o_ref[:, c * LANES:(c + 1) * LANES] = (
            xc * lax.rsqrt(ms + RMS_EPS) * g_ref[:, c * LANES:(c + 1) * LANES]).astype(o_ref.dtype)


def _head_norm(proj, col_block, gain_row):
    m = proj.shape[0]
    tm = min(512, m)
    return pl.pallas_call(
        _head_norm_body,
        grid=(m // tm,),
        in_specs=[
            pl.BlockSpec((tm, D_MODEL), lambda i: (i, col_block)),
            pl.BlockSpec((1, D_MODEL), lambda i: (0, 0)),
        ],
        out_specs=pl.BlockSpec((tm, D_MODEL), lambda i: (i, 0)),
        out_shape=jax.ShapeDtypeStruct((m, D_MODEL), BF16),
        compiler_params=_params("parallel"),
        name="head_norm",
    )(proj, gain_row)


def _fox_forget_body(f_ref, bf_ref, o_ref):
    B = LANES
    seq = f_ref.shape[1]
    lf = jax.nn.log_sigmoid(f_ref[...] + bf_ref[...])
    ti = lax.broadcasted_iota(jnp.int32, (B, B), 0)
    tj = lax.broadcasted_iota(jnp.int32, (B, B), 1)
    triu = (ti <= tj).astype(F32)
    carry = jnp.zeros((lf.shape[0], 1), F32)
    for c in range(seq // B):
        blk = _mm(lf[:, c * B:(c + 1) * B], triu, HIGHEST) + carry
        o_ref[:, c * B:(c + 1) * B] = blk
        carry = blk[:, B - 1:B]


def _fox_forget(f_rows, b_f, bsz, seq):
    H = N_HEADS
    return pl.pallas_call(
        _fox_forget_body,
        grid=(bsz,),
        in_specs=[pl.BlockSpec((H, seq), lambda b: (0, b)), pl.BlockSpec((H, 1), lambda b: (0, 0))],
        out_specs=pl.BlockSpec((H, seq), lambda b: (0, b)),
        out_shape=jax.ShapeDtypeStruct((H, bsz * seq), F32),
        compiler_params=_params("parallel"),
        name="fox_forget",
    )(f_rows, b_f.reshape(H, 1).astype(F32))


def _fox_body(q_ref, k_ref, v_ref, f_ref, gate_ref, o_ref):
    T = q_ref.shape[0]
    i = pl.program_id(1)
    lane = lax.broadcasted_iota(jnp.int32, (T, LANES), 1)
    m0 = lane < HEAD_DIM
    qi = lax.broadcasted_iota(jnp.int32, (T, T), 0)
    kj = lax.broadcasted_iota(jnp.int32, (T, T), 1)
    causal = kj <= qi
    ones = jnp.ones((T, LANES), BF16)

    def pair(p, carry):
        sl = pl.ds(pl.multiple_of(p * LANES, LANES), LANES)
        q2 = q_ref[:, sl]
        outs = []
        for n in range(2):
            mine = m0 if n == 0 else ~m0
            qh = jnp.where(mine, q2, jnp.zeros_like(q2))

            def kv_step(j, st, masked):
                m_prev, acc = st
                ks = pl.ds(pl.multiple_of(j * T, T), T)
                s = _nt(qh, k_ref[ks, sl]) - f_ref[pl.ds(2 * p + n, 1), ks]
                if masked:
                    s = jnp.where(causal, s, -jnp.inf)
                m_new = jnp.maximum(m_prev, jnp.max(s, axis=-1, keepdims=True))
                alpha = jnp.exp(m_prev - m_new)
                pmat = jnp.exp(s - m_new).astype(BF16)
                v_aug = jnp.where(mine, v_ref[ks, sl], ones)
                return m_new, alpha * acc + _mm(pmat, v_aug)

            st = (jnp.full((T, 1), -jnp.inf, F32), jnp.zeros((T, LANES), F32))
            st = lax.fori_loop(0, i, lambda j, st: kv_step(j, st, False), st)
            _, acc = kv_step(i, st, True)
            outs.append(acc / pltpu.roll(acc, HEAD_DIM, axis=1))
        o = jnp.where(m0, outs[0], outs[1])
        o_ref[:, sl] = (o * _silu(gate_ref[:, sl])).astype(o_ref.dtype)
        return carry

    lax.fori_loop(0, N_PAIRS, pair, 0)


def _fox_attention(qn, kn, vb, f_cum, proj, bsz, seq):
    T = min(FOX_BLOCK, seq)
    nq = seq // T
    m = bsz * seq
    return pl.pallas_call(
        _fox_body,
        grid=(bsz, nq),
        in_specs=[
            pl.BlockSpec((T, D_MODEL), lambda b, i: (b * nq + i, 0)),
            pl.BlockSpec((seq, D_MODEL), lambda b, i: (b, 0)),
            pl.BlockSpec((seq, D_MODEL), lambda b, i: (b, 0)),
            pl.BlockSpec((N_HEADS, seq), lambda b, i: (0, b)),
            pl.BlockSpec((T, D_MODEL), lambda b, i: (b * nq + i, 3)),
        ],
        out_specs=pl.BlockSpec((T, D_MODEL), lambda b, i: (b * nq + i, 0)),
        out_shape=jax.ShapeDtypeStruct((m, D_MODEL), BF16),
        compiler_params=_params("parallel", "arbitrary"),
        name="fox_attention",
    )(qn, kn, vb, f_cum, proj)


def _fox_layer(x2, bsz, seq, norm, w_in, b_f, q_gain, k_gain, w_out):
    W = D_MODEL
    proj = _norm_matmul(x2, norm, w_in[:, :4 * W].astype(BF16), tn=512, name="fox_in_proj")
    _, f_rows = _gate_proj(x2, norm, w_in[:, 4 * W:])
    f_cum = _fox_forget(f_rows, b_f, bsz, seq)
    scale = HEAD_DIM ** -0.5
    qn = _head_norm(proj, 0, (jnp.tile(q_gain, N_HEADS) * scale).reshape(1, W).astype(F32))
    kn = _head_norm(proj, 1, jnp.tile(k_gain, N_HEADS).reshape(1, W).astype(F32))
    vb = proj[:, 2 * W:3 * W].astype(BF16)
    a = _fox_attention(qn, kn, vb, f_cum, proj, bsz, seq)
    return _out_proj(a, w_out.astype(BF16), x2)


def kernel(x, rel_bias, swa_norm, swa_w_in, swa_q_gain, swa_k_gain, swa_sinks, swa_w_out, rwkv_norm, rwkv_mix, rwkv_w_in, rwkv_w0, rwkv_w_lora1, rwkv_w_lora2, rwkv_a0, rwkv_a_lora1, rwkv_a_lora2, rwkv_k_k, rwkv_k_a, rwkv_r_k, rwkv_gn_w, rwkv_gn_b, rwkv_w_out, mlstm_norm, mlstm_w_in, mlstm_b_i, mlstm_b_f, mlstm_h_gain, mlstm_w_out, fox_norm, fox_w_in, fox_b_f, fox_q_gain, fox_k_gain, fox_w_out):
    bsz, seq, d = x.shape
    depth = swa_norm.shape[0] + rwkv_norm.shape[0] + mlstm_norm.shape[0] + fox_norm.shape[0]
    x2 = x.reshape(bsz * seq, d)
    for layer in range(depth):
        kind, idx = layer % 4, layer // 4
        if kind == 0:
            x2 = _swa_layer(x2, bsz, seq, rel_bias, swa_norm[idx], swa_w_in[idx], swa_q_gain[idx],
                            swa_k_gain[idx], swa_sinks[idx], swa_w_out[idx])
        elif kind == 1:
            x2 = _rwkv_layer(x2, bsz, seq, rwkv_norm[idx], rwkv_mix[idx], rwkv_w_in[idx],
                             rwkv_w0[idx], rwkv_w_lora1[idx], rwkv_w_lora2[idx], rwkv_a0[idx],
                             rwkv_a_lora1[idx], rwkv_a_lora2[idx], rwkv_k_k[idx], rwkv_k_a[idx],
                             rwkv_r_k[idx], rwkv_gn_w[idx], rwkv_gn_b[idx], rwkv_w_out[idx])
        elif kind == 2:
            x2 = _mlstm_layer(x2, bsz, seq, mlstm_norm[idx], mlstm_w_in[idx], mlstm_b_i[idx],
                              mlstm_b_f[idx], mlstm_h_gain[idx], mlstm_w_out[idx])
        else:
            x2 = _fox_layer(x2, bsz, seq, fox_norm[idx], fox_w_in[idx], fox_b_f[idx],
                            fox_q_gain[idx], fox_k_gain[idx], fox_w_out[idx])
    return x2.reshape(bsz, seq, d)
```

```python
import functools
import math

import jax
import jax.numpy as jnp
from jax import lax
from jax.experimental import pallas as pl
from jax.experimental.pallas import tpu as pltpu

F32 = jnp.float32
BF16 = jnp.bfloat16
HIGHEST = lax.Precision.HIGHEST

D_MODEL = 2048
RMS_EPS = 1e-6
GN_EPS = 64e-5

HEAD_DIM = 64
N_HEADS = D_MODEL // HEAD_DIM
LANES = 128
N_PAIRS = D_MODEL // LANES

SWA_KV_HEADS = 4
SWA_GROUP = N_HEADS // SWA_KV_HEADS
SWA_KV_WIDTH = SWA_KV_HEADS * HEAD_DIM
SWA_BLOCK = 128
REL_BUCKETS = 32
REL_MAX_DIST = 128

RWKV_CHUNK = 64
RWKV_LORA_PAD = 128
RWKV_PAIR_UNROLL = 8

MLSTM_HEADS = 8
MLSTM_V_DIM = 256
MLSTM_QK_DIM = 128
MLSTM_CHUNK = 128
MLSTM_MAIN = 2 * MLSTM_HEADS * MLSTM_QK_DIM + 3 * MLSTM_HEADS * MLSTM_V_DIM

FOX_BLOCK = 128
FOX_KV = 512
FOX_PAIR_UNROLL = 4

VMEM_LIMIT = 56 * 1024 * 1024


def _params(*sem):
    return pltpu.CompilerParams(dimension_semantics=sem, vmem_limit_bytes=VMEM_LIMIT)


def _nt(a, b, precision=None):
    return lax.dot_general(a, b, (((1,), (1,)), ((), ())), precision=precision,
                           preferred_element_type=F32)


def _tn(a, b, precision=None):
    return lax.dot_general(a, b, (((0,), (0,)), ((), ())), precision=precision,
                           preferred_element_type=F32)


def _mm(a, b, precision=None):
    return jnp.dot(a, b, precision=precision, preferred_element_type=F32)


def _rms_rows(x, g):
    return x * lax.rsqrt(jnp.mean(x * x, axis=-1, keepdims=True) + RMS_EPS) * g


def _silu(x):
    return x * jax.nn.sigmoid(x)


def _round_robin(gens):
    results = [None] * len(gens)
    live = list(range(len(gens)))
    while live:
        for n in list(live):
            try:
                next(gens[n])
            except StopIteration as done:
                results[n] = done.value
                live.remove(n)
    return results


def _norm_matmul_body(x_ref, g_ref, w_ref, o_ref, xn_ref):
    @pl.when(pl.program_id(1) == 0)
    def _():
        xn_ref[...] = _rms_rows(x_ref[...], g_ref[...]).astype(BF16)

    o_ref[...] = _mm(xn_ref[...], w_ref[...]).astype(o_ref.dtype)


def _norm_matmul(x, g, w, *, tn, out_dtype=F32, name="norm_matmul"):
    m, k = x.shape
    n = w.shape[1]
    tm = min(512, m)
    return pl.pallas_call(
        _norm_matmul_body,
        grid=(m // tm, n // tn),
        in_specs=[
            pl.BlockSpec((tm, k), lambda i, j: (i, 0)),
            pl.BlockSpec((1, k), lambda i, j: (0, 0)),
            pl.BlockSpec((k, tn), lambda i, j: (0, j)),
        ],
        out_specs=pl.BlockSpec((tm, tn), lambda i, j: (i, j)),
        out_shape=jax.ShapeDtypeStruct((m, n), out_dtype),
        scratch_shapes=[pltpu.VMEM((tm, k), BF16)],
        compiler_params=_params("parallel", "arbitrary"),
        name=name,
    )(x, g.reshape(1, k), w)


def _gate_proj_body(x_ref, g_ref, w_ref, wt_ref, cols_ref, rows_ref):
    xn = _rms_rows(x_ref[...], g_ref[...]).astype(BF16)
    cols_ref[...] = _mm(xn, w_ref[...])
    rows_ref[...] = _nt(wt_ref[...], xn)


def _gate_proj(x, g, w_gates):
    m, k = x.shape
    n_g = w_gates.shape[1]
    tm = min(512, m)
    w_pad = jnp.zeros((k, LANES), BF16).at[:, :n_g].set(w_gates.astype(BF16))
    w_t = w_gates.T.astype(BF16)
    return pl.pallas_call(
        _gate_proj_body,
        grid=(m // tm,),
        in_specs=[
            pl.BlockSpec((tm, k), lambda i: (i, 0)),
            pl.BlockSpec((1, k), lambda i: (0, 0)),
            pl.BlockSpec((k, LANES), lambda i: (0, 0)),
            pl.BlockSpec((n_g, k), lambda i: (0, 0)),
        ],
        out_specs=[
            pl.BlockSpec((tm, LANES), lambda i: (i, 0)),
            pl.BlockSpec((n_g, tm), lambda i: (0, i)),
        ],
        out_shape=[
            jax.ShapeDtypeStruct((m, LANES), F32),
            jax.ShapeDtypeStruct((n_g, m), F32),
        ],
        compiler_params=_params("parallel"),
        name="gate_proj",
    )(x, g.reshape(1, k), w_pad, w_t)


def _out_proj_body(a_ref, w_ref, r_ref, o_ref):
    o_ref[...] = r_ref[...] + _mm(a_ref[...], w_ref[...])


def _out_proj(a, w, resid):
    m, k = a.shape
    n = w.shape[1]
    tm = min(512, m)
    return pl.pallas_call(
        _out_proj_body,
        grid=(m // tm,),
        in_specs=[
            pl.BlockSpec((tm, k), lambda i: (i, 0)),
            pl.BlockSpec((k, n), lambda i: (0, 0)),
            pl.BlockSpec((tm, n), lambda i: (i, 0)),
        ],
        out_specs=pl.BlockSpec((tm, n), lambda i: (i, 0)),
        out_shape=jax.ShapeDtypeStruct((m, n), F32),
        compiler_params=_params("parallel"),
        name="out_proj",
    )(a, w, resid)


def _t5_bucket(dist):
    max_exact = REL_BUCKETS // 2
    d_f = jnp.maximum(dist, 1).astype(F32)
    large = max_exact + (jnp.log(d_f / max_exact) / math.log(REL_MAX_DIST / max_exact)
                         * (REL_BUCKETS - max_exact)).astype(jnp.int32)
    large = jnp.minimum(large, REL_BUCKETS - 1)
    return jnp.where(dist < max_exact, dist, large)


def _swa_body(nb, q_ref, gate_ref, kc_ref, kp_ref, vc_ref, vp_ref, bias_ref, sink_ref,
              qg_ref, kg_ref, o_ref):
    L = SWA_BLOCK
    j = pl.program_id(0) % nb
    qi = lax.broadcasted_iota(jnp.int32, (L, 2 * L), 0)
    kj = lax.broadcasted_iota(jnp.int32, (L, 2 * L), 1)
    dist = qi + L - kj
    valid = (dist >= 0) & (dist < L) & ((kj >= L) | (j > 0))
    scale = HEAD_DIM ** -0.5
    outs = []
    for kh in range(SWA_KV_HEADS):
        ks = slice(kh * HEAD_DIM, (kh + 1) * HEAD_DIM)
        k2 = jnp.concatenate([kp_ref[:, ks], kc_ref[:, ks]], axis=0)
        kn = _rms_rows(k2, kg_ref[...]).astype(BF16)
        v2 = jnp.concatenate([vp_ref[:, ks], vc_ref[:, ks]], axis=0).astype(BF16)
        for g in range(SWA_GROUP):
            h = kh * SWA_GROUP + g
            qh = q_ref[:, h * HEAD_DIM:(h + 1) * HEAD_DIM]
            qn = (_rms_rows(qh, qg_ref[...]) * scale).astype(BF16)
            s = _nt(qn, kn) + bias_ref[h]
            s = jnp.where(valid, s, -jnp.inf)
            sink = sink_ref[h]
            m = jnp.maximum(jnp.max(s, axis=-1, keepdims=True), sink)
            p = jnp.exp(s - m)
            denom = jnp.sum(p, axis=-1, keepdims=True) + jnp.exp(sink - m)
            outs.append(_mm(p.astype(BF16), v2) / denom)
    o = jnp.concatenate(outs, axis=-1)
    o_ref[...] = (o * _silu(gate_ref[...])).astype(o_ref.dtype)


def _swa_attention(proj, bias, sinks, q_gain, k_gain, bsz, seq):
    L = SWA_BLOCK
    nb = seq // L
    m = bsz * seq
    kcol = 2 * D_MODEL // SWA_KV_WIDTH
    cur = lambda r: r
    prev = lambda r: jnp.where(r % nb == 0, r, r - 1)
    return pl.pallas_call(
        functools.partial(_swa_body, nb),
        grid=(bsz * nb,),
        in_specs=[
            pl.BlockSpec((L, D_MODEL), lambda r: (r, 0)),
            pl.BlockSpec((L, D_MODEL), lambda r: (r, 1)),
            pl.BlockSpec((L, SWA_KV_WIDTH), lambda r: (cur(r), kcol)),
            pl.BlockSpec((L, SWA_KV_WIDTH), lambda r: (prev(r), kcol)),
            pl.BlockSpec((L, SWA_KV_WIDTH), lambda r: (cur(r), kcol + 1)),
            pl.BlockSpec((L, SWA_KV_WIDTH), lambda r: (prev(r), kcol + 1)),
            pl.BlockSpec((N_HEADS, L, 2 * L), lambda r: (0, 0, 0)),
            pl.BlockSpec(memory_space=pltpu.SMEM),
            pl.BlockSpec((1, HEAD_DIM), lambda r: (0, 0)),
            pl.BlockSpec((1, HEAD_DIM), lambda r: (0, 0)),
        ],
        out_specs=pl.BlockSpec((L, D_MODEL), lambda r: (r, 0)),
        out_shape=jax.ShapeDtypeStruct((m, D_MODEL), BF16),
        compiler_params=_params("parallel"),
        name="swa_attention",
    )(proj, proj, proj, proj, proj, proj, bias, sinks,
      q_gain.reshape(1, HEAD_DIM), k_gain.reshape(1, HEAD_DIM))


def _swa_layer(x2, bsz, seq, rel_bias, norm, w_in, q_gain, k_gain, sinks, w_out):
    W = D_MODEL
    kvw = SWA_KV_WIDTH
    w_perm = jnp.concatenate([w_in[:, :W], w_in[:, W + 2 * kvw:], w_in[:, W:W + 2 * kvw]], axis=1)
    proj = _norm_matmul(x2, norm, w_perm.astype(BF16), tn=512, name="swa_in_proj")
    L = SWA_BLOCK
    qi = jnp.arange(L)[:, None]
    kj = jnp.arange(2 * L)[None, :]
    dist = qi + L - kj
    bias = rel_bias.astype(F32)[_t5_bucket(jnp.maximum(dist, 0))].transpose(2, 0, 1)
    a = _swa_attention(proj, bias, sinks.astype(F32), q_gain, k_gain, bsz, seq)
    return _out_proj(a, w_out.astype(BF16), x2)


def _shifted_mix(x_ref, xp_ref, g_ref, seq_start):
    h = _rms_rows(x_ref[...], g_ref[...])
    h_prev_row = _rms_rows(xp_ref[7:8, :], g_ref[...])
    h_prev_row = jnp.where(seq_start, 0.0, h_prev_row)
    rolled = pltpu.roll(h, 1, axis=0)
    row = lax.broadcasted_iota(jnp.int32, h.shape, 0)
    shifted = jnp.where(row == 0, h_prev_row, rolled)
    return h, shifted - h


def _rwkv_in_body(tiles_per_seq, x_ref, xp_ref, g_ref, mix_ref, w_ref, o_ref, xm_ref):
    i = pl.program_id(0)

    @pl.when((pl.program_id(1) == 0) & (pl.program_id(2) == 0))
    def _():
        h, xx = _shifted_mix(x_ref, xp_ref, g_ref, i % tiles_per_seq == 0)
        for c in range(4):
            xm_ref[c] = (h + xx * mix_ref[c:c + 1, :]).astype(BF16)

    c = pl.program_id(1)
    o_ref[0] = _mm(xm_ref[c], w_ref[0])


def _rwkv_in_proj(x2, seq, norm, mix8, w4):
    m, k = x2.shape
    tm = min(512, seq)
    tn = 512
    rows8 = tm // 8
    return pl.pallas_call(
        functools.partial(_rwkv_in_body, seq // tm),
        grid=(m // tm, 4, k // tn),
        in_specs=[
            pl.BlockSpec((tm, k), lambda i, c, j: (i, 0)),
            pl.BlockSpec((8, k), lambda i, c, j: (jnp.maximum(i * rows8 - 1, 0), 0)),
            pl.BlockSpec((1, k), lambda i, c, j: (0, 0)),
            pl.BlockSpec((8, k), lambda i, c, j: (0, 0)),
            pl.BlockSpec((1, k, tn), lambda i, c, j: (c, 0, j)),
        ],
        out_specs=pl.BlockSpec((1, tm, tn), lambda i, c, j: (c, i, j)),
        out_shape=jax.ShapeDtypeStruct((4, m, k), F32),
        scratch_shapes=[pltpu.VMEM((4, tm, k), BF16)],
        compiler_params=_params("parallel", "arbitrary", "arbitrary"),
        name="rwkv_in_proj",
    )(x2, x2, norm.reshape(1, k), mix8, w4)


def _rwkv_lora_body(tiles_per_seq, x_ref, xp_ref, g_ref, mix_ref, w1_ref, w2_ref, a1_ref, a2_ref,
                    w0_ref, a0_ref, lw_ref, a_ref):
    i = pl.program_id(0)
    h, xx = _shifted_mix(x_ref, xp_ref, g_ref, i % tiles_per_seq == 0)
    xw = (h + xx * mix_ref[4:5, :]).astype(BF16)
    xa = (h + xx * mix_ref[5:6, :]).astype(BF16)
    t = jnp.tanh(_mm(xw, w1_ref[...])).astype(BF16)
    w_pre = w0_ref[...] + _mm(t, w2_ref[...])
    lw_ref[...] = -jnp.exp(-jax.nn.softplus(-w_pre) - 0.5)
    u = _mm(xa, a1_ref[...]).astype(BF16)
    a_ref[...] = jax.nn.sigmoid(a0_ref[...] + _mm(u, a2_ref[...]))


def _rwkv_lora(x2, seq, norm, mix8, w1, w2, a1, a2, w0, a0):
    m, k = x2.shape
    tm = min(512, seq)
    rows8 = tm // 8
    P = RWKV_LORA_PAD
    full = lambda shape: pl.BlockSpec(shape, lambda i: tuple(0 for _ in shape))
    return pl.pallas_call(
        functools.partial(_rwkv_lora_body, seq // tm),
        grid=(m // tm,),
        in_specs=[
            pl.BlockSpec((tm, k), lambda i: (i, 0)),
            pl.BlockSpec((8, k), lambda i: (jnp.maximum(i * rows8 - 1, 0), 0)),
            full((1, k)), full((8, k)),
            full((k, P)), full((P, k)), full((k, P)), full((P, k)),
            full((1, k)), full((1, k)),
        ],
        out_specs=[pl.BlockSpec((tm, k), lambda i: (i, 0)), pl.BlockSpec((tm, k), lambda i: (i, 0))],
        out_shape=[jax.ShapeDtypeStruct((m, k), F32), jax.ShapeDtypeStruct((m, k), F32)],
        compiler_params=_params("parallel"),
        name="rwkv_lora",
    )(x2, x2, norm.reshape(1, k), mix8, w1, w2, a1, a2, w0.reshape(1, k), a0.reshape(1, k))


def _stack2(z, m0):
    zero = jnp.zeros_like(z)
    return jnp.concatenate([jnp.where(m0, z, zero), jnp.where(m0, zero, z)], axis=0)


def _rwkv_scan_body(r_ref, k_ref, v_ref, g_ref, lw_ref, a_ref, kk_ref, ka_ref, rk_ref,
                    gnw_ref, gnb_ref, o_ref, ht_ref):
    C = RWKV_CHUNK

    @pl.when(pl.program_id(1) == 0)
    def _():
        ht_ref[...] = jnp.zeros_like(ht_ref)

    lane = lax.broadcasted_iota(jnp.int32, (C, LANES), 1)
    tok = lax.broadcasted_iota(jnp.int32, (C, LANES), 0)
    m0 = lane < HEAD_DIM
    col = lane % HEAD_DIM
    strict = tok > col
    incl = tok >= col
    ti = lax.broadcasted_iota(jnp.int32, (C, C), 0)
    tj = lax.broadcasted_iota(jnp.int32, (C, C), 1)
    tri = (ti >= tj).astype(F32)
    bi = lax.broadcasted_iota(jnp.int32, (LANES, LANES), 0) // HEAD_DIM
    bj = lax.broadcasted_iota(jnp.int32, (LANES, LANES), 1) // HEAD_DIM
    bd = bi == bj
    bd_ones = bd.astype(BF16)
    tri = tri.astype(BF16)

    def head_sum(z):
        return _mm(z.astype(BF16), bd_ones)

    def load(p):
        sl = pl.ds(pl.multiple_of(p * LANES, LANES), LANES)
        return (r_ref[0, :, sl], k_ref[0, :, sl], v_ref[0, :, sl], g_ref[0, :, sl], lw_ref[:, sl],
                a_ref[:, sl], kk_ref[:, sl], ka_ref[:, sl], rk_ref[:, sl], gnw_ref[:, sl],
                gnb_ref[:, sl], ht_ref[p])

    def compute(r, k, v, g, lw, a, k_k, k_a, r_k, gn_w, gn_b, ht):
        kkp = k * k_k
        ss = head_sum(kkp * kkp)
        lw_hi = lw.astype(BF16)
        lw_lo = (lw - lw_hi.astype(F32)).astype(BF16)
        cum = _mm(tri, lw_hi) + _mm(tri, lw_lo)
        yield
        kk = kkp / jnp.maximum(jnp.sqrt(ss), 1e-12)
        kp = k * (1.0 + (a - 1.0) * k_a)
        cum_end = cum[C - 1:C, :]
        g_in = jnp.exp(cum)
        g_prev = jnp.exp(cum - lw)
        g_inv = jnp.exp(-cum)
        g_end = jnp.exp(cum_end - cum)
        kka = kk * a
        a_t = -kk * g_prev
        r_t = r * g_in
        b_t = kka * g_inv
        k_t = kp * g_inv

        x1 = jnp.concatenate([a_t, r_t], axis=0).astype(BF16)
        bk2 = jnp.concatenate([_stack2(b_t, m0), _stack2(k_t, m0)], axis=0).astype(BF16)
        gbk = _nt(x1, bk2)
        xh = _nt(x1, ht.astype(BF16))
        yield
        zero = jnp.zeros((C, LANES), F32)
        l_ab = jnp.where(strict, gbk[:C, :LANES], zero)
        l_ak = jnp.where(strict, gbk[:C, LANES:], zero)
        p_rb = jnp.where(incl, gbk[C:, :LANES], zero)
        p_rk = jnp.where(incl, gbk[C:, LANES:], zero)

        t_inv = jnp.where(tok == col, 1.0, zero) + jnp.where(
            (tok // 2 == col // 2) & (tok % 2 == 1) & (col % 2 == 0), l_ab, zero)
        v2 = _stack2(v, m0).astype(BF16)
        rhs = xh[:C] + _mm(l_ak.astype(BF16), v2)
        bonus_dot = head_sum(r * kp * r_k)
        s = 2
        while s < C:
            sel = (tok // (2 * s) == col // (2 * s)) & (tok % (2 * s) >= s) & (col % (2 * s) < s)
            l_s = jnp.where(sel, l_ab, zero)
            tl = _mm(t_inv.astype(BF16), _stack2(l_s, m0).astype(BF16))
            yield
            t_inv = t_inv + _mm(tl.astype(BF16), _stack2(t_inv, m0).astype(BF16))
            yield
            s *= 2

        u = _mm(t_inv.astype(BF16), _stack2(rhs, m0).astype(BF16))
        yield
        u2 = _stack2(u, m0).astype(BF16)
        p_cat = jnp.concatenate([p_rb, p_rk], axis=1).astype(BF16)
        y = xh[C:] + _mm(p_cat, jnp.concatenate([u2, v2], axis=0))
        uv = jnp.concatenate([u, v], axis=0).astype(BF16)
        bk_end = jnp.concatenate([kka * g_end, kp * g_end], axis=0).astype(BF16)
        upd = _tn(uv, bk_end)
        yield
        ht_new = ht * jnp.exp(cum_end) + jnp.where(bd, upd, 0.0)
        mu = head_sum(y) * (1.0 / HEAD_DIM)
        yield
        dy = y - mu
        var = head_sum(dy * dy) * (1.0 / HEAD_DIM)
        yield
        yn = dy * lax.rsqrt(var + GN_EPS) * gn_w + gn_b
        return ((yn + bonus_dot * v) * _silu(g)).astype(o_ref.dtype), ht_new

    def group(gi, carry):
        base = gi * RWKV_PAIR_UNROLL
        ins = [load(base + n) for n in range(RWKV_PAIR_UNROLL)]
        outs = _round_robin([compute(*x) for x in ins])
        for n, (o, ht_new) in enumerate(outs):
            p = base + n
            o_ref[:, pl.ds(pl.multiple_of(p * LANES, LANES), LANES)] = o
            ht_ref[p] = ht_new
        return carry

    lax.fori_loop(0, N_PAIRS // RWKV_PAIR_UNROLL, group, 0)


def _rwkv_scan(rkvg, lw, a, k_k, k_a, r_k, gn_w, gn_b, bsz, seq):
    C = RWKV_CHUNK
    nc = seq // C
    m = bsz * seq
    k = D_MODEL
    part = lambda c: pl.BlockSpec((1, C, k), lambda b, j, c=c: (c, b * nc + j, 0))
    tile = pl.BlockSpec((C, k), lambda b, j: (b * nc + j, 0))
    vec = pl.BlockSpec((1, k), lambda b, j: (0, 0))
    return pl.pallas_call(
        _rwkv_scan_body,
        grid=(bsz, nc),
        in_specs=[part(0), part(1), part(2), part(3), tile, tile, vec, vec, vec, vec, vec],
        out_specs=tile,
        out_shape=jax.ShapeDtypeStruct((m, k), BF16),
        scratch_shapes=[pltpu.VMEM((N_PAIRS, LANES, LANES), F32)],
        compiler_params=_params("parallel", "arbitrary"),
        name="rwkv_scan",
    )(rkvg, rkvg, rkvg, rkvg, lw, a, k_k.reshape(1, k), k_a.reshape(1, k), r_k.reshape(1, k),
      gn_w.reshape(1, k), gn_b.reshape(1, k))


def _rwkv_layer(x2, bsz, seq, norm, mix, w_in, w0, w_lora1, w_lora2, a0, a_lora1, a_lora2,
                k_k, k_a, r_k, gn_w, gn_b, w_out):
    k = D_MODEL
    P = RWKV_LORA_PAD
    mix8 = jnp.zeros((8, k), F32).at[:mix.shape[0]].set(mix)
    pad_cols = lambda w: jnp.zeros((k, P), BF16).at[:, :w.shape[1]].set(w.astype(BF16))
    pad_rows = lambda w: jnp.zeros((P, k), BF16).at[:w.shape[0]].set(w.astype(BF16))
    rkvg = _rwkv_in_proj(x2, seq, norm, mix8, w_in.astype(BF16))
    lw, a = _rwkv_lora(x2, seq, norm, mix8, pad_cols(w_lora1), pad_rows(w_lora2),
                       pad_cols(a_lora1), pad_rows(a_lora2), w0, a0)
    y = _rwkv_scan(rkvg, lw, a, k_k, k_a, r_k, gn_w, gn_b, bsz, seq)
    return _out_proj(y, w_out.astype(BF16), x2)


def _mlstm_body(q_ref, k_ref, v_ref, op_ref, gate_ref, gc_ref, gr_ref, bi_ref, bf_ref, hg_ref,
                o_ref, ct_ref, m_ref):
    L = MLSTM_CHUNK
    H = MLSTM_HEADS
    dk = MLSTM_QK_DIM
    dv = MLSTM_V_DIM

    @pl.when(pl.program_id(1) == 0)
    def _():
        ct_ref[...] = jnp.zeros_like(ct_ref)
        m_ref[...] = jnp.zeros_like(m_ref)

    ti = lax.broadcasted_iota(jnp.int32, (L, L), 0)
    tj = lax.broadcasted_iota(jnp.int32, (L, L), 1)
    tril = ti >= tj
    tri_f = tril.astype(F32)
    gc = gc_ref[...] + bi_ref[...]
    gr = gr_ref[...] + bf_ref[...]
    lane = lax.broadcasted_iota(jnp.int32, gc.shape, 1)
    f_cols = jnp.where((lane >= H) & (lane < 2 * H), jax.nn.log_sigmoid(gc), 0.0)
    b_cols = _mm(tri_f, f_cols, HIGHEST)
    f_rows = jax.nn.log_sigmoid(gr)
    b_rows = _nt(f_rows, tri_f, HIGHEST)
    ones_blk = jnp.ones((L, LANES), BF16)

    for h in range(H):
        q = (q_ref[:, h * dk:(h + 1) * dk] * dk ** -0.5).astype(BF16)
        k = k_ref[:, h * dk:(h + 1) * dk].astype(BF16)
        v = v_ref[:, h * dv:(h + 1) * dv].astype(BF16)
        v_aug = jnp.concatenate([v, ones_blk], axis=-1)
        i_col = gc[:, h:h + 1]
        i_row = gr[h:h + 1, :]
        b_col = b_cols[:, H + h:H + h + 1]
        b_row = b_rows[H + h:H + h + 1, :]
        g_tot = b_col[L - 1:L, :]
        m_prev = m_ref[h][:, :1]
        dmat = jnp.where(tril, b_col - b_row + i_row, -jnp.inf)
        inter = b_col + m_prev
        m_t = jnp.maximum(inter, jnp.max(dmat, axis=-1, keepdims=True))
        w_intra = jnp.exp(dmat - m_t)
        w_inter = jnp.exp(inter - m_t)
        sw = _nt(q, k) * w_intra
        ct = ct_ref[h]
        num = w_inter * _mm(q, ct.astype(BF16)) + _mm(sw.astype(BF16), v_aug)
        den = num[:, dv:dv + 1]
        h_t = num[:, :dv] / jnp.maximum(jnp.abs(den), jnp.exp(-m_t))
        decay = g_tot - b_col + i_col
        m_new = jnp.maximum(g_tot + m_prev, jnp.max(decay, axis=0, keepdims=True))
        ws = jnp.exp(decay - m_new)
        carry_scale = jnp.exp(g_tot + m_prev - m_new)
        wv = (ws * v_aug.astype(F32)).astype(BF16)
        ct_ref[h] = carry_scale * ct + _tn(k, wv)
        m_ref[h] = jnp.broadcast_to(m_new, (1, LANES))

        hn = _rms_rows(h_t, hg_ref[:, h * dv:(h + 1) * dv])
        osl = slice(h * dv, (h + 1) * dv)
        o_ref[:, osl] = (hn * jax.nn.sigmoid(op_ref[:, osl]) * _silu(gate_ref[:, osl])).astype(o_ref.dtype)


def _mlstm_scan(proj, g_cols, g_rows, b_i, b_f, h_gain, bsz, seq):
    L = MLSTM_CHUNK
    H = MLSTM_HEADS
    nc = seq // L
    m = bsz * seq
    qkw = H * MLSTM_QK_DIM
    vw = H * MLSTM_V_DIM
    bias = jnp.concatenate([b_i, b_f]).astype(F32)
    bias_row = jnp.zeros((1, LANES), F32).at[0, :2 * H].set(bias)
    bias_col = bias.reshape(2 * H, 1)
    row = lambda b, j: b * nc + j
    return pl.pallas_call(
        _mlstm_body,
        grid=(bsz, nc),
        in_specs=[
            pl.BlockSpec((L, qkw), lambda b, j: (row(b, j), 0)),
            pl.BlockSpec((L, qkw), lambda b, j: (row(b, j), 1)),
            pl.BlockSpec((L, vw), lambda b, j: (row(b, j), 1)),
            pl.BlockSpec((L, vw), lambda b, j: (row(b, j), 2)),
            pl.BlockSpec((L, vw), lambda b, j: (row(b, j), 3)),
            pl.BlockSpec((L, LANES), lambda b, j: (row(b, j), 0)),
            pl.BlockSpec((2 * H, L), lambda b, j: (0, row(b, j))),
            pl.BlockSpec((1, LANES), lambda b, j: (0, 0)),
            pl.BlockSpec((2 * H, 1), lambda b, j: (0, 0)),
            pl.BlockSpec((1, vw), lambda b, j: (0, 0)),
        ],
        out_specs=pl.BlockSpec((L, vw), lambda b, j: (row(b, j), 0)),
        out_shape=jax.ShapeDtypeStruct((m, vw), BF16),
        scratch_shapes=[
            pltpu.VMEM((H, MLSTM_QK_DIM, MLSTM_V_DIM + LANES), F32),
            pltpu.VMEM((H, 1, LANES), F32),
        ],
        compiler_params=_params("parallel", "arbitrary"),
        name="mlstm_scan",
    )(proj, proj, proj, proj, proj, g_cols, g_rows, bias_row, bias_col, h_gain.reshape(1, vw))


def _mlstm_layer(x2, bsz, seq, norm, w_in, b_i, b_f, h_gain, w_out):
    proj = _norm_matmul(x2, norm, w_in[:, :MLSTM_MAIN].astype(BF16), tn=512, name="mlstm_in_proj")
    g_cols, g_rows = _gate_proj(x2, norm, w_in[:, MLSTM_MAIN:])
    a = _mlstm_scan(proj, g_cols, g_rows, b_i, b_f, h_gain, bsz, seq)
    return _out_proj(a, w_out.astype(BF16), x2)


def _head_norm_body(x_ref, g_ref, o_ref):
    x = x_ref[...]
    bi = lax.broadcasted_iota(jnp.int32, (LANES, LANES), 0) // HEAD_DIM
    bj = lax.broadcasted_iota(jnp.int32, (LANES, LANES), 1) // HEAD_DIM
    bd_mean = jnp.where(bi == bj, 1.0 / HEAD_DIM, 0.0).astype(F32)
    for c in range(x.shape[1] // LANES):
        xc = x[:, c * LANES:(c + 1) * LANES]
        ms = _mm(xc * xc, bd_mean, HIGHEST)
        o_ref[:, c * LANES:(c + 1) * LANES] = (
            xc * lax.rsqrt(ms + RMS_EPS) * g_ref[:, c * LANES:(c + 1) * LANES]).astype(o_ref.dtype)


def _head_norm(proj, col_block, gain_row):
    m = proj.shape[0]
    tm = min(512, m)
    return pl.pallas_call(
        _head_norm_body,
        grid=(m // tm,),
        in_specs=[
            pl.BlockSpec((tm, D_MODEL), lambda i: (i, col_block)),
            pl.BlockSpec((1, D_MODEL), lambda i: (0, 0)),
        ],
        out_specs=pl.BlockSpec((tm, D_MODEL), lambda i: (i, 0)),
        out_shape=jax.ShapeDtypeStruct((m, D_MODEL), BF16),
        compiler_params=_params("parallel"),
        name="head_norm",
    )(proj, gain_row)


def _fox_forget_body(f_ref, bf_ref, o_ref):
    B = LANES
    seq = f_ref.shape[1]
    lf = jax.nn.log_sigmoid(f_ref[...] + bf_ref[...])
    ti = lax.broadcasted_iota(jnp.int32, (B, B), 0)
    tj = lax.broadcasted_iota(jnp.int32, (B, B), 1)
    triu = (ti <= tj).astype(F32)
    carry = jnp.zeros((lf.shape[0], 1), F32)
    for c in range(seq // B):
        blk = _mm(lf[:, c * B:(c + 1) * B], triu, HIGHEST) + carry
        o_ref[:, c * B:(c + 1) * B] = blk
        carry = blk[:, B - 1:B]


def _fox_forget(f_rows, b_f, bsz, seq):
    H = f_rows.shape[0]
    return pl.pallas_call(
        _fox_forget_body,
        grid=(bsz,),
        in_specs=[pl.BlockSpec((H, seq), lambda b: (0, b)), pl.BlockSpec((H, 1), lambda b: (0, 0))],
        out_specs=pl.BlockSpec((H, seq), lambda b: (0, b)),
        out_shape=jax.ShapeDtypeStruct((H, bsz * seq), F32),
        compiler_params=_params("parallel"),
        name="fox_forget",
    )(f_rows, b_f.reshape(H, 1).astype(F32))


def _fox_body(q_ref, k_ref, v_ref, f_ref, gate_ref, o_ref):
    T = q_ref.shape[0]
    KV = min(FOX_KV, k_ref.shape[0])
    G = FOX_PAIR_UNROLL
    i = pl.program_id(1)
    n_full = (i * T) // KV
    assert KV % T == 0
    k0 = lax.broadcasted_iota(jnp.int32, (KV, LANES), 1).astype(F32).astype(BF16) < HEAD_DIM
    t0 =lax.broadcasted_iota(jnp.int32, (T, LANES), 1) < HEAD_DIM
    one = jnp.ones((KV, LANES), BF16)
    zero = jnp.zeros((KV, LANES), BF16)
    ones_bd = jnp.concatenate([jnp.where(k0, one, zero), jnp.where(k0, zero, one)], axis=0)
    q_pos = i * T + lax.broadcasted_iota(jnp.int32, (T, 2 * KV), 0)
    k_off = lax.broadcasted_iota(jnp.int32, (T, 2 * KV), 1) % KV

    def split2(x):
        return jnp.concatenate([jnp.where(k0, x, zero), jnp.where(k0, zero, x)], axis=0)

    def pair_step(p, q2, j, state, masked):
        m_a, m_b, acc = state
        sl = pl.ds(pl.multiple_of(p * LANES, LANES), LANES)
        ks = pl.ds(pl.multiple_of(j * KV, KV), KV)
        s = _nt(q2, split2(k_ref[ks, sl]))
        f2 = jnp.concatenate([f_ref[p, 0:1, ks], f_ref[p, 1:2, ks]], axis=1)
        yield
        s = s - f2
        if masked:
            s = jnp.where(j * KV + k_off <= q_pos, s, -jnp.inf)
        new_a = jnp.maximum(m_a, jnp.max(s[:, :KV], axis=-1, keepdims=True))
        new_b = jnp.maximum(m_b, jnp.max(s[:, KV:], axis=-1, keepdims=True))
        pm = jnp.concatenate([jnp.exp(s[:, :KV] - new_a), jnp.exp(s[:, KV:] - new_b)], axis=1)
        rhs = jnp.concatenate([split2(v_ref[ks, sl]), ones_bd], axis=1)
        pv = _mm(pm.astype(BF16), rhs)
        yield
        alpha = jnp.where(t0, jnp.exp(m_a - new_a), jnp.exp(m_b - new_b))
        acc = acc * jnp.concatenate([alpha, alpha], axis=1) + pv
        return new_a, new_b, acc

    def group(gi, carry):
        pairs = [gi * G + n for n in range(G)]
        qs = [q_ref[:, pl.ds(pl.multiple_of(p * LANES, LANES), LANES)] for p in pairs]

        def kv_step(j, states, masked):
            return tuple(_round_robin(
                [pair_step(p, q2, j, st, masked) for p, q2, st in zip(pairs, qs, states)]))

        init = (jnp.full((T, 1), -jnp.inf, F32), jnp.full((T, 1), -jnp.inf, F32),
                jnp.zeros((T, 2 * LANES), F32))
        states = lax.fori_loop(0, n_full, lambda j, st: kv_step(j, st, False), (init,) * G)
        states = kv_step(n_full, states, True)
        for p, (_, _, acc) in zip(pairs, states):
            sl = pl.ds(pl.multiple_of(p * LANES, LANES), LANES)
            o = acc[:, :LANES] / acc[:, LANES:]
            o_ref[:, sl] = (o * _silu(gate_ref[:, sl])).astype(o_ref.dtype)
        return carry

    lax.fori_loop(0, N_PAIRS // G, group, 0)


def _fox_attention(qn, kn, vb, f_cum, proj, bsz, seq):
    T = min(FOX_BLOCK, seq)
    nq = seq // T
    m = bsz * seq
    return pl.pallas_call(
        _fox_body,
        grid=(bsz, nq),
        in_specs=[
            pl.BlockSpec((T, D_MODEL), lambda b, i: (b * nq + i, 0)),
            pl.BlockSpec((seq, D_MODEL), lambda b, i: (b, 0)),
            pl.BlockSpec((seq, D_MODEL), lambda b, i: (b, 0)),
            pl.BlockSpec((N_PAIRS, 8, seq), lambda b, i: (0, 0, b)),
            pl.BlockSpec((T, D_MODEL), lambda b, i: (b * nq + i, 3)),
        ],
        out_specs=pl.BlockSpec((T, D_MODEL), lambda b, i: (b * nq + i, 0)),
        out_shape=jax.ShapeDtypeStruct((m, D_MODEL), BF16),
        compiler_params=_params("parallel", "arbitrary"),
        name="fox_attention",
    )(qn, kn, vb, f_cum, proj)


def _fox_layer(x2, bsz, seq, norm, w_in, b_f, q_gain, k_gain, w_out):
    W = D_MODEL
    proj = _norm_matmul(x2, norm, w_in[:, :4 * W].astype(BF16), tn=512, name="fox_in_proj")
    k = w_in.shape[0]
    w_f = jnp.zeros((k, N_PAIRS, 8), F32).at[:, :, :2].set(w_in[:, 4 * W:].reshape(k, N_PAIRS, 2))
    b_f8 = jnp.zeros((N_PAIRS, 8), F32).at[:, :2].set(b_f.reshape(N_PAIRS, 2))
    _, f_rows = _gate_proj(x2, norm, w_f.reshape(k, N_PAIRS * 8))
    f_cum = _fox_forget(f_rows, b_f8.reshape(-1), bsz, seq).reshape(N_PAIRS, 8, bsz * seq)
    scale = HEAD_DIM ** -0.5
    qn = _head_norm(proj, 0, (jnp.tile(q_gain, N_HEADS) * scale).reshape(1, W).astype(F32))
    kn = _head_norm(proj, 1, jnp.tile(k_gain, N_HEADS).reshape(1, W).astype(F32))
    vb = proj[:, 2 * W:3 * W].astype(BF16)
    a = _fox_attention(qn, kn, vb, f_cum, proj, bsz, seq)
    return _out_proj(a, w_out.astype(BF16), x2)


def kernel(x, rel_bias, swa_norm, swa_w_in, swa_q_gain, swa_k_gain, swa_sinks, swa_w_out, rwkv_norm, rwkv_mix, rwkv_w_in, rwkv_w0, rwkv_w_lora1, rwkv_w_lora2, rwkv_a0, rwkv_a_lora1, rwkv_a_lora2, rwkv_k_k, rwkv_k_a, rwkv_r_k, rwkv_gn_w, rwkv_gn_b, rwkv_w_out, mlstm_norm, mlstm_w_in, mlstm_b_i, mlstm_b_f, mlstm_h_gain, mlstm_w_out, fox_norm, fox_w_in, fox_b_f, fox_q_gain, fox_k_gain, fox_w_out):
    bsz, seq, d = x.shape
    depth = swa_norm.shape[0] + rwkv_norm.shape[0] + mlstm_norm.shape[0] + fox_norm.shape[0]
    x2 = x.reshape(bsz * seq, d)
    for layer in range(depth):
        kind, idx = layer % 4, layer // 4
        if kind == 0:
            x2 = _swa_layer(x2, bsz, seq, rel_bias, swa_norm[idx], swa_w_in[idx], swa_q_gain[idx],
                            swa_k_gain[idx], swa_sinks[idx], swa_w_out[idx])
        elif kind == 1:
            x2 = _rwkv_layer(x2, bsz, seq, rwkv_norm[idx], rwkv_mix[idx], rwkv_w_in[idx],
                             rwkv_w0[idx], rwkv_w_lora1[idx], rwkv_w_lora2[idx], rwkv_a0[idx],
                             rwkv_a_lora1[idx], rwkv_a_lora2[idx], rwkv_k_k[idx], rwkv_k_a[idx],
                             rwkv_r_k[idx], rwkv_gn_w[idx], rwkv_gn_b[idx], rwkv_w_out[idx])
        elif kind == 2:
            x2 = _mlstm_layer(x2, bsz, seq, mlstm_norm[idx], mlstm_w_in[idx], mlstm_b_i[idx],
                              mlstm_b_f[idx], mlstm_h_gain[idx], mlstm_w_out[idx])
        else:
            x2 = _fox_layer(x2, bsz, seq, fox_norm[idx], fox_w_in[idx], fox_b_f[idx],
                            fox_q_gain[idx], fox_k_gain[idx], fox_w_out[idx])
    return x2.reshape(bsz, seq, d)
```

```python
import functools
import math

import jax
import jax.numpy as jnp
from jax import lax
from jax.experimental import pallas as pl
from jax.experimental.pallas import tpu as pltpu

F32 = jnp.float32
BF16 = jnp.bfloat16
HIGHEST = lax.Precision.HIGHEST

D_MODEL = 2048
RMS_EPS = 1e-6
GN_EPS = 64e-5

HEAD_DIM = 64
N_HEADS = D_MODEL // HEAD_DIM
LANES = 128
N_PAIRS = D_MODEL // LANES

SWA_KV_HEADS = 4
SWA_GROUP = N_HEADS // SWA_KV_HEADS
SWA_KV_WIDTH = SWA_KV_HEADS * HEAD_DIM
SWA_BLOCK = 128
SWA_PAIR_UNROLL = 8
REL_BUCKETS = 32
REL_MAX_DIST = 128

RWKV_CHUNK = 64
RWKV_TM = 512
RWKV_LORA_PAD = 128
RWKV_PAIR_UNROLL = 8

MLSTM_HEADS = 8
MLSTM_V_DIM = 256
MLSTM_QK_DIM = 128
MLSTM_CHUNK = 128
MLSTM_HEAD_UNROLL = 4
MLSTM_MAIN = 2 * MLSTM_HEADS * MLSTM_QK_DIM + 3 * MLSTM_HEADS * MLSTM_V_DIM

FOX_BLOCK = 128
FOX_KV = 512
FOX_PAIR_UNROLL = 4

VMEM_LIMIT = 56 * 1024 * 1024
PROJ_TM = 1024
PROJ_TN_CHOICES = (1536, 1024, 512, 256, 128)


def _params(*sem):
    return pltpu.CompilerParams(dimension_semantics=sem, vmem_limit_bytes=VMEM_LIMIT)


def _nt(a, b, precision=None):
    return lax.dot_general(a, b, (((1,), (1,)), ((), ())), precision=precision,
                           preferred_element_type=F32)


def _tn(a, b, precision=None):
    return lax.dot_general(a, b, (((0,), (0,)), ((), ())), precision=precision,
                           preferred_element_type=F32)


def _mm(a, b, precision=None):
    return jnp.dot(a, b, precision=precision, preferred_element_type=F32)


def _rms_rows(x, g):
    return x * lax.rsqrt(jnp.mean(x * x, axis=-1, keepdims=True) + RMS_EPS) * g


def _silu(x):
    return x * jax.nn.sigmoid(x)


def _round_robin(gens):
    results = [None] * len(gens)
    live = list(range(len(gens)))
    while live:
        for n in list(live):
            try:
                next(gens[n])
            except StopIteration as done:
                results[n] = done.value
                live.remove(n)
    return results


def _norm_matmul_body(x_ref, g_ref, w_ref, o_ref, xn_ref):
    @pl.when(pl.program_id(1) == 0)
    def _():
        xn_ref[...] = _rms_rows(x_ref[...], g_ref[...]).astype(BF16)

    o_ref[...] = _mm(xn_ref[...], w_ref[...]).astype(o_ref.dtype)


def _proj_tiles(m, n):
    tm = min(PROJ_TM, m)
    tn = next(t for t in PROJ_TN_CHOICES if n % t == 0)
    return tm, tn


def _norm_matmul(x, g, w, *, name="norm_matmul"):
    m, k = x.shape
    n = w.shape[1]
    tm, tn = _proj_tiles(m, n)
    out_dtype = BF16
    return pl.pallas_call(
        _norm_matmul_body,
        grid=(m // tm, n // tn),
        in_specs=[
            pl.BlockSpec((tm, k), lambda i, j: (i, 0)),
            pl.BlockSpec((1, k), lambda i, j: (0, 0)),
            pl.BlockSpec((k, tn), lambda i, j: (0, j)),
        ],
        out_specs=pl.BlockSpec((tm, tn), lambda i, j: (i, j)),
        out_shape=jax.ShapeDtypeStruct((m, n), out_dtype),
        scratch_shapes=[pltpu.VMEM((tm, k), BF16)],
        compiler_params=_params("parallel", "arbitrary"),
        name=name,
    )(x, g.reshape(1, k), w)


def _gate_proj_body(x_ref, g_ref, w_ref, wt_ref, cols_ref, rows_ref):
    xn = _rms_rows(x_ref[...], g_ref[...]).astype(BF16)
    cols_ref[...] = _mm(xn, w_ref[...])
    rows_ref[...] = _nt(wt_ref[...], xn)


def _gate_proj(x, g, w_gates):
    m, k = x.shape
    n_g = w_gates.shape[1]
    tm = min(512, m)
    w_pad = jnp.zeros((k, LANES), BF16).at[:, :n_g].set(w_gates.astype(BF16))
    w_t = w_gates.T.astype(BF16)
    return pl.pallas_call(
        _gate_proj_body,
        grid=(m // tm,),
        in_specs=[
            pl.BlockSpec((tm, k), lambda i: (i, 0)),
            pl.BlockSpec((1, k), lambda i: (0, 0)),
            pl.BlockSpec((k, LANES), lambda i: (0, 0)),
            pl.BlockSpec((n_g, k), lambda i: (0, 0)),
        ],
        out_specs=[
            pl.BlockSpec((tm, LANES), lambda i: (i, 0)),
            pl.BlockSpec((n_g, tm), lambda i: (0, i)),
        ],
        out_shape=[
            jax.ShapeDtypeStruct((m, LANES), F32),
            jax.ShapeDtypeStruct((n_g, m), F32),
        ],
        compiler_params=_params("parallel"),
        name="gate_proj",
    )(x, g.reshape(1, k), w_pad, w_t)


def _out_proj_body(a_ref, w_ref, r_ref, o_ref):
    o_ref[...] = r_ref[...] + _mm(a_ref[...], w_ref[...])


def _out_proj(a, w, resid):
    m, k = a.shape
    n = w.shape[1]
    tm = min(512, m)
    return pl.pallas_call(
        _out_proj_body,
        grid=(m // tm,),
        in_specs=[
            pl.BlockSpec((tm, k), lambda i: (i, 0)),
            pl.BlockSpec((k, n), lambda i: (0, 0)),
            pl.BlockSpec((tm, n), lambda i: (i, 0)),
        ],
        out_specs=pl.BlockSpec((tm, n), lambda i: (i, 0)),
        out_shape=jax.ShapeDtypeStruct((m, n), F32),
        compiler_params=_params("parallel"),
        name="out_proj",
    )(a, w, resid)


def _t5_bucket(dist):
    max_exact = REL_BUCKETS // 2
    d_f = jnp.maximum(dist, 1).astype(F32)
    large = max_exact + (jnp.log(d_f / max_exact) / math.log(REL_MAX_DIST / max_exact)
                         * (REL_BUCKETS - max_exact)).astype(jnp.int32)
    large = jnp.minimum(large, REL_BUCKETS - 1)
    return jnp.where(dist < max_exact, dist, large)


def _swa_body(q_ref, gate_ref, kc_ref, kp_ref, vc_ref, vp_ref, bias_ref, sink_ref, qg_ref, kg_ref,
              o_ref):
    L = SWA_BLOCK
    k0 = lax.broadcasted_iota(jnp.int32, (2 * L, LANES), 1) < HEAD_DIM
    t0 = lax.broadcasted_iota(jnp.int32, (L, LANES), 1) < HEAD_DIM
    bi = lax.broadcasted_iota(jnp.int32, (LANES, LANES), 0) // HEAD_DIM
    bj = lax.broadcasted_iota(jnp.int32, (LANES, LANES), 1) // HEAD_DIM
    bd_ones = (bi == bj).astype(BF16)
    zero = jnp.zeros((2 * L, LANES), F32)
    one = jnp.ones((2 * L, LANES), F32)
    ones_bd = jnp.concatenate([jnp.where(k0, one, zero), jnp.where(k0, zero, one)], axis=0).astype(BF16)

    def head_mean_sq(z):
        return _mm((z * z).astype(BF16), bd_ones) * (1.0 / HEAD_DIM)

    def both_halves(x, head_in_low_lanes):
        if head_in_low_lanes:
            lo = jnp.where(k0, x, zero)
            hi = pltpu.roll(lo, HEAD_DIM, axis=1)
        else:
            hi = jnp.where(k0, zero, x)
            lo = pltpu.roll(hi, HEAD_DIM, axis=1)
        return jnp.concatenate([lo, hi], axis=0)

    k_all = jnp.concatenate([kp_ref[...], kc_ref[...]], axis=0).astype(F32)
    v_all = jnp.concatenate([vp_ref[...], vc_ref[...]], axis=0).astype(F32)
    kbd, rhs = [], []
    for t in range(SWA_KV_WIDTH // LANES):
        kt = k_all[:, t * LANES:(t + 1) * LANES]
        kn = kt * lax.rsqrt(head_mean_sq(kt) + RMS_EPS) * kg_ref[...]
        vt = v_all[:, t * LANES:(t + 1) * LANES]
        for low in (True, False):
            kbd.append(both_halves(kn, low).astype(BF16))
            rhs.append(jnp.concatenate([both_halves(vt, low).astype(BF16), ones_bd], axis=1))

    def pair(p):
        kh = p // (SWA_GROUP // 2)
        sl = slice(p * LANES, (p + 1) * LANES)
        q2 = q_ref[:, sl].astype(F32)
        ms = head_mean_sq(q2)
        yield
        qn = (q2 * lax.rsqrt(ms + RMS_EPS) * qg_ref[...]).astype(BF16)
        s = _nt(qn, kbd[kh]) + bias_ref[0, p]
        yield
        sink_a = sink_ref[2 * p]
        sink_b = sink_ref[2 * p + 1]
        m_a = jnp.maximum(jnp.max(s[:, :2 * L], axis=-1, keepdims=True), sink_a)
        m_b = jnp.maximum(jnp.max(s[:, 2 * L:], axis=-1, keepdims=True), sink_b)
        pm = jnp.concatenate([jnp.exp(s[:, :2 * L] - m_a), jnp.exp(s[:, 2 * L:] - m_b)], axis=1)
        pv = _mm(pm.astype(BF16), rhs[kh])
        yield
        sink_t = jnp.where(t0, jnp.exp(sink_a - m_a), jnp.exp(sink_b - m_b))
        o = pv[:, :LANES] / (pv[:, LANES:] + sink_t)
        return (o * _silu(gate_ref[:, sl].astype(F32))).astype(o_ref.dtype)

    for g0 in range(0, N_PAIRS, SWA_PAIR_UNROLL):
        outs = _round_robin([pair(p) for p in range(g0, g0 + SWA_PAIR_UNROLL)])
        for n, o in enumerate(outs):
            o_ref[:, (g0 + n) * LANES:(g0 + n + 1) * LANES] = o


def _swa_attention(proj, bias, sinks, q_gain2, k_gain2, bsz, seq):
    L = SWA_BLOCK
    nb = seq // L
    m = bsz * seq
    kcol = 2 * D_MODEL // SWA_KV_WIDTH
    prev = lambda r: jnp.where(r % nb == 0, r, r - 1)
    return pl.pallas_call(
        _swa_body,
        grid=(bsz * nb,),
        in_specs=[
            pl.BlockSpec((L, D_MODEL), lambda r: (r, 0)),
            pl.BlockSpec((L, D_MODEL), lambda r: (r, 1)),
            pl.BlockSpec((L, SWA_KV_WIDTH), lambda r: (r, kcol)),
            pl.BlockSpec((L, SWA_KV_WIDTH), lambda r: (prev(r), kcol)),
            pl.BlockSpec((L, SWA_KV_WIDTH), lambda r: (r, kcol + 1)),
            pl.BlockSpec((L, SWA_KV_WIDTH), lambda r: (prev(r), kcol + 1)),
            pl.BlockSpec((1, N_PAIRS, L, 4 * L), lambda r: (jnp.where(r % nb == 0, 0, 1), 0, 0, 0)),
            pl.BlockSpec(memory_space=pltpu.SMEM),
            pl.BlockSpec((1, LANES), lambda r: (0, 0)),
            pl.BlockSpec((1, LANES), lambda r: (0, 0)),
        ],
        out_specs=pl.BlockSpec((L, D_MODEL), lambda r: (r, 0)),
        out_shape=jax.ShapeDtypeStruct((m, D_MODEL), BF16),
        compiler_params=_params("parallel"),
        name="swa_attention",
    )(proj, proj, proj, proj, proj, proj, bias, sinks, q_gain2, k_gain2)


def _swa_bias_tables(rel_bias):
    L = SWA_BLOCK
    qi = jnp.arange(L)[:, None]
    kj = jnp.arange(2 * L)[None, :]
    dist = qi + L - kj
    in_band = (dist >= 0) & (dist < L)
    bias = rel_bias.astype(F32)[_t5_bucket(jnp.maximum(dist, 0))].transpose(2, 0, 1)
    general = jnp.where(in_band[None], bias, -jnp.inf)
    first = jnp.where((in_band & (kj >= L))[None], bias, -jnp.inf)
    pairs = lambda t: t.reshape(N_PAIRS, 2, L, 2 * L).transpose(0, 2, 1, 3).reshape(N_PAIRS, L, 4 * L)
    return jnp.stack([pairs(first), pairs(general)])


def _swa_layer(x2, bsz, seq, rel_bias, norm, w_in, q_gain, k_gain, sinks, w_out):
    W = D_MODEL
    kvw = SWA_KV_WIDTH
    w_perm = jnp.concatenate([w_in[:, :W], w_in[:, W + 2 * kvw:], w_in[:, W:W + 2 * kvw]], axis=1)
    proj = _norm_matmul(x2, norm, w_perm.astype(BF16), name="swa_in_proj")
    q_gain2 = (jnp.tile(q_gain, 2) * HEAD_DIM ** -0.5).reshape(1, LANES).astype(F32)
    k_gain2 = jnp.tile(k_gain, 2).reshape(1, LANES).astype(F32)
    a = _swa_attention(proj, _swa_bias_tables(rel_bias), sinks.astype(F32), q_gain2, k_gain2, bsz, seq)
    return _out_proj(a, w_out.astype(BF16), x2)


def _shifted_mix(x_ref, xp_ref, g_ref, seq_start):
    h = _rms_rows(x_ref[...], g_ref[...])
    h_prev_row = _rms_rows(xp_ref[7:8, :], g_ref[...])
    h_prev_row = jnp.where(seq_start, 0.0, h_prev_row)
    rolled = pltpu.roll(h, 1, axis=0)
    row = lax.broadcasted_iota(jnp.int32, h.shape, 0)
    shifted = jnp.where(row == 0, h_prev_row, rolled)
    return h, shifted - h


def _rwkv_in_body(tiles_per_seq, x_ref, xp_ref, g_ref, mix_ref, w_ref, o_ref, xm_ref):
    i = pl.program_id(0)

    @pl.when((pl.program_id(1) == 0) & (pl.program_id(2) == 0))
    def _():
        h, xx = _shifted_mix(x_ref, xp_ref, g_ref, i % tiles_per_seq == 0)
        for c in range(4):
            xm_ref[c] = (h + xx * mix_ref[c:c + 1, :]).astype(BF16)

    c = pl.program_id(1)
    o_ref[0] = _mm(xm_ref[c], w_ref[0]).astype(o_ref.dtype)


def _rwkv_in_proj(x2, seq, norm, mix8, w4):
    m, k = x2.shape
    tm = min(RWKV_TM, seq)
    tn = k
    rows8 = tm // 8
    return pl.pallas_call(
        functools.partial(_rwkv_in_body, seq // tm),
        grid=(m // tm, 4, k // tn),
        in_specs=[
            pl.BlockSpec((tm, k), lambda i, c, j: (i, 0)),
            pl.BlockSpec((8, k), lambda i, c, j: (jnp.maximum(i * rows8 - 1, 0), 0)),
            pl.BlockSpec((1, k), lambda i, c, j: (0, 0)),
            pl.BlockSpec((8, k), lambda i, c, j: (0, 0)),
            pl.BlockSpec((1, k, tn), lambda i, c, j: (c, 0, j)),
        ],
        out_specs=pl.BlockSpec((1, tm, tn), lambda i, c, j: (c, i, j)),
        out_shape=jax.ShapeDtypeStruct((4, m, k), BF16),
        scratch_shapes=[pltpu.VMEM((4, tm, k), BF16)],
        compiler_params=_params("parallel", "arbitrary", "arbitrary"),
        name="rwkv_in_proj",
    )(x2, x2, norm.reshape(1, k), mix8, w4)


def _rwkv_lora_body(tiles_per_seq, x_ref, xp_ref, g_ref, mix_ref, w1_ref, w2_ref, a1_ref, a2_ref,
                    w0_ref, a0_ref, lw_ref, a_ref):
    i = pl.program_id(0)
    h, xx = _shifted_mix(x_ref, xp_ref, g_ref, i % tiles_per_seq == 0)
    xw = (h + xx * mix_ref[4:5, :]).astype(BF16)
    xa = (h + xx * mix_ref[5:6, :]).astype(BF16)
    t = jnp.tanh(_mm(xw, w1_ref[...])).astype(BF16)
    w_pre = w0_ref[...] + _mm(t, w2_ref[...])
    lw_ref[...] = -jnp.exp(-jax.nn.softplus(-w_pre) - 0.5)
    u = _mm(xa, a1_ref[...]).astype(BF16)
    a_ref[...] = jax.nn.sigmoid(a0_ref[...] + _mm(u, a2_ref[...]))


def _rwkv_lora(x2, seq, norm, mix8, w1, w2, a1, a2, w0, a0):
    m, k = x2.shape
    tm = min(RWKV_TM, seq)
    rows8 = tm // 8
    P = RWKV_LORA_PAD
    full = lambda shape: pl.BlockSpec(shape, lambda i: tuple(0 for _ in shape))
    return pl.pallas_call(
        functools.partial(_rwkv_lora_body, seq // tm),
        grid=(m // tm,),
        in_specs=[
            pl.BlockSpec((tm, k), lambda i: (i, 0)),
            pl.BlockSpec((8, k), lambda i: (jnp.maximum(i * rows8 - 1, 0), 0)),
            full((1, k)), full((8, k)),
            full((k, P)), full((P, k)), full((k, P)), full((P, k)),
            full((1, k)), full((1, k)),
        ],
        out_specs=[pl.BlockSpec((tm, k), lambda i: (i, 0)), pl.BlockSpec((tm, k), lambda i: (i, 0))],
        out_shape=[jax.ShapeDtypeStruct((m, k), F32), jax.ShapeDtypeStruct((m, k), F32)],
        compiler_params=_params("parallel"),
        name="rwkv_lora",
    )(x2, x2, norm.reshape(1, k), mix8, w1, w2, a1, a2, w0.reshape(1, k), a0.reshape(1, k))


def _stack2(z, m0):
    zero = jnp.zeros_like(z)
    return jnp.concatenate([jnp.where(m0, z, zero), jnp.where(m0, zero, z)], axis=0)


def _rwkv_scan_body(r_ref, k_ref, v_ref, g_ref, lw_ref, a_ref, kk_ref, ka_ref, rk_ref,
                    gnw_ref, gnb_ref, o_ref, ht_ref):
    C = RWKV_CHUNK

    @pl.when(pl.program_id(1) == 0)
    def _():
        ht_ref[...] = jnp.zeros_like(ht_ref)

    lane = lax.broadcasted_iota(jnp.int32, (C, LANES), 1)
    tok = lax.broadcasted_iota(jnp.int32, (C, LANES), 0)
    m0 = lane < HEAD_DIM
    col = lane % HEAD_DIM
    strict = tok > col
    incl = tok >= col
    ti = lax.broadcasted_iota(jnp.int32, (C, C), 0)
    tj = lax.broadcasted_iota(jnp.int32, (C, C), 1)
    tri = (ti >= tj).astype(F32)
    bi = lax.broadcasted_iota(jnp.int32, (LANES, LANES), 0) // HEAD_DIM
    bj = lax.broadcasted_iota(jnp.int32, (LANES, LANES), 1) // HEAD_DIM
    bd = bi == bj
    bd_ones = bd.astype(BF16)
    tri = tri.astype(BF16)

    def head_sum(z):
        return _mm(z.astype(BF16), bd_ones)

    def load(p):
        sl = pl.ds(pl.multiple_of(p * LANES, LANES), LANES)
        f32 = lambda ref: ref[0, :, sl].astype(F32)
        return (f32(r_ref), f32(k_ref), f32(v_ref), f32(g_ref), lw_ref[:, sl],
                a_ref[:, sl], kk_ref[:, sl], ka_ref[:, sl], rk_ref[:, sl], gnw_ref[:, sl],
                gnb_ref[:, sl], ht_ref[p])

    def compute(r, k, v, g, lw, a, k_k, k_a, r_k, gn_w, gn_b, ht):
        kkp = k * k_k
        ss = head_sum(kkp * kkp)
        lw_hi = lw.astype(BF16)
        lw_lo = (lw - lw_hi.astype(F32)).astype(BF16)
        cum = _mm(tri, lw_hi) + _mm(tri, lw_lo)
        yield
        kk = kkp / jnp.maximum(jnp.sqrt(ss), 1e-12)
        kp = k * (1.0 + (a - 1.0) * k_a)
        cum_end = cum[C - 1:C, :]
        g_in = jnp.exp(cum)
        g_prev = jnp.exp(cum - lw)
        g_inv = jnp.exp(-cum)
        g_end = jnp.exp(cum_end - cum)
        kka = kk * a
        a_t = -kk * g_prev
        r_t = r * g_in
        b_t = kka * g_inv
        k_t = kp * g_inv

        x1 = jnp.concatenate([a_t, r_t], axis=0).astype(BF16)
        bk2 = jnp.concatenate([_stack2(b_t, m0), _stack2(k_t, m0)], axis=0).astype(BF16)
        gbk = _nt(x1, bk2)
        xh = _nt(x1, ht.astype(BF16))
        yield
        zero = jnp.zeros((C, LANES), F32)
        l_ab = jnp.where(strict, gbk[:C, :LANES], zero)
        l_ak = jnp.where(strict, gbk[:C, LANES:], zero)
        p_rb = jnp.where(incl, gbk[C:, :LANES], zero)
        p_rk = jnp.where(incl, gbk[C:, LANES:], zero)

        t_inv = jnp.where(tok == col, 1.0, zero) + jnp.where(
            (tok // 2 == col // 2) & (tok % 2 == 1) & (col % 2 == 0), l_ab, zero)
        v2 = _stack2(v, m0).astype(BF16)
        rhs = xh[:C] + _mm(l_ak.astype(BF16), v2)
        bonus_dot = head_sum(r * kp * r_k)
        s = 2
        while s < C:
            sel = (tok // (2 * s) == col // (2 * s)) & (tok % (2 * s) >= s) & (col % (2 * s) < s)
            l_s = jnp.where(sel, l_ab, zero)
            tl = _mm(t_inv.astype(BF16), _stack2(l_s, m0).astype(BF16))
            yield
            t_inv = t_inv + _mm(tl.astype(BF16), _stack2(t_inv, m0).astype(BF16))
            yield
            s *= 2

        u = _mm(t_inv.astype(BF16), _stack2(rhs, m0).astype(BF16))
        yield
        u2 = _stack2(u, m0).astype(BF16)
        p_cat = jnp.concatenate([p_rb, p_rk], axis=1).astype(BF16)
        y = xh[C:] + _mm(p_cat, jnp.concatenate([u2, v2], axis=0))
        uv = jnp.concatenate([u, v], axis=0).astype(BF16)
        bk_end = jnp.concatenate([kka * g_end, kp * g_end], axis=0).astype(BF16)
        upd = _tn(uv, bk_end)
        yield
        ht_new = ht * jnp.exp(cum_end) + jnp.where(bd, upd, 0.0)
        mu = head_sum(y) * (1.0 / HEAD_DIM)
        yield
        dy = y - mu
        var = head_sum(dy * dy) * (1.0 / HEAD_DIM)
        yield
        yn = dy * lax.rsqrt(var + GN_EPS) * gn_w + gn_b
        return ((yn + bonus_dot * v) * _silu(g)).astype(o_ref.dtype), ht_new

    def group(gi, carry):
        base = gi * RWKV_PAIR_UNROLL
        ins = [load(base + n) for n in range(RWKV_PAIR_UNROLL)]
        outs = _round_robin([compute(*x) for x in ins])
        for n, (o, ht_new) in enumerate(outs):
            p = base + n
            o_ref[:, pl.ds(pl.multiple_of(p * LANES, LANES), LANES)] = o
            ht_ref[p] = ht_new
        return carry

    lax.fori_loop(0, N_PAIRS // RWKV_PAIR_UNROLL, group, 0)


def _rwkv_scan(rkvg, lw, a, k_k, k_a, r_k, gn_w, gn_b, bsz, seq):
    C = RWKV_CHUNK
    nc = seq // C
    m = bsz * seq
    k = D_MODEL
    part = lambda c: pl.BlockSpec((1, C, k), lambda b, j, c=c: (c, b * nc + j, 0))
    tile = pl.BlockSpec((C, k), lambda b, j: (b * nc + j, 0))
    vec = pl.BlockSpec((1, k), lambda b, j: (0, 0))
    return pl.pallas_call(
        _rwkv_scan_body,
        grid=(bsz, nc),
        in_specs=[part(0), part(1), part(2), part(3), tile, tile, vec, vec, vec, vec, vec],
        out_specs=tile,
        out_shape=jax.ShapeDtypeStruct((m, k), BF16),
        scratch_shapes=[pltpu.VMEM((N_PAIRS, LANES, LANES), F32)],
        compiler_params=_params("parallel", "arbitrary"),
        name="rwkv_scan",
    )(rkvg, rkvg, rkvg, rkvg, lw, a, k_k.reshape(1, k), k_a.reshape(1, k), r_k.reshape(1, k),
      gn_w.reshape(1, k), gn_b.reshape(1, k))


def _rwkv_layer(x2, bsz, seq, norm, mix, w_in, w0, w_lora1, w_lora2, a0, a_lora1, a_lora2,
                k_k, k_a, r_k, gn_w, gn_b, w_out):
    k = D_MODEL
    P = RWKV_LORA_PAD
    mix8 = jnp.zeros((8, k), F32).at[:mix.shape[0]].set(mix)
    pad_cols = lambda w: jnp.zeros((k, P), BF16).at[:, :w.shape[1]].set(w.astype(BF16))
    pad_rows = lambda w: jnp.zeros((P, k), BF16).at[:w.shape[0]].set(w.astype(BF16))
    rkvg = _rwkv_in_proj(x2, seq, norm, mix8, w_in.astype(BF16))
    lw, a = _rwkv_lora(x2, seq, norm, mix8, pad_cols(w_lora1), pad_rows(w_lora2),
                       pad_cols(a_lora1), pad_rows(a_lora2), w0, a0)
    y = _rwkv_scan(rkvg, lw, a, k_k, k_a, r_k, gn_w, gn_b, bsz, seq)
    return _out_proj(y, w_out.astype(BF16), x2)


def _mlstm_body(q_ref, k_ref, v_ref, op_ref, gate_ref, gc_ref, gr_ref, bi_ref, bf_ref, hg_ref,
                o_ref, ct_ref, m_ref):
    L = MLSTM_CHUNK
    H = MLSTM_HEADS
    dk = MLSTM_QK_DIM
    dv = MLSTM_V_DIM

    @pl.when(pl.program_id(1) == 0)
    def _():
        ct_ref[...] = jnp.zeros_like(ct_ref)
        m_ref[...] = jnp.zeros_like(m_ref)

    ti = lax.broadcasted_iota(jnp.int32, (L, L), 0)
    tj = lax.broadcasted_iota(jnp.int32, (L, L), 1)
    tril = ti >= tj
    tri_f = tril.astype(F32)
    gc = gc_ref[...] + bi_ref[...]
    gr = gr_ref[...] + bf_ref[...]
    lane = lax.broadcasted_iota(jnp.int32, gc.shape, 1)
    f_cols = jnp.where((lane >= H) & (lane < 2 * H), jax.nn.log_sigmoid(gc), 0.0)
    b_cols = _mm(tri_f, f_cols, HIGHEST)
    f_rows = jax.nn.log_sigmoid(gr)
    b_rows = _nt(f_rows, tri_f, HIGHEST)
    ones_blk = jnp.ones((L, LANES), BF16)

    def head(h):
        q = q_ref[:, h * dk:(h + 1) * dk]
        k = k_ref[:, h * dk:(h + 1) * dk]
        v = v_ref[:, h * dv:(h + 1) * dv]
        v_aug = jnp.concatenate([v, ones_blk], axis=-1)
        ct = ct_ref[h]
        m_prev = m_ref[h][:, :1]
        qk = _nt(q, k)
        q_ct = _mm(q, ct.astype(BF16))
        yield
        i_col = gc[:, h:h + 1]
        i_row = gr[h:h + 1, :]
        b_col = b_cols[:, H + h:H + h + 1]
        b_row = b_rows[H + h:H + h + 1, :]
        g_tot = b_col[L - 1:L, :]
        dmat = jnp.where(tril, b_col - b_row + i_row, -jnp.inf)
        inter = b_col + m_prev
        m_t = jnp.maximum(inter, jnp.max(dmat, axis=-1, keepdims=True))
        w_intra = jnp.exp(dmat - m_t)
        w_inter = jnp.exp(inter - m_t)
        sw = qk * w_intra
        num = (w_inter * q_ct + _mm(sw.astype(BF16), v_aug)) * dk ** -0.5
        decay = g_tot - b_col + i_col
        m_new = jnp.maximum(g_tot + m_prev, jnp.max(decay, axis=0, keepdims=True))
        ws = jnp.exp(decay - m_new)
        carry_scale = jnp.exp(g_tot + m_prev - m_new)
        wv = (ws * v_aug.astype(F32)).astype(BF16)
        upd = _tn(k, wv)
        yield
        den = num[:, dv:dv + 1]
        h_t = num[:, :dv] / jnp.maximum(jnp.abs(den), jnp.exp(-m_t))
        hn = _rms_rows(h_t, hg_ref[:, h * dv:(h + 1) * dv])
        osl = slice(h * dv, (h + 1) * dv)
        out = hn * jax.nn.sigmoid(op_ref[:, osl].astype(F32)) * _silu(gate_ref[:, osl].astype(F32))
        return out.astype(o_ref.dtype), carry_scale * ct + upd, jnp.broadcast_to(m_new, (1, LANES))

    for h0 in range(0, H, MLSTM_HEAD_UNROLL):
        heads = range(h0, h0 + MLSTM_HEAD_UNROLL)
        for h, (out, ct_new, m_new) in zip(heads, _round_robin([head(h) for h in heads])):
            o_ref[:, h * dv:(h + 1) * dv] = out
            ct_ref[h] = ct_new
            m_ref[h] = m_new


def _mlstm_scan(proj, g_cols, g_rows, b_i, b_f, h_gain, bsz, seq):
    L = MLSTM_CHUNK
    H = MLSTM_HEADS
    nc = seq // L
    m = bsz * seq
    qkw = H * MLSTM_QK_DIM
    vw = H * MLSTM_V_DIM
    bias = jnp.concatenate([b_i, b_f]).astype(F32)
    bias_row = jnp.zeros((1, LANES), F32).at[0, :2 * H].set(bias)
    bias_col = bias.reshape(2 * H, 1)
    row = lambda b, j: b * nc + j
    return pl.pallas_call(
        _mlstm_body,
        grid=(bsz, nc),
        in_specs=[
            pl.BlockSpec((L, qkw), lambda b, j: (row(b, j), 0)),
            pl.BlockSpec((L, qkw), lambda b, j: (row(b, j), 1)),
            pl.BlockSpec((L, vw), lambda b, j: (row(b, j), 1)),
            pl.BlockSpec((L, vw), lambda b, j: (row(b, j), 2)),
            pl.BlockSpec((L, vw), lambda b, j: (row(b, j), 3)),
            pl.BlockSpec((L, LANES), lambda b, j: (row(b, j), 0)),
            pl.BlockSpec((2 * H, L), lambda b, j: (0, row(b, j))),
            pl.BlockSpec((1, LANES), lambda b, j: (0, 0)),
            pl.BlockSpec((2 * H, 1), lambda b, j: (0, 0)),
            pl.BlockSpec((1, vw), lambda b, j: (0, 0)),
        ],
        out_specs=pl.BlockSpec((L, vw), lambda b, j: (row(b, j), 0)),
        out_shape=jax.ShapeDtypeStruct((m, vw), BF16),
        scratch_shapes=[
            pltpu.VMEM((H, MLSTM_QK_DIM, MLSTM_V_DIM + LANES), F32),
            pltpu.VMEM((H, 1, LANES), F32),
        ],
        compiler_params=_params("parallel", "arbitrary"),
        name="mlstm_scan",
    )(proj, proj, proj, proj, proj, g_cols, g_rows, bias_row, bias_col, h_gain.reshape(1, vw))


def _mlstm_layer(x2, bsz, seq, norm, w_in, b_i, b_f, h_gain, w_out):
    proj = _norm_matmul(x2, norm, w_in[:, :MLSTM_MAIN].astype(BF16), name="mlstm_in_proj")
    g_cols, g_rows = _gate_proj(x2, norm, w_in[:, MLSTM_MAIN:])
    a = _mlstm_scan(proj, g_cols, g_rows, b_i, b_f, h_gain, bsz, seq)
    return _out_proj(a, w_out.astype(BF16), x2)


def _head_norm_body(x_ref, g_ref, o_ref):
    x = x_ref[...].astype(F32)
    bi = lax.broadcasted_iota(jnp.int32, (LANES, LANES), 0) // HEAD_DIM
    bj = lax.broadcasted_iota(jnp.int32, (LANES, LANES), 1) // HEAD_DIM
    bd_mean = jnp.where(bi == bj, 1.0 / HEAD_DIM, 0.0).astype(BF16)
    for c in range(x.shape[1] // LANES):
        xc = x[:, c * LANES:(c + 1) * LANES]
        ms = _mm((xc * xc).astype(BF16), bd_mean)
        o_ref[:, c * LANES:(c + 1) * LANES] = (
            xc * lax.rsqrt(ms + RMS_EPS) * g_ref[:, c * LANES:(c + 1) * LANES]).astype(o_ref.dtype)


def _head_norm(proj, col_block, gain_row):
    m = proj.shape[0]
    tm = min(512, m)
    return pl.pallas_call(
        _head_norm_body,
        grid=(m // tm,),
        in_specs=[
            pl.BlockSpec((tm, D_MODEL), lambda i: (i, col_block)),
            pl.BlockSpec((1, D_MODEL), lambda i: (0, 0)),
        ],
        out_specs=pl.BlockSpec((tm, D_MODEL), lambda i: (i, 0)),
        out_shape=jax.ShapeDtypeStruct((m, D_MODEL), BF16),
        compiler_params=_params("parallel"),
        name="head_norm",
    )(proj, gain_row)


def _fox_forget_body(f_ref, bf_ref, o_ref):
    B = LANES
    seq = f_ref.shape[1]
    lf = jax.nn.log_sigmoid(f_ref[...] + bf_ref[...])
    ti = lax.broadcasted_iota(jnp.int32, (B, B), 0)
    tj = lax.broadcasted_iota(jnp.int32, (B, B), 1)
    triu = (ti <= tj).astype(F32)
    carry = jnp.zeros((lf.shape[0], 1), F32)
    for c in range(seq // B):
        blk = _mm(lf[:, c * B:(c + 1) * B], triu, HIGHEST) + carry
        o_ref[:, c * B:(c + 1) * B] = blk
        carry = blk[:, B - 1:B]


def _fox_forget(f_rows, b_f, bsz, seq):
    H = f_rows.shape[0]
    return pl.pallas_call(
        _fox_forget_body,
        grid=(bsz,),
        in_specs=[pl.BlockSpec((H, seq), lambda b: (0, b)), pl.BlockSpec((H, 1), lambda b: (0, 0))],
        out_specs=pl.BlockSpec((H, seq), lambda b: (0, b)),
        out_shape=jax.ShapeDtypeStruct((H, bsz * seq), F32),
        compiler_params=_params("parallel"),
        name="fox_forget",
    )(f_rows, b_f.reshape(H, 1).astype(F32))


def _fox_body(q_ref, k_ref, v_ref, f_ref, gate_ref, o_ref):
    T = q_ref.shape[0]
    KV = min(FOX_KV, k_ref.shape[0])
    G = FOX_PAIR_UNROLL
    i = pl.program_id(1)
    n_full = (i * T) // KV
    assert KV % T == 0
    k0 = lax.broadcasted_iota(jnp.int32, (KV, LANES), 1).astype(F32).astype(BF16) < HEAD_DIM
    t0 =lax.broadcasted_iota(jnp.int32, (T, LANES), 1) < HEAD_DIM
    one = jnp.ones((KV, LANES), BF16)
    zero = jnp.zeros((KV, LANES), BF16)
    ones_bd = jnp.concatenate([jnp.where(k0, one, zero), jnp.where(k0, zero, one)], axis=0)
    q_pos = i * T + lax.broadcasted_iota(jnp.int32, (T, 2 * KV), 0)
    k_off = lax.broadcasted_iota(jnp.int32, (T, 2 * KV), 1) % KV

    def split2(x):
        return jnp.concatenate([jnp.where(k0, x, zero), jnp.where(k0, zero, x)], axis=0)

    def pair_step(p, q2, j, state, masked):
        m_a, m_b, acc = state
        sl = pl.ds(pl.multiple_of(p * LANES, LANES), LANES)
        ks = pl.ds(pl.multiple_of(j * KV, KV), KV)
        s = _nt(q2, split2(k_ref[ks, sl]))
        f2 = jnp.concatenate([f_ref[p, 0:1, ks], f_ref[p, 1:2, ks]], axis=1)
        yield
        s = s - f2
        if masked:
            s = jnp.where(j * KV + k_off <= q_pos, s, -jnp.inf)
        new_a = jnp.maximum(m_a, jnp.max(s[:, :KV], axis=-1, keepdims=True))
        new_b = jnp.maximum(m_b, jnp.max(s[:, KV:], axis=-1, keepdims=True))
        pm = jnp.concatenate([jnp.exp(s[:, :KV] - new_a), jnp.exp(s[:, KV:] - new_b)], axis=1)
        rhs = jnp.concatenate([split2(v_ref[ks, sl]), ones_bd], axis=1)
        pv = _mm(pm.astype(BF16), rhs)
        yield
        alpha = jnp.where(t0, jnp.exp(m_a - new_a), jnp.exp(m_b - new_b))
        acc = acc * jnp.concatenate([alpha, alpha], axis=1) + pv
        return new_a, new_b, acc

    def group(gi, carry):
        pairs = [gi * G + n for n in range(G)]
        qs = [q_ref[:, pl.ds(pl.multiple_of(p * LANES, LANES), LANES)] for p in pairs]

        def kv_step(j, states, masked):
            return tuple(_round_robin(
                [pair_step(p, q2, j, st, masked) for p, q2, st in zip(pairs, qs, states)]))

        init = (jnp.full((T, 1), -jnp.inf, F32), jnp.full((T, 1), -jnp.inf, F32),
                jnp.zeros((T, 2 * LANES), F32))
        states = lax.fori_loop(0, n_full, lambda j, st: kv_step(j, st, False), (init,) * G)
        states = kv_step(n_full, states, True)
        for p, (_, _, acc) in zip(pairs, states):
            sl = pl.ds(pl.multiple_of(p * LANES, LANES), LANES)
            o = acc[:, :LANES] / acc[:, LANES:]
            o_ref[:, sl] = (o * _silu(gate_ref[:, sl].astype(F32))).astype(o_ref.dtype)
        return carry

    lax.fori_loop(0, N_PAIRS // G, group, 0)


def _fox_attention(qn, kn, f_cum, proj, bsz, seq):
    T = min(FOX_BLOCK, seq)
    nq = seq // T
    m = bsz * seq
    return pl.pallas_call(
        _fox_body,
        grid=(bsz, nq),
        in_specs=[
            pl.BlockSpec((T, D_MODEL), lambda b, i: (b * nq + i, 0)),
            pl.BlockSpec((seq, D_MODEL), lambda b, i: (b, 0)),
            pl.BlockSpec((seq, D_MODEL), lambda b, i: (b, 2)),
            pl.BlockSpec((N_PAIRS, 8, seq), lambda b, i: (0, 0, b)),
            pl.BlockSpec((T, D_MODEL), lambda b, i: (b * nq + i, 3)),
        ],
        out_specs=pl.BlockSpec((T, D_MODEL), lambda b, i: (b * nq + i, 0)),
        out_shape=jax.ShapeDtypeStruct((m, D_MODEL), BF16),
        compiler_params=_params("parallel", "arbitrary"),
        name="fox_attention",
    )(qn, kn, proj, f_cum, proj)


def _fox_layer(x2, bsz, seq, norm, w_in, b_f, q_gain, k_gain, w_out):
    W = D_MODEL
    proj = _norm_matmul(x2, norm, w_in[:, :4 * W].astype(BF16), name="fox_in_proj")
    k = w_in.shape[0]
    w_f = jnp.zeros((k, N_PAIRS, 8), F32).at[:, :, :2].set(w_in[:, 4 * W:].reshape(k, N_PAIRS, 2))
    b_f8 = jnp.zeros((N_PAIRS, 8), F32).at[:, :2].set(b_f.reshape(N_PAIRS, 2))
    _, f_rows = _gate_proj(x2, norm, w_f.reshape(k, N_PAIRS * 8))
    f_cum = _fox_forget(f_rows, b_f8.reshape(-1), bsz, seq).reshape(N_PAIRS, 8, bsz * seq)
    scale = HEAD_DIM ** -0.5
    qn = _head_norm(proj, 0, (jnp.tile(q_gain, N_HEADS) * scale).reshape(1, W).astype(F32))
    kn = _head_norm(proj, 1, jnp.tile(k_gain, N_HEADS).reshape(1, W).astype(F32))
    a = _fox_attention(qn, kn, f_cum, proj, bsz, seq)
    return _out_proj(a, w_out.astype(BF16), x2)


def kernel(x, rel_bias, swa_norm, swa_w_in, swa_q_gain, swa_k_gain, swa_sinks, swa_w_out, rwkv_norm, rwkv_mix, rwkv_w_in, rwkv_w0, rwkv_w_lora1, rwkv_w_lora2, rwkv_a0, rwkv_a_lora1, rwkv_a_lora2, rwkv_k_k, rwkv_k_a, rwkv_r_k, rwkv_gn_w, rwkv_gn_b, rwkv_w_out, mlstm_norm, mlstm_w_in, mlstm_b_i, mlstm_b_f, mlstm_h_gain, mlstm_w_out, fox_norm, fox_w_in, fox_b_f, fox_q_gain, fox_k_gain, fox_w_out):
    bsz, seq, d = x.shape
    depth = swa_norm.shape[0] + rwkv_norm.shape[0] + mlstm_norm.shape[0] + fox_norm.shape[0]
    x2 = x.reshape(bsz * seq, d)
    for layer in range(depth):
        kind, idx = layer % 4, layer // 4
        if kind == 0:
            x2 = _swa_layer(x2, bsz, seq, rel_bias, swa_norm[idx], swa_w_in[idx], swa_q_gain[idx],
                            swa_k_gain[idx], swa_sinks[idx], swa_w_out[idx])
        elif kind == 1:
            x2 = _rwkv_layer(x2, bsz, seq, rwkv_norm[idx], rwkv_mix[idx], rwkv_w_in[idx],
                             rwkv_w0[idx], rwkv_w_lora1[idx], rwkv_w_lora2[idx], rwkv_a0[idx],
                             rwkv_a_lora1[idx], rwkv_a_lora2[idx], rwkv_k_k[idx], rwkv_k_a[idx],
                             rwkv_r_k[idx], rwkv_gn_w[idx], rwkv_gn_b[idx], rwkv_w_out[idx])
        elif kind == 2:
            x2 = _mlstm_layer(x2, bsz, seq, mlstm_norm[idx], mlstm_w_in[idx], mlstm_b_i[idx],
                              mlstm_b_f[idx], mlstm_h_gain[idx], mlstm_w_out[idx])
        else:
            x2 = _fox_layer(x2, bsz, seq, fox_norm[idx], fox_w_in[idx], fox_b_f[idx],
                            fox_q_gain[idx], fox_k_gain[idx], fox_w_out[idx])
    return x2.reshape(bsz, seq, d)
```

```python
import functools
import math

import jax
import jax.numpy as jnp
from jax import lax
from jax.experimental import pallas as pl
from jax.experimental.pallas import tpu as pltpu

F32 = jnp.float32
BF16 = jnp.bfloat16
HIGHEST = lax.Precision.HIGHEST

D_MODEL = 2048
RMS_EPS = 1e-6
GN_EPS = 64e-5
LOG2_E = math.log2(math.e)

HEAD_DIM = 64
N_HEADS = D_MODEL // HEAD_DIM
LANES = 128
N_PAIRS = D_MODEL // LANES

SWA_KV_HEADS = 4
SWA_GROUP = N_HEADS // SWA_KV_HEADS
SWA_KV_WIDTH = SWA_KV_HEADS * HEAD_DIM
SWA_BLOCK = 128
SWA_PAIR_UNROLL = 8
REL_BUCKETS = 32
REL_MAX_DIST = 128

RWKV_CHUNK = 64
RWKV_TM = 512
RWKV_LORA_PAD = 128
RWKV_PAIR_UNROLL = 16

MLSTM_HEADS = 8
MLSTM_V_DIM = 256
MLSTM_QK_DIM = 128
MLSTM_CHUNK = 128
MLSTM_HEAD_UNROLL = 4
MLSTM_MAIN = 2 * MLSTM_HEADS * MLSTM_QK_DIM + 3 * MLSTM_HEADS * MLSTM_V_DIM

FOX_BLOCK = 128
FOX_KV = 512
FOX_PAIR_UNROLL = 4

VMEM_LIMIT = 56 * 1024 * 1024
PROJ_TM = 1024
PROJ_TN_CHOICES = (1536, 1024, 512, 256, 128)


def _params(*sem):
    return pltpu.CompilerParams(dimension_semantics=sem, vmem_limit_bytes=VMEM_LIMIT)


def _nt(a, b, precision=None):
    return lax.dot_general(a, b, (((1,), (1,)), ((), ())), precision=precision,
                           preferred_element_type=F32)


def _tn(a, b, precision=None):
    return lax.dot_general(a, b, (((0,), (0,)), ((), ())), precision=precision,
                           preferred_element_type=F32)


def _mm(a, b, precision=None):
    return jnp.dot(a, b, precision=precision, preferred_element_type=F32)


def _rms_rows(x, g):
    return x * lax.rsqrt(jnp.mean(x * x, axis=-1, keepdims=True) + RMS_EPS) * g


def _silu(x):
    return x * jax.nn.sigmoid(x)


def _round_robin(gens):
    results = [None] * len(gens)
    live = list(range(len(gens)))
    while live:
        for n in list(live):
            try:
                next(gens[n])
            except StopIteration as done:
                results[n] = done.value
                live.remove(n)
    return results


def _norm_matmul_body(x_ref, g_ref, w_ref, o_ref, xn_ref):
    @pl.when(pl.program_id(1) == 0)
    def _():
        xn_ref[...] = _rms_rows(x_ref[...], g_ref[...]).astype(BF16)

    o_ref[...] = _mm(xn_ref[...], w_ref[...]).astype(o_ref.dtype)


def _proj_tiles(m, n):
    tm = min(PROJ_TM, m)
    tn = next(t for t in PROJ_TN_CHOICES if n % t == 0)
    return tm, tn


def _norm_matmul(x, g, w, *, name="norm_matmul"):
    m, k = x.shape
    n = w.shape[1]
    tm, tn = _proj_tiles(m, n)
    out_dtype = BF16
    return pl.pallas_call(
        _norm_matmul_body,
        grid=(m // tm, n // tn),
        in_specs=[
            pl.BlockSpec((tm, k), lambda i, j: (i, 0)),
            pl.BlockSpec((1, k), lambda i, j: (0, 0)),
            pl.BlockSpec((k, tn), lambda i, j: (0, j)),
        ],
        out_specs=pl.BlockSpec((tm, tn), lambda i, j: (i, j)),
        out_shape=jax.ShapeDtypeStruct((m, n), out_dtype),
        scratch_shapes=[pltpu.VMEM((tm, k), BF16)],
        compiler_params=_params("parallel", "arbitrary"),
        name=name,
    )(x, g.reshape(1, k), w)


def _gate_proj_body(x_ref, g_ref, w_ref, wt_ref, cols_ref, rows_ref):
    xn = _rms_rows(x_ref[...], g_ref[...]).astype(BF16)
    cols_ref[...] = _mm(xn, w_ref[...])
    rows_ref[...] = _nt(wt_ref[...], xn)


def _gate_proj(x, g, w_gates):
    m, k = x.shape
    n_g = w_gates.shape[1]
    tm = min(512, m)
    w_pad = jnp.zeros((k, LANES), BF16).at[:, :n_g].set(w_gates.astype(BF16))
    w_t = w_gates.T.astype(BF16)
    return pl.pallas_call(
        _gate_proj_body,
        grid=(m // tm,),
        in_specs=[
            pl.BlockSpec((tm, k), lambda i: (i, 0)),
            pl.BlockSpec((1, k), lambda i: (0, 0)),
            pl.BlockSpec((k, LANES), lambda i: (0, 0)),
            pl.BlockSpec((n_g, k), lambda i: (0, 0)),
        ],
        out_specs=[
            pl.BlockSpec((tm, LANES), lambda i: (i, 0)),
            pl.BlockSpec((n_g, tm), lambda i: (0, i)),
        ],
        out_shape=[
            jax.ShapeDtypeStruct((m, LANES), F32),
            jax.ShapeDtypeStruct((n_g, m), F32),
        ],
        compiler_params=_params("parallel"),
        name="gate_proj",
    )(x, g.reshape(1, k), w_pad, w_t)


def _out_proj_body(a_ref, w_ref, r_ref, o_ref):
    o_ref[...] = r_ref[...] + _mm(a_ref[...], w_ref[...])


def _out_proj(a, w, resid):
    m, k = a.shape
    n = w.shape[1]
    tm = min(512, m)
    return pl.pallas_call(
        _out_proj_body,
        grid=(m // tm,),
        in_specs=[
            pl.BlockSpec((tm, k), lambda i: (i, 0)),
            pl.BlockSpec((k, n), lambda i: (0, 0)),
            pl.BlockSpec((tm, n), lambda i: (i, 0)),
        ],
        out_specs=pl.BlockSpec((tm, n), lambda i: (i, 0)),
        out_shape=jax.ShapeDtypeStruct((m, n), F32),
        compiler_params=_params("parallel"),
        name="out_proj",
    )(a, w, resid)


def _t5_bucket(dist):
    max_exact = REL_BUCKETS // 2
    d_f = jnp.maximum(dist, 1).astype(F32)
    large = max_exact + (jnp.log(d_f / max_exact) / math.log(REL_MAX_DIST / max_exact)
                         * (REL_BUCKETS - max_exact)).astype(jnp.int32)
    large = jnp.minimum(large, REL_BUCKETS - 1)
    return jnp.where(dist < max_exact, dist, large)


def _swa_body(q_ref, gate_ref, kc_ref, kp_ref, vc_ref, vp_ref, bias_ref, sink_ref, qg_ref, kg_ref,
              o_ref):
    L = SWA_BLOCK
    k0 = lax.broadcasted_iota(jnp.int32, (2 * L, LANES), 1) < HEAD_DIM
    t0 = lax.broadcasted_iota(jnp.int32, (L, LANES), 1) < HEAD_DIM
    bi = lax.broadcasted_iota(jnp.int32, (LANES, LANES), 0) // HEAD_DIM
    bj = lax.broadcasted_iota(jnp.int32, (LANES, LANES), 1) // HEAD_DIM
    bd_ones = (bi == bj).astype(BF16)
    zero = jnp.zeros((2 * L, LANES), F32)
    one = jnp.ones((2 * L, LANES), F32)
    ones_bd = jnp.concatenate([jnp.where(k0, one, zero), jnp.where(k0, zero, one)], axis=0).astype(BF16)

    def head_mean_sq(z):
        return _mm((z * z).astype(BF16), bd_ones) * (1.0 / HEAD_DIM)

    def both_halves(x, head_in_low_lanes):
        if head_in_low_lanes:
            lo = jnp.where(k0, x, zero)
            hi = pltpu.roll(lo, HEAD_DIM, axis=1)
        else:
            hi = jnp.where(k0, zero, x)
            lo = pltpu.roll(hi, HEAD_DIM, axis=1)
        return jnp.concatenate([lo, hi], axis=0)

    k_all = jnp.concatenate([kp_ref[...], kc_ref[...]], axis=0).astype(F32)
    v_all = jnp.concatenate([vp_ref[...], vc_ref[...]], axis=0).astype(F32)
    kbd, rhs = [], []
    for t in range(SWA_KV_WIDTH // LANES):
        kt = k_all[:, t * LANES:(t + 1) * LANES]
        kn = kt * lax.rsqrt(head_mean_sq(kt) + RMS_EPS) * kg_ref[...]
        vt = v_all[:, t * LANES:(t + 1) * LANES]
        for low in (True, False):
            kbd.append(both_halves(kn, low).astype(BF16))
            rhs.append(jnp.concatenate([both_halves(vt, low).astype(BF16), ones_bd], axis=1))

    def pair(p):
        kh = p // (SWA_GROUP // 2)
        sl = slice(p * LANES, (p + 1) * LANES)
        q2 = q_ref[:, sl].astype(F32)
        ms = head_mean_sq(q2)
        yield
        qn = (q2 * lax.rsqrt(ms + RMS_EPS) * qg_ref[...]).astype(BF16)
        s = _nt(qn, kbd[kh]) + bias_ref[0, p]
        yield
        sink_a = sink_ref[2 * p]
        sink_b = sink_ref[2 * p + 1]
        m_a = jnp.maximum(jnp.max(s[:, :2 * L], axis=-1, keepdims=True), sink_a)
        m_b = jnp.maximum(jnp.max(s[:, 2 * L:], axis=-1, keepdims=True), sink_b)
        pm = jnp.concatenate([jnp.exp(s[:, :2 * L] - m_a), jnp.exp(s[:, 2 * L:] - m_b)], axis=1)
        pv = _mm(pm.astype(BF16), rhs[kh])
        yield
        sink_t = jnp.where(t0, jnp.exp(sink_a - m_a), jnp.exp(sink_b - m_b))
        o = pv[:, :LANES] / (pv[:, LANES:] + sink_t)
        return (o * _silu(gate_ref[:, sl].astype(F32))).astype(o_ref.dtype)

    for g0 in range(0, N_PAIRS, SWA_PAIR_UNROLL):
        outs = _round_robin([pair(p) for p in range(g0, g0 + SWA_PAIR_UNROLL)])
        for n, o in enumerate(outs):
            o_ref[:, (g0 + n) * LANES:(g0 + n + 1) * LANES] = o


def _swa_attention(proj, bias, sinks, q_gain2, k_gain2, bsz, seq):
    L = SWA_BLOCK
    nb = seq // L
    m = bsz * seq
    kcol = 2 * D_MODEL // SWA_KV_WIDTH
    prev = lambda r: jnp.where(r % nb == 0, r, r - 1)
    return pl.pallas_call(
        _swa_body,
        grid=(bsz * nb,),
        in_specs=[
            pl.BlockSpec((L, D_MODEL), lambda r: (r, 0)),
            pl.BlockSpec((L, D_MODEL), lambda r: (r, 1)),
            pl.BlockSpec((L, SWA_KV_WIDTH), lambda r: (r, kcol)),
            pl.BlockSpec((L, SWA_KV_WIDTH), lambda r: (prev(r), kcol)),
            pl.BlockSpec((L, SWA_KV_WIDTH), lambda r: (r, kcol + 1)),
            pl.BlockSpec((L, SWA_KV_WIDTH), lambda r: (prev(r), kcol + 1)),
            pl.BlockSpec((1, N_PAIRS, L, 4 * L), lambda r: (jnp.where(r % nb == 0, 0, 1), 0, 0, 0)),
            pl.BlockSpec(memory_space=pltpu.SMEM),
            pl.BlockSpec((1, LANES), lambda r: (0, 0)),
            pl.BlockSpec((1, LANES), lambda r: (0, 0)),
        ],
        out_specs=pl.BlockSpec((L, D_MODEL), lambda r: (r, 0)),
        out_shape=jax.ShapeDtypeStruct((m, D_MODEL), BF16),
        compiler_params=_params("parallel"),
        name="swa_attention",
    )(proj, proj, proj, proj, proj, proj, bias, sinks, q_gain2, k_gain2)


def _swa_bias_tables(rel_bias):
    L = SWA_BLOCK
    qi = jnp.arange(L)[:, None]
    kj = jnp.arange(2 * L)[None, :]
    dist = qi + L - kj
    in_band = (dist >= 0) & (dist < L)
    bias = rel_bias.astype(F32)[_t5_bucket(jnp.maximum(dist, 0))].transpose(2, 0, 1)
    general = jnp.where(in_band[None], bias, -jnp.inf)
    first = jnp.where((in_band & (kj >= L))[None], bias, -jnp.inf)
    pairs = lambda t: t.reshape(N_PAIRS, 2, L, 2 * L).transpose(0, 2, 1, 3).reshape(N_PAIRS, L, 4 * L)
    return jnp.stack([pairs(first), pairs(general)])


def _swa_layer(x2, bsz, seq, rel_bias, norm, w_in, q_gain, k_gain, sinks, w_out):
    W = D_MODEL
    kvw = SWA_KV_WIDTH
    w_perm = jnp.concatenate([w_in[:, :W], w_in[:, W + 2 * kvw:], w_in[:, W:W + 2 * kvw]], axis=1)
    proj = _norm_matmul(x2, norm, w_perm.astype(BF16), name="swa_in_proj")
    q_gain2 = (jnp.tile(q_gain, 2) * HEAD_DIM ** -0.5).reshape(1, LANES).astype(F32)
    k_gain2 = jnp.tile(k_gain, 2).reshape(1, LANES).astype(F32)
    a = _swa_attention(proj, _swa_bias_tables(rel_bias), sinks.astype(F32), q_gain2, k_gain2, bsz, seq)
    return _out_proj(a, w_out.astype(BF16), x2)


def _shifted_mix(x_ref, xp_ref, g_ref, seq_start):
    h = _rms_rows(x_ref[...], g_ref[...])
    h_prev_row = _rms_rows(xp_ref[7:8, :], g_ref[...])
    h_prev_row = jnp.where(seq_start, 0.0, h_prev_row)
    rolled = pltpu.roll(h, 1, axis=0)
    row = lax.broadcasted_iota(jnp.int32, h.shape, 0)
    shifted = jnp.where(row == 0, h_prev_row, rolled)
    return h, shifted - h


def _rwkv_in_body(tiles_per_seq, x_ref, xp_ref, g_ref, mix_ref, w_ref, o_ref, xm_ref):
    i = pl.program_id(0)

    @pl.when((pl.program_id(1) == 0) & (pl.program_id(2) == 0))
    def _():
        h, xx = _shifted_mix(x_ref, xp_ref, g_ref, i % tiles_per_seq == 0)
        for c in range(4):
            xm_ref[c] = (h + xx * mix_ref[c:c + 1, :]).astype(BF16)

    c = pl.program_id(1)
    o_ref[0] = _mm(xm_ref[c], w_ref[0]).astype(o_ref.dtype)


def _rwkv_in_proj(x2, seq, norm, mix8, w4):
    m, k = x2.shape
    tm = min(RWKV_TM, seq)
    tn = k
    rows8 = tm // 8
    return pl.pallas_call(
        functools.partial(_rwkv_in_body, seq // tm),
        grid=(m // tm, 4, k // tn),
        in_specs=[
            pl.BlockSpec((tm, k), lambda i, c, j: (i, 0)),
            pl.BlockSpec((8, k), lambda i, c, j: (jnp.maximum(i * rows8 - 1, 0), 0)),
            pl.BlockSpec((1, k), lambda i, c, j: (0, 0)),
            pl.BlockSpec((8, k), lambda i, c, j: (0, 0)),
            pl.BlockSpec((1, k, tn), lambda i, c, j: (c, 0, j)),
        ],
        out_specs=pl.BlockSpec((1, tm, tn), lambda i, c, j: (c, i, j)),
        out_shape=jax.ShapeDtypeStruct((4, m, k), BF16),
        scratch_shapes=[pltpu.VMEM((4, tm, k), BF16)],
        compiler_params=_params("parallel", "arbitrary", "arbitrary"),
        name="rwkv_in_proj",
    )(x2, x2, norm.reshape(1, k), mix8, w4)


def _rwkv_lora_body(tiles_per_seq, x_ref, xp_ref, g_ref, mix_ref, w1_ref, w2_ref, a1_ref, a2_ref,
                    w0_ref, a0_ref, lw_ref, a_ref):
    i = pl.program_id(0)
    h, xx = _shifted_mix(x_ref, xp_ref, g_ref, i % tiles_per_seq == 0)
    xw = (h + xx * mix_ref[4:5, :]).astype(BF16)
    xa = (h + xx * mix_ref[5:6, :]).astype(BF16)
    t = jnp.tanh(_mm(xw, w1_ref[...])).astype(BF16)
    w_pre = w0_ref[...] + _mm(t, w2_ref[...])
    lw_ref[...] = -jnp.exp(-jax.nn.softplus(-w_pre) - 0.5)
    u = _mm(xa, a1_ref[...]).astype(BF16)
    a_ref[...] = jax.nn.sigmoid(a0_ref[...] + _mm(u, a2_ref[...]))


def _rwkv_lora(x2, seq, norm, mix8, w1, w2, a1, a2, w0, a0):
    m, k = x2.shape
    tm = min(RWKV_TM, seq)
    rows8 = tm // 8
    P = RWKV_LORA_PAD
    full = lambda shape: pl.BlockSpec(shape, lambda i: tuple(0 for _ in shape))
    return pl.pallas_call(
        functools.partial(_rwkv_lora_body, seq // tm),
        grid=(m // tm,),
        in_specs=[
            pl.BlockSpec((tm, k), lambda i: (i, 0)),
            pl.BlockSpec((8, k), lambda i: (jnp.maximum(i * rows8 - 1, 0), 0)),
            full((1, k)), full((8, k)),
            full((k, P)), full((P, k)), full((k, P)), full((P, k)),
            full((1, k)), full((1, k)),
        ],
        out_specs=[pl.BlockSpec((tm, k), lambda i: (i, 0)), pl.BlockSpec((tm, k), lambda i: (i, 0))],
        out_shape=[jax.ShapeDtypeStruct((m, k), F32), jax.ShapeDtypeStruct((m, k), F32)],
        compiler_params=_params("parallel"),
        name="rwkv_lora",
    )(x2, x2, norm.reshape(1, k), mix8, w1, w2, a1, a2, w0.reshape(1, k), a0.reshape(1, k))


def _stack2(z, m0):
    zero = jnp.zeros_like(z)
    return jnp.concatenate([jnp.where(m0, z, zero), jnp.where(m0, zero, z)], axis=0)


def _rwkv_scan_body(r_ref, k_ref, v_ref, g_ref, lw_ref, a_ref, kk_ref, ka_ref, rk_ref,
                    gnw_ref, gnb_ref, o_ref, ht_ref):
    C = RWKV_CHUNK

    @pl.when(pl.program_id(1) == 0)
    def _():
        ht_ref[...] = jnp.zeros_like(ht_ref)

    lane = lax.broadcasted_iota(jnp.int32, (C, LANES), 1)
    tok = lax.broadcasted_iota(jnp.int32, (C, LANES), 0)
    m0 = lane < HEAD_DIM
    col = lane % HEAD_DIM
    strict = tok > col
    incl = tok >= col
    ti = lax.broadcasted_iota(jnp.int32, (C, C), 0)
    tj = lax.broadcasted_iota(jnp.int32, (C, C), 1)
    tri = (ti >= tj).astype(F32)
    bi = lax.broadcasted_iota(jnp.int32, (LANES, LANES), 0) // HEAD_DIM
    bj = lax.broadcasted_iota(jnp.int32, (LANES, LANES), 1) // HEAD_DIM
    bd = bi == bj
    bd_ones = bd.astype(BF16)
    tri = tri.astype(BF16)

    def head_sum(z):
        return _mm(z.astype(BF16), bd_ones)

    def load(p):
        sl = pl.ds(pl.multiple_of(p * LANES, LANES), LANES)
        f32 = lambda ref: ref[0, :, sl].astype(F32)
        return (f32(r_ref), f32(k_ref), f32(v_ref), f32(g_ref), lw_ref[:, sl],
                a_ref[:, sl], kk_ref[:, sl], ka_ref[:, sl], rk_ref[:, sl], gnw_ref[:, sl],
                gnb_ref[:, sl], ht_ref[p])

    def compute(r, k, v, g, lw, a, k_k, k_a, r_k, gn_w, gn_b, ht):
        kkp = k * k_k
        ss = head_sum(kkp * kkp)
        lw_hi = lw.astype(BF16)
        lw_lo = (lw - lw_hi.astype(F32)).astype(BF16)
        cum = _mm(tri, lw_hi) + _mm(tri, lw_lo)
        yield
        kk = kkp / jnp.maximum(jnp.sqrt(ss), 1e-12)
        kp = k * (1.0 + (a - 1.0) * k_a)
        cum_end = cum[C - 1:C, :]
        g_in = jnp.exp(cum)
        g_prev = jnp.exp(cum - lw)
        g_inv = jnp.exp(-cum)
        g_end = jnp.exp(cum_end - cum)
        kka = kk * a
        a_t = -kk * g_prev
        r_t = r * g_in
        b_t = kka * g_inv
        k_t = kp * g_inv

        x1 = jnp.concatenate([a_t, r_t], axis=0).astype(BF16)
        bk2 = jnp.concatenate([_stack2(b_t, m0), _stack2(k_t, m0)], axis=0).astype(BF16)
        gbk = _nt(x1, bk2)
        xh = _nt(x1, ht.astype(BF16))
        yield
        zero = jnp.zeros((C, LANES), F32)
        l_ab = jnp.where(strict, gbk[:C, :LANES], zero)
        l_ak = jnp.where(strict, gbk[:C, LANES:], zero)
        p_rb = jnp.where(incl, gbk[C:, :LANES], zero)
        p_rk = jnp.where(incl, gbk[C:, LANES:], zero)

        t_inv = jnp.where(tok == col, 1.0, zero) + jnp.where(
            (tok // 2 == col // 2) & (tok % 2 == 1) & (col % 2 == 0), l_ab, zero)
        v2 = _stack2(v, m0).astype(BF16)
        rhs = xh[:C] + _mm(l_ak.astype(BF16), v2)
        bonus_dot = head_sum(r * kp * r_k)
        s = 2
        while s < C:
            sel = (tok // (2 * s) == col // (2 * s)) & (tok % (2 * s) >= s) & (col % (2 * s) < s)
            l_s = jnp.where(sel, l_ab, zero)
            tl = _mm(t_inv.astype(BF16), _stack2(l_s, m0).astype(BF16))
            yield
            t_inv = t_inv + _mm(tl.astype(BF16), _stack2(t_inv, m0).astype(BF16))
            yield
            s *= 2

        u = _mm(t_inv.astype(BF16), _stack2(rhs, m0).astype(BF16))
        yield
        u2 = _stack2(u, m0).astype(BF16)
        p_cat = jnp.concatenate([p_rb, p_rk], axis=1).astype(BF16)
        y = xh[C:] + _mm(p_cat, jnp.concatenate([u2, v2], axis=0))
        uv = jnp.concatenate([u, v], axis=0).astype(BF16)
        bk_end = jnp.concatenate([kka * g_end, kp * g_end], axis=0).astype(BF16)
        upd = _tn(uv, bk_end)
        yield
        ht_new = ht * jnp.exp(cum_end) + jnp.where(bd, upd, 0.0)
        mu = head_sum(y) * (1.0 / HEAD_DIM)
        yield
        dy = y - mu
        var = head_sum(dy * dy) * (1.0 / HEAD_DIM)
        yield
        yn = dy * lax.rsqrt(var + GN_EPS) * gn_w + gn_b
        return ((yn + bonus_dot * v) * _silu(g)).astype(o_ref.dtype), ht_new

    def group(gi, carry):
        base = gi * RWKV_PAIR_UNROLL
        ins = [load(base + n) for n in range(RWKV_PAIR_UNROLL)]
        outs = _round_robin([compute(*x) for x in ins])
        for n, (o, ht_new) in enumerate(outs):
            p = base + n
            o_ref[:, pl.ds(pl.multiple_of(p * LANES, LANES), LANES)] = o
            ht_ref[p] = ht_new
        return carry

    lax.fori_loop(0, N_PAIRS // RWKV_PAIR_UNROLL, group, 0)


def _rwkv_scan(rkvg, lw, a, k_k, k_a, r_k, gn_w, gn_b, bsz, seq):
    C = RWKV_CHUNK
    nc = seq // C
    m = bsz * seq
    k = D_MODEL
    part = lambda c: pl.BlockSpec((1, C, k), lambda b, j, c=c: (c, b * nc + j, 0))
    tile = pl.BlockSpec((C, k), lambda b, j: (b * nc + j, 0))
    vec = pl.BlockSpec((1, k), lambda b, j: (0, 0))
    return pl.pallas_call(
        _rwkv_scan_body,
        grid=(bsz, nc),
        in_specs=[part(0), part(1), part(2), part(3), tile, tile, vec, vec, vec, vec, vec],
        out_specs=tile,
        out_shape=jax.ShapeDtypeStruct((m, k), BF16),
        scratch_shapes=[pltpu.VMEM((N_PAIRS, LANES, LANES), F32)],
        compiler_params=_params("parallel", "arbitrary"),
        name="rwkv_scan",
    )(rkvg, rkvg, rkvg, rkvg, lw, a, k_k.reshape(1, k), k_a.reshape(1, k), r_k.reshape(1, k),
      gn_w.reshape(1, k), gn_b.reshape(1, k))


def _rwkv_layer(x2, bsz, seq, norm, mix, w_in, w0, w_lora1, w_lora2, a0, a_lora1, a_lora2,
                k_k, k_a, r_k, gn_w, gn_b, w_out):
    k = D_MODEL
    P = RWKV_LORA_PAD
    mix8 = jnp.zeros((8, k), F32).at[:mix.shape[0]].set(mix)
    pad_cols = lambda w: jnp.zeros((k, P), BF16).at[:, :w.shape[1]].set(w.astype(BF16))
    pad_rows = lambda w: jnp.zeros((P, k), BF16).at[:w.shape[0]].set(w.astype(BF16))
    rkvg = _rwkv_in_proj(x2, seq, norm, mix8, w_in.astype(BF16))
    lw, a = _rwkv_lora(x2, seq, norm, mix8, pad_cols(w_lora1), pad_rows(w_lora2),
                       pad_cols(a_lora1), pad_rows(a_lora2), w0, a0)
    y = _rwkv_scan(rkvg, lw, a, k_k, k_a, r_k, gn_w, gn_b, bsz, seq)
    return _out_proj(y, w_out.astype(BF16), x2)


def _mlstm_body(q_ref, k_ref, v_ref, op_ref, gate_ref, gc_ref, gr_ref, bi_ref, bf_ref, hg_ref,
                o_ref, ct_ref, m_ref):
    L = MLSTM_CHUNK
    H = MLSTM_HEADS
    dk = MLSTM_QK_DIM
    dv = MLSTM_V_DIM

    @pl.when(pl.program_id(1) == 0)
    def _():
        ct_ref[...] = jnp.zeros_like(ct_ref)
        m_ref[...] = jnp.zeros_like(m_ref)

    ti = lax.broadcasted_iota(jnp.int32, (L, L), 0)
    tj = lax.broadcasted_iota(jnp.int32, (L, L), 1)
    tril = ti >= tj
    tri_f = tril.astype(F32)
    gc = gc_ref[...] + bi_ref[...]
    gr = gr_ref[...] + bf_ref[...]
    lane = lax.broadcasted_iota(jnp.int32, gc.shape, 1)
    f_cols = jnp.where((lane >= H) & (lane < 2 * H), jax.nn.log_sigmoid(gc), 0.0)
    b_cols = _mm(tri_f, f_cols, HIGHEST)
    f_rows = jax.nn.log_sigmoid(gr)
    b_rows = _nt(f_rows, tri_f, HIGHEST)
    ones_blk = jnp.ones((L, LANES), BF16)

    def head(h):
        q = q_ref[:, h * dk:(h + 1) * dk]
        k = k_ref[:, h * dk:(h + 1) * dk]
        v = v_ref[:, h * dv:(h + 1) * dv]
        v_aug = jnp.concatenate([v, ones_blk], axis=-1)
        ct = ct_ref[h]
        m_prev = m_ref[h][:, :1]
        qk = _nt(q, k)
        q_ct = _mm(q, ct.astype(BF16))
        yield
        i_col = gc[:, h:h + 1]
        i_row = gr[h:h + 1, :]
        b_col = b_cols[:, H + h:H + h + 1]
        b_row = b_rows[H + h:H + h + 1, :]
        g_tot = b_col[L - 1:L, :]
        dmat = jnp.where(tril, b_col - b_row + i_row, -jnp.inf)
        inter = b_col + m_prev
        m_t = jnp.maximum(inter, jnp.max(dmat, axis=-1, keepdims=True))
        w_intra = jnp.exp(dmat - m_t) * dk ** -0.5
        w_inter = jnp.exp(inter - m_t) * dk ** -0.5
        sw = qk * w_intra
        num = w_inter * q_ct + _mm(sw.astype(BF16), v_aug)
        decay = g_tot - b_col + i_col
        m_new = jnp.maximum(g_tot + m_prev, jnp.max(decay, axis=0, keepdims=True))
        ws = jnp.exp(decay - m_new)
        carry_scale = jnp.exp(g_tot + m_prev - m_new)
        wv = (ws * v_aug.astype(F32)).astype(BF16)
        upd = _tn(k, wv)
        yield
        den = num[:, dv:dv + 1]
        h_t = num[:, :dv] / jnp.maximum(jnp.abs(den), jnp.exp(-m_t))
        hn = _rms_rows(h_t, hg_ref[:, h * dv:(h + 1) * dv])
        osl = slice(h * dv, (h + 1) * dv)
        out = hn * jax.nn.sigmoid(op_ref[:, osl].astype(F32)) * _silu(gate_ref[:, osl].astype(F32))
        return out.astype(o_ref.dtype), carry_scale * ct + upd, jnp.broadcast_to(m_new, (1, LANES))

    for h0 in range(0, H, MLSTM_HEAD_UNROLL):
        heads = range(h0, h0 + MLSTM_HEAD_UNROLL)
        for h, (out, ct_new, m_new) in zip(heads, _round_robin([head(h) for h in heads])):
            o_ref[:, h * dv:(h + 1) * dv] = out
            ct_ref[h] = ct_new
            m_ref[h] = m_new


def _mlstm_scan(proj, g_cols, g_rows, b_i, b_f, h_gain, bsz, seq):
    L = MLSTM_CHUNK
    H = MLSTM_HEADS
    nc = seq // L
    m = bsz * seq
    qkw = H * MLSTM_QK_DIM
    vw = H * MLSTM_V_DIM
    bias = jnp.concatenate([b_i, b_f]).astype(F32)
    bias_row = jnp.zeros((1, LANES), F32).at[0, :2 * H].set(bias)
    bias_col = bias.reshape(2 * H, 1)
    row = lambda b, j: b * nc + j
    return pl.pallas_call(
        _mlstm_body,
        grid=(bsz, nc),
        in_specs=[
            pl.BlockSpec((L, qkw), lambda b, j: (row(b, j), 0)),
            pl.BlockSpec((L, qkw), lambda b, j: (row(b, j), 1)),
            pl.BlockSpec((L, vw), lambda b, j: (row(b, j), 1)),
            pl.BlockSpec((L, vw), lambda b, j: (row(b, j), 2)),
            pl.BlockSpec((L, vw), lambda b, j: (row(b, j), 3)),
            pl.BlockSpec((L, LANES), lambda b, j: (row(b, j), 0)),
            pl.BlockSpec((2 * H, L), lambda b, j: (0, row(b, j))),
            pl.BlockSpec((1, LANES), lambda b, j: (0, 0)),
            pl.BlockSpec((2 * H, 1), lambda b, j: (0, 0)),
            pl.BlockSpec((1, vw), lambda b, j: (0, 0)),
        ],
        out_specs=pl.BlockSpec((L, vw), lambda b, j: (row(b, j), 0)),
        out_shape=jax.ShapeDtypeStruct((m, vw), BF16),
        scratch_shapes=[
            pltpu.VMEM((H, MLSTM_QK_DIM, MLSTM_V_DIM + LANES), F32),
            pltpu.VMEM((H, 1, LANES), F32),
        ],
        compiler_params=_params("parallel", "arbitrary"),
        name="mlstm_scan",
    )(proj, proj, proj, proj, proj, g_cols, g_rows, bias_row, bias_col, h_gain.reshape(1, vw))


def _mlstm_layer(x2, bsz, seq, norm, w_in, b_i, b_f, h_gain, w_out):
    proj = _norm_matmul(x2, norm, w_in[:, :MLSTM_MAIN].astype(BF16), name="mlstm_in_proj")
    g_cols, g_rows = _gate_proj(x2, norm, w_in[:, MLSTM_MAIN:])
    a = _mlstm_scan(proj, g_cols, g_rows, b_i, b_f, h_gain, bsz, seq)
    return _out_proj(a, w_out.astype(BF16), x2)


def _head_norm_body(x_ref, g_ref, o_ref):
    x = x_ref[...].astype(F32)
    bi = lax.broadcasted_iota(jnp.int32, (LANES, LANES), 0) // HEAD_DIM
    bj = lax.broadcasted_iota(jnp.int32, (LANES, LANES), 1) // HEAD_DIM
    bd_mean = jnp.where(bi == bj, 1.0 / HEAD_DIM, 0.0).astype(BF16)
    for c in range(x.shape[1] // LANES):
        xc = x[:, c * LANES:(c + 1) * LANES]
        ms = _mm((xc * xc).astype(BF16), bd_mean)
        o_ref[:, c * LANES:(c + 1) * LANES] = (
            xc * lax.rsqrt(ms + RMS_EPS) * g_ref[:, c * LANES:(c + 1) * LANES]).astype(o_ref.dtype)


def _head_norm(proj, col_block, gain_row):
    m = proj.shape[0]
    tm = min(512, m)
    return pl.pallas_call(
        _head_norm_body,
        grid=(m // tm,),
        in_specs=[
            pl.BlockSpec((tm, D_MODEL), lambda i: (i, col_block)),
            pl.BlockSpec((1, D_MODEL), lambda i: (0, 0)),
        ],
        out_specs=pl.BlockSpec((tm, D_MODEL), lambda i: (i, 0)),
        out_shape=jax.ShapeDtypeStruct((m, D_MODEL), BF16),
        compiler_params=_params("parallel"),
        name="head_norm",
    )(proj, gain_row)


def _fox_forget_body(f_ref, bf_ref, o_ref):
    B = LANES
    seq = f_ref.shape[1]
    lf = jax.nn.log_sigmoid(f_ref[...] + bf_ref[...])
    ti = lax.broadcasted_iota(jnp.int32, (B, B), 0)
    tj = lax.broadcasted_iota(jnp.int32, (B, B), 1)
    triu = (ti <= tj).astype(F32)
    carry = jnp.zeros((lf.shape[0], 1), F32)
    for c in range(seq // B):
        blk = _mm(lf[:, c * B:(c + 1) * B], triu, HIGHEST) + carry
        o_ref[:, c * B:(c + 1) * B] = blk * LOG2_E
        carry = blk[:, B - 1:B]


def _fox_forget(f_rows, b_f, bsz, seq):
    H = f_rows.shape[0]
    return pl.pallas_call(
        _fox_forget_body,
        grid=(bsz,),
        in_specs=[pl.BlockSpec((H, seq), lambda b: (0, b)), pl.BlockSpec((H, 1), lambda b: (0, 0))],
        out_specs=pl.BlockSpec((H, seq), lambda b: (0, b)),
        out_shape=jax.ShapeDtypeStruct((H, bsz * seq), F32),
        compiler_params=_params("parallel"),
        name="fox_forget",
    )(f_rows, b_f.reshape(H, 1).astype(F32))


def _fox_body(q_ref, k_ref, v_ref, f_ref, gate_ref, o_ref):
    T = q_ref.shape[0]
    KV = min(FOX_KV, k_ref.shape[0])
    G = FOX_PAIR_UNROLL
    i = pl.program_id(1)
    n_full = (i * T) // KV
    assert KV % T == 0
    SUB = LANES
    k0 = lax.broadcasted_iota(jnp.int32, (SUB, LANES), 1).astype(F32).astype(BF16) < HEAD_DIM
    t0 = lax.broadcasted_iota(jnp.int32, (T, LANES), 1) < HEAD_DIM
    one = jnp.ones((SUB, LANES), BF16)
    zero = jnp.zeros((SUB, LANES), BF16)
    ones_sub = jnp.concatenate([jnp.where(k0, one, zero), jnp.where(k0, zero, one)], axis=0)
    ones_bd = jnp.concatenate([ones_sub] * (KV // SUB), axis=0)
    q_pos = i * T + lax.broadcasted_iota(jnp.int32, (T, 2 * SUB), 0)
    k_off = lax.broadcasted_iota(jnp.int32, (T, 2 * SUB), 1) % SUB

    def split2(x):
        return jnp.concatenate([jnp.where(k0, x, zero), jnp.where(k0, zero, x)], axis=0)

    def pair_step(p, q2, j, state, masked):
        m_a, m_b, acc = state
        sl = pl.ds(pl.multiple_of(p * LANES, LANES), LANES)
        parts, v_rows = [], []
        mx = None
        for c in range(KV // SUB):
            ks = pl.ds(pl.multiple_of(j * KV + c * SUB, SUB), SUB)
            f2 = jnp.concatenate([f_ref[p, 0:1, ks], f_ref[p, 1:2, ks]], axis=1)
            s = _nt(q2, split2(k_ref[ks, sl])) - f2
            if masked:
                s = jnp.where(j * KV + c * SUB + k_off <= q_pos, s, -jnp.inf)
            parts.append(s)
            mx = s if mx is None else jnp.maximum(mx, s)
            v_rows.append(split2(v_ref[ks, sl]))
        yield
        new_a = jnp.maximum(m_a, jnp.max(mx[:, :SUB], axis=-1, keepdims=True))
        new_b = jnp.maximum(m_b, jnp.max(mx[:, SUB:], axis=-1, keepdims=True))
        pm = jnp.concatenate(
            [jnp.concatenate([jnp.exp2(s[:, :SUB] - new_a), jnp.exp2(s[:, SUB:] - new_b)],
                             axis=1).astype(BF16) for s in parts], axis=1)
        rhs = jnp.concatenate([jnp.concatenate(v_rows, axis=0), ones_bd], axis=1)
        pv = _mm(pm, rhs)
        yield
        alpha = jnp.where(t0, jnp.exp2(m_a - new_a), jnp.exp2(m_b - new_b))
        acc = acc * jnp.concatenate([alpha, alpha], axis=1) + pv
        return new_a, new_b, acc

    def group(gi, carry):
        pairs = [gi * G + n for n in range(G)]
        qs = [q_ref[:, pl.ds(pl.multiple_of(p * LANES, LANES), LANES)] for p in pairs]

        def kv_step(j, states, masked):
            return tuple(_round_robin(
                [pair_step(p, q2, j, st, masked) for p, q2, st in zip(pairs, qs, states)]))

        init = (jnp.full((T, 1), -jnp.inf, F32), jnp.full((T, 1), -jnp.inf, F32),
                jnp.zeros((T, 2 * LANES), F32))
        states = lax.fori_loop(0, n_full, lambda j, st: kv_step(j, st, False), (init,) * G)
        states = kv_step(n_full, states, True)
        for p, (_, _, acc) in zip(pairs, states):
            sl = pl.ds(pl.multiple_of(p * LANES, LANES), LANES)
            o = acc[:, :LANES] / acc[:, LANES:]
            o_ref[:, sl] = (o * _silu(gate_ref[:, sl].astype(F32))).astype(o_ref.dtype)
        return carry

    lax.fori_loop(0, N_PAIRS // G, group, 0)


def _fox_attention(qn, kn, f_cum, proj, bsz, seq):
    T = min(FOX_BLOCK, seq)
    nq = seq // T
    m = bsz * seq
    return pl.pallas_call(
        _fox_body,
        grid=(bsz, nq),
        in_specs=[
            pl.BlockSpec((T, D_MODEL), lambda b, i: (b * nq + i, 0)),
            pl.BlockSpec((seq, D_MODEL), lambda b, i: (b, 0)),
            pl.BlockSpec((seq, D_MODEL), lambda b, i: (b, 2)),
            pl.BlockSpec((N_PAIRS, 8, seq), lambda b, i: (0, 0, b)),
            pl.BlockSpec((T, D_MODEL), lambda b, i: (b * nq + i, 3)),
        ],
        out_specs=pl.BlockSpec((T, D_MODEL), lambda b, i: (b * nq + i, 0)),
        out_shape=jax.ShapeDtypeStruct((m, D_MODEL), BF16),
        compiler_params=_params("parallel", "arbitrary"),
        name="fox_attention",
    )(qn, kn, proj, f_cum, proj)


def _fox_layer(x2, bsz, seq, norm, w_in, b_f, q_gain, k_gain, w_out):
    W = D_MODEL
    proj = _norm_matmul(x2, norm, w_in[:, :4 * W].astype(BF16), name="fox_in_proj")
    k = w_in.shape[0]
    w_f = jnp.zeros((k, N_PAIRS, 8), F32).at[:, :, :2].set(w_in[:, 4 * W:].reshape(k, N_PAIRS, 2))
    b_f8 = jnp.zeros((N_PAIRS, 8), F32).at[:, :2].set(b_f.reshape(N_PAIRS, 2))
    _, f_rows = _gate_proj(x2, norm, w_f.reshape(k, N_PAIRS * 8))
    f_cum = _fox_forget(f_rows, b_f8.reshape(-1), bsz, seq).reshape(N_PAIRS, 8, bsz * seq)
    scale = HEAD_DIM ** -0.5 * LOG2_E
    qn = _head_norm(proj, 0, (jnp.tile(q_gain, N_HEADS) * scale).reshape(1, W).astype(F32))
    kn = _head_norm(proj, 1, jnp.tile(k_gain, N_HEADS).reshape(1, W).astype(F32))
    a = _fox_attention(qn, kn, f_cum, proj, bsz, seq)
    return _out_proj(a, w_out.astype(BF16), x2)


def kernel(x, rel_bias, swa_norm, swa_w_in, swa_q_gain, swa_k_gain, swa_sinks, swa_w_out, rwkv_norm, rwkv_mix, rwkv_w_in, rwkv_w0, rwkv_w_lora1, rwkv_w_lora2, rwkv_a0, rwkv_a_lora1, rwkv_a_lora2, rwkv_k_k, rwkv_k_a, rwkv_r_k, rwkv_gn_w, rwkv_gn_b, rwkv_w_out, mlstm_norm, mlstm_w_in, mlstm_b_i, mlstm_b_f, mlstm_h_gain, mlstm_w_out, fox_norm, fox_w_in, fox_b_f, fox_q_gain, fox_k_gain, fox_w_out):
    bsz, seq, d = x.shape
    depth = swa_norm.shape[0] + rwkv_norm.shape[0] + mlstm_norm.shape[0] + fox_norm.shape[0]
    x2 = x.reshape(bsz * seq, d)
    for layer in range(depth):
        kind, idx = layer % 4, layer // 4
        if kind == 0:
            x2 = _swa_layer(x2, bsz, seq, rel_bias, swa_norm[idx], swa_w_in[idx], swa_q_gain[idx],
                            swa_k_gain[idx], swa_sinks[idx], swa_w_out[idx])
        elif kind == 1:
            x2 = _rwkv_layer(x2, bsz, seq, rwkv_norm[idx], rwkv_mix[idx], rwkv_w_in[idx],
                             rwkv_w0[idx], rwkv_w_lora1[idx], rwkv_w_lora2[idx], rwkv_a0[idx],
                             rwkv_a_lora1[idx], rwkv_a_lora2[idx], rwkv_k_k[idx], rwkv_k_a[idx],
                             rwkv_r_k[idx], rwkv_gn_w[idx], rwkv_gn_b[idx], rwkv_w_out[idx])
        elif kind == 2:
            x2 = _mlstm_layer(x2, bsz, seq, mlstm_norm[idx], mlstm_w_in[idx], mlstm_b_i[idx],
                              mlstm_b_f[idx], mlstm_h_gain[idx], mlstm_w_out[idx])
        else:
            x2 = _fox_layer(x2, bsz, seq, fox_norm[idx], fox_w_in[idx], fox_b_f[idx],
                            fox_q_gain[idx], fox_k_gain[idx], fox_w_out[idx])
    return x2.reshape(bsz, seq, d)
```

```python
import functools
import math

import jax
import jax.numpy as jnp
from jax import lax
from jax.experimental import pallas as pl
from jax.experimental.pallas import tpu as pltpu

F32 = jnp.float32
BF16 = jnp.bfloat16
HIGHEST = lax.Precision.HIGHEST

D_MODEL = 2048
RMS_EPS = 1e-6
GN_EPS = 64e-5
LOG2_E = math.log2(math.e)

HEAD_DIM = 64
N_HEADS = D_MODEL // HEAD_DIM
LANES = 128
N_PAIRS = D_MODEL // LANES

SWA_KV_HEADS = 4
SWA_GROUP = N_HEADS // SWA_KV_HEADS
SWA_KV_WIDTH = SWA_KV_HEADS * HEAD_DIM
SWA_BLOCK = 128
SWA_PAIR_UNROLL = 8
REL_BUCKETS = 32
REL_MAX_DIST = 128

RWKV_CHUNK = 64
RWKV_TM = 512
RWKV_LORA_PAD = 128
RWKV_PAIR_UNROLL = 16

MLSTM_HEADS = 8
MLSTM_V_DIM = 256
MLSTM_QK_DIM = 128
MLSTM_CHUNK = 128
MLSTM_HEAD_UNROLL = 4
MLSTM_MAIN = 2 * MLSTM_HEADS * MLSTM_QK_DIM + 3 * MLSTM_HEADS * MLSTM_V_DIM

FOXT_BLOCK = 256
FOXT_HEADS = 8
FOXT_KV = 256
FOXT_UNROLL = 8
FOXT_V_ROWS = 80
FOX_BLOCK = 128
FOX_KV = 512
FOX_PAIR_UNROLL = 4

VMEM_LIMIT = 56 * 1024 * 1024
PROJ_TM = 1024
PROJ_TN_CHOICES = (1536, 1024, 512, 256, 128)


def _params(*sem):
    return pltpu.CompilerParams(dimension_semantics=sem, vmem_limit_bytes=VMEM_LIMIT)


def _nt(a, b, precision=None):
    return lax.dot_general(a, b, (((1,), (1,)), ((), ())), precision=precision,
                           preferred_element_type=F32)


def _tn(a, b, precision=None):
    return lax.dot_general(a, b, (((0,), (0,)), ((), ())), precision=precision,
                           preferred_element_type=F32)


def _mm(a, b, precision=None):
    return jnp.dot(a, b, precision=precision, preferred_element_type=F32)


def _rms_rows(x, g):
    return x * lax.rsqrt(jnp.mean(x * x, axis=-1, keepdims=True) + RMS_EPS) * g


def _silu(x):
    return x * jax.nn.sigmoid(x)


def _round_robin(gens):
    results = [None] * len(gens)
    live = list(range(len(gens)))
    while live:
        for n in list(live):
            try:
                next(gens[n])
            except StopIteration as done:
                results[n] = done.value
                live.remove(n)
    return results


def _norm_matmul_body(x_ref, g_ref, w_ref, o_ref, xn_ref):
    @pl.when(pl.program_id(1) == 0)
    def _():
        xn_ref[...] = _rms_rows(x_ref[...], g_ref[...]).astype(BF16)

    o_ref[...] = _mm(xn_ref[...], w_ref[...]).astype(o_ref.dtype)


def _proj_tiles(m, n):
    tm = min(PROJ_TM, m)
    tn = next(t for t in PROJ_TN_CHOICES if n % t == 0)
    return tm, tn


def _norm_matmul(x, g, w, *, name="norm_matmul"):
    m, k = x.shape
    n = w.shape[1]
    tm, tn = _proj_tiles(m, n)
    out_dtype = BF16
    return pl.pallas_call(
        _norm_matmul_body,
        grid=(m // tm, n // tn),
        in_specs=[
            pl.BlockSpec((tm, k), lambda i, j: (i, 0)),
            pl.BlockSpec((1, k), lambda i, j: (0, 0)),
            pl.BlockSpec((k, tn), lambda i, j: (0, j)),
        ],
        out_specs=pl.BlockSpec((tm, tn), lambda i, j: (i, j)),
        out_shape=jax.ShapeDtypeStruct((m, n), out_dtype),
        scratch_shapes=[pltpu.VMEM((tm, k), BF16)],
        compiler_params=_params("parallel", "arbitrary"),
        name=name,
    )(x, g.reshape(1, k), w)


def _gate_proj_body(x_ref, g_ref, w_ref, wt_ref, cols_ref, rows_ref):
    xn = _rms_rows(x_ref[...], g_ref[...]).astype(BF16)
    cols_ref[...] = _mm(xn, w_ref[...])
    rows_ref[...] = _nt(wt_ref[...], xn)


def _gate_proj(x, g, w_gates):
    m, k = x.shape
    n_g = w_gates.shape[1]
    tm = min(512, m)
    w_pad = jnp.zeros((k, LANES), BF16).at[:, :n_g].set(w_gates.astype(BF16))
    w_t = w_gates.T.astype(BF16)
    return pl.pallas_call(
        _gate_proj_body,
        grid=(m // tm,),
        in_specs=[
            pl.BlockSpec((tm, k), lambda i: (i, 0)),
            pl.BlockSpec((1, k), lambda i: (0, 0)),
            pl.BlockSpec((k, LANES), lambda i: (0, 0)),
            pl.BlockSpec((n_g, k), lambda i: (0, 0)),
        ],
        out_specs=[
            pl.BlockSpec((tm, LANES), lambda i: (i, 0)),
            pl.BlockSpec((n_g, tm), lambda i: (0, i)),
        ],
        out_shape=[
            jax.ShapeDtypeStruct((m, LANES), F32),
            jax.ShapeDtypeStruct((n_g, m), F32),
        ],
        compiler_params=_params("parallel"),
        name="gate_proj",
    )(x, g.reshape(1, k), w_pad, w_t)


def _out_proj_body(a_ref, w_ref, r_ref, o_ref):
    o_ref[...] = r_ref[...] + _mm(a_ref[...], w_ref[...])


def _out_proj(a, w, resid):
    m, k = a.shape
    n = w.shape[1]
    tm = min(512, m)
    return pl.pallas_call(
        _out_proj_body,
        grid=(m // tm,),
        in_specs=[
            pl.BlockSpec((tm, k), lambda i: (i, 0)),
            pl.BlockSpec((k, n), lambda i: (0, 0)),
            pl.BlockSpec((tm, n), lambda i: (i, 0)),
        ],
        out_specs=pl.BlockSpec((tm, n), lambda i: (i, 0)),
        out_shape=jax.ShapeDtypeStruct((m, n), F32),
        compiler_params=_params("parallel"),
        name="out_proj",
    )(a, w, resid)


def _t5_bucket(dist):
    max_exact = REL_BUCKETS // 2
    d_f = jnp.maximum(dist, 1).astype(F32)
    large = max_exact + (jnp.log(d_f / max_exact) / math.log(REL_MAX_DIST / max_exact)
                         * (REL_BUCKETS - max_exact)).astype(jnp.int32)
    large = jnp.minimum(large, REL_BUCKETS - 1)
    return jnp.where(dist < max_exact, dist, large)


def _swa_body(q_ref, gate_ref, kc_ref, kp_ref, vc_ref, vp_ref, bias_ref, sink_ref, qg_ref, kg_ref,
              o_ref):
    L = SWA_BLOCK
    k0 = lax.broadcasted_iota(jnp.int32, (2 * L, LANES), 1) < HEAD_DIM
    t0 = lax.broadcasted_iota(jnp.int32, (L, LANES), 1) < HEAD_DIM
    bi = lax.broadcasted_iota(jnp.int32, (LANES, LANES), 0) // HEAD_DIM
    bj = lax.broadcasted_iota(jnp.int32, (LANES, LANES), 1) // HEAD_DIM
    bd_ones = (bi == bj).astype(BF16)
    zero = jnp.zeros((2 * L, LANES), F32)
    one = jnp.ones((2 * L, LANES), F32)
    ones_bd = jnp.concatenate([jnp.where(k0, one, zero), jnp.where(k0, zero, one)], axis=0).astype(BF16)

    def head_mean_sq(z):
        return _mm((z * z).astype(BF16), bd_ones) * (1.0 / HEAD_DIM)

    def both_halves(x, head_in_low_lanes):
        if head_in_low_lanes:
            lo = jnp.where(k0, x, zero)
            hi = pltpu.roll(lo, HEAD_DIM, axis=1)
        else:
            hi = jnp.where(k0, zero, x)
            lo = pltpu.roll(hi, HEAD_DIM, axis=1)
        return jnp.concatenate([lo, hi], axis=0)

    k_all = jnp.concatenate([kp_ref[...], kc_ref[...]], axis=0).astype(F32)
    v_all = jnp.concatenate([vp_ref[...], vc_ref[...]], axis=0).astype(F32)
    kbd, rhs = [], []
    for t in range(SWA_KV_WIDTH // LANES):
        kt = k_all[:, t * LANES:(t + 1) * LANES]
        kn = kt * lax.rsqrt(head_mean_sq(kt) + RMS_EPS) * kg_ref[...]
        vt = v_all[:, t * LANES:(t + 1) * LANES]
        for low in (True, False):
            kbd.append(both_halves(kn, low).astype(BF16))
            rhs.append(jnp.concatenate([both_halves(vt, low).astype(BF16), ones_bd], axis=1))

    def pair(p):
        kh = p // (SWA_GROUP // 2)
        sl = slice(p * LANES, (p + 1) * LANES)
        q2 = q_ref[:, sl].astype(F32)
        ms = head_mean_sq(q2)
        yield
        qn = (q2 * lax.rsqrt(ms + RMS_EPS) * qg_ref[...]).astype(BF16)
        s = _nt(qn, kbd[kh]) + bias_ref[0, p]
        yield
        sink_a = sink_ref[2 * p]
        sink_b = sink_ref[2 * p + 1]
        m_a = jnp.maximum(jnp.max(s[:, :2 * L], axis=-1, keepdims=True), sink_a)
        m_b = jnp.maximum(jnp.max(s[:, 2 * L:], axis=-1, keepdims=True), sink_b)
        pm = jnp.concatenate([jnp.exp(s[:, :2 * L] - m_a), jnp.exp(s[:, 2 * L:] - m_b)], axis=1)
        pv = _mm(pm.astype(BF16), rhs[kh])
        yield
        sink_t = jnp.where(t0, jnp.exp(sink_a - m_a), jnp.exp(sink_b - m_b))
        o = pv[:, :LANES] / (pv[:, LANES:] + sink_t)
        return (o * _silu(gate_ref[:, sl].astype(F32))).astype(o_ref.dtype)

    for g0 in range(0, N_PAIRS, SWA_PAIR_UNROLL):
        outs = _round_robin([pair(p) for p in range(g0, g0 + SWA_PAIR_UNROLL)])
        for n, o in enumerate(outs):
            o_ref[:, (g0 + n) * LANES:(g0 + n + 1) * LANES] = o


def _swa_attention(proj, bias, sinks, q_gain2, k_gain2, bsz, seq):
    L = SWA_BLOCK
    nb = seq // L
    m = bsz * seq
    kcol = 2 * D_MODEL // SWA_KV_WIDTH
    prev = lambda r: jnp.where(r % nb == 0, r, r - 1)
    return pl.pallas_call(
        _swa_body,
        grid=(bsz * nb,),
        in_specs=[
            pl.BlockSpec((L, D_MODEL), lambda r: (r, 0)),
            pl.BlockSpec((L, D_MODEL), lambda r: (r, 1)),
            pl.BlockSpec((L, SWA_KV_WIDTH), lambda r: (r, kcol)),
            pl.BlockSpec((L, SWA_KV_WIDTH), lambda r: (prev(r), kcol)),
            pl.BlockSpec((L, SWA_KV_WIDTH), lambda r: (r, kcol + 1)),
            pl.BlockSpec((L, SWA_KV_WIDTH), lambda r: (prev(r), kcol + 1)),
            pl.BlockSpec((1, N_PAIRS, L, 4 * L), lambda r: (jnp.where(r % nb == 0, 0, 1), 0, 0, 0)),
            pl.BlockSpec(memory_space=pltpu.SMEM),
            pl.BlockSpec((1, LANES), lambda r: (0, 0)),
            pl.BlockSpec((1, LANES), lambda r: (0, 0)),
        ],
        out_specs=pl.BlockSpec((L, D_MODEL), lambda r: (r, 0)),
        out_shape=jax.ShapeDtypeStruct((m, D_MODEL), BF16),
        compiler_params=_params("parallel"),
        name="swa_attention",
    )(proj, proj, proj, proj, proj, proj, bias, sinks, q_gain2, k_gain2)


def _swa_bias_tables(rel_bias):
    L = SWA_BLOCK
    qi = jnp.arange(L)[:, None]
    kj = jnp.arange(2 * L)[None, :]
    dist = qi + L - kj
    in_band = (dist >= 0) & (dist < L)
    bias = rel_bias.astype(F32)[_t5_bucket(jnp.maximum(dist, 0))].transpose(2, 0, 1)
    general = jnp.where(in_band[None], bias, -jnp.inf)
    first = jnp.where((in_band & (kj >= L))[None], bias, -jnp.inf)
    pairs = lambda t: t.reshape(N_PAIRS, 2, L, 2 * L).transpose(0, 2, 1, 3).reshape(N_PAIRS, L, 4 * L)
    return jnp.stack([pairs(first), pairs(general)])


def _swa_layer(x2, bsz, seq, rel_bias, norm, w_in, q_gain, k_gain, sinks, w_out):
    W = D_MODEL
    kvw = SWA_KV_WIDTH
    w_perm = jnp.concatenate([w_in[:, :W], w_in[:, W + 2 * kvw:], w_in[:, W:W + 2 * kvw]], axis=1)
    proj = _norm_matmul(x2, norm, w_perm.astype(BF16), name="swa_in_proj")
    q_gain2 = (jnp.tile(q_gain, 2) * HEAD_DIM ** -0.5).reshape(1, LANES).astype(F32)
    k_gain2 = jnp.tile(k_gain, 2).reshape(1, LANES).astype(F32)
    a = _swa_attention(proj, _swa_bias_tables(rel_bias), sinks.astype(F32), q_gain2, k_gain2, bsz, seq)
    return _out_proj(a, w_out.astype(BF16), x2)


def _shifted_mix(x_ref, xp_ref, g_ref, seq_start):
    h = _rms_rows(x_ref[...], g_ref[...])
    h_prev_row = _rms_rows(xp_ref[7:8, :], g_ref[...])
    h_prev_row = jnp.where(seq_start, 0.0, h_prev_row)
    rolled = pltpu.roll(h, 1, axis=0)
    row = lax.broadcasted_iota(jnp.int32, h.shape, 0)
    shifted = jnp.where(row == 0, h_prev_row, rolled)
    return h, shifted - h


def _rwkv_in_body(tiles_per_seq, x_ref, xp_ref, g_ref, mix_ref, w_ref, o_ref, xm_ref):
    i = pl.program_id(0)

    @pl.when((pl.program_id(1) == 0) & (pl.program_id(2) == 0))
    def _():
        h, xx = _shifted_mix(x_ref, xp_ref, g_ref, i % tiles_per_seq == 0)
        for c in range(4):
            xm_ref[c] = (h + xx * mix_ref[c:c + 1, :]).astype(BF16)

    c = pl.program_id(1)
    o_ref[0] = _mm(xm_ref[c], w_ref[0]).astype(o_ref.dtype)


def _rwkv_in_proj(x2, seq, norm, mix8, w4):
    m, k = x2.shape
    tm = min(RWKV_TM, seq)
    tn = k
    rows8 = tm // 8
    return pl.pallas_call(
        functools.partial(_rwkv_in_body, seq // tm),
        grid=(m // tm, 4, k // tn),
        in_specs=[
            pl.BlockSpec((tm, k), lambda i, c, j: (i, 0)),
            pl.BlockSpec((8, k), lambda i, c, j: (jnp.maximum(i * rows8 - 1, 0), 0)),
            pl.BlockSpec((1, k), lambda i, c, j: (0, 0)),
            pl.BlockSpec((8, k), lambda i, c, j: (0, 0)),
            pl.BlockSpec((1, k, tn), lambda i, c, j: (c, 0, j)),
        ],
        out_specs=pl.BlockSpec((1, tm, tn), lambda i, c, j: (c, i, j)),
        out_shape=jax.ShapeDtypeStruct((4, m, k), BF16),
        scratch_shapes=[pltpu.VMEM((4, tm, k), BF16)],
        compiler_params=_params("parallel", "arbitrary", "arbitrary"),
        name="rwkv_in_proj",
    )(x2, x2, norm.reshape(1, k), mix8, w4)


def _rwkv_lora_body(tiles_per_seq, x_ref, xp_ref, g_ref, mix_ref, w1_ref, w2_ref, a1_ref, a2_ref,
                    w0_ref, a0_ref, lw_ref, a_ref):
    i = pl.program_id(0)
    h, xx = _shifted_mix(x_ref, xp_ref, g_ref, i % tiles_per_seq == 0)
    xw = (h + xx * mix_ref[4:5, :]).astype(BF16)
    xa = (h + xx * mix_ref[5:6, :]).astype(BF16)
    t = jnp.tanh(_mm(xw, w1_ref[...])).astype(BF16)
    w_pre = w0_ref[...] + _mm(t, w2_ref[...])
    lw_ref[...] = -jnp.exp(-jax.nn.softplus(-w_pre) - 0.5)
    u = _mm(xa, a1_ref[...]).astype(BF16)
    a_ref[...] = jax.nn.sigmoid(a0_ref[...] + _mm(u, a2_ref[...]))


def _rwkv_lora(x2, seq, norm, mix8, w1, w2, a1, a2, w0, a0):
    m, k = x2.shape
    tm = min(RWKV_TM, seq)
    rows8 = tm // 8
    P = RWKV_LORA_PAD
    full = lambda shape: pl.BlockSpec(shape, lambda i: tuple(0 for _ in shape))
    return pl.pallas_call(
        functools.partial(_rwkv_lora_body, seq // tm),
        grid=(m // tm,),
        in_specs=[
            pl.BlockSpec((tm, k), lambda i: (i, 0)),
            pl.BlockSpec((8, k), lambda i: (jnp.maximum(i * rows8 - 1, 0), 0)),
            full((1, k)), full((8, k)),
            full((k, P)), full((P, k)), full((k, P)), full((P, k)),
            full((1, k)), full((1, k)),
        ],
        out_specs=[pl.BlockSpec((tm, k), lambda i: (i, 0)), pl.BlockSpec((tm, k), lambda i: (i, 0))],
        out_shape=[jax.ShapeDtypeStruct((m, k), F32), jax.ShapeDtypeStruct((m, k), F32)],
        compiler_params=_params("parallel"),
        name="rwkv_lora",
    )(x2, x2, norm.reshape(1, k), mix8, w1, w2, a1, a2, w0.reshape(1, k), a0.reshape(1, k))


def _stack2(z, m0):
    zero = jnp.zeros_like(z)
    return jnp.concatenate([jnp.where(m0, z, zero), jnp.where(m0, zero, z)], axis=0)


def _rwkv_scan_body(r_ref, k_ref, v_ref, g_ref, lw_ref, a_ref, kk_ref, ka_ref, rk_ref,
                    gnw_ref, gnb_ref, o_ref, ht_ref):
    C = RWKV_CHUNK

    @pl.when(pl.program_id(1) == 0)
    def _():
        ht_ref[...] = jnp.zeros_like(ht_ref)

    lane = lax.broadcasted_iota(jnp.int32, (C, LANES), 1)
    tok = lax.broadcasted_iota(jnp.int32, (C, LANES), 0)
    m0 = lane < HEAD_DIM
    col = lane % HEAD_DIM
    strict = tok > col
    incl = tok >= col
    ti = lax.broadcasted_iota(jnp.int32, (C, C), 0)
    tj = lax.broadcasted_iota(jnp.int32, (C, C), 1)
    tri = (ti >= tj).astype(F32)
    bi = lax.broadcasted_iota(jnp.int32, (LANES, LANES), 0) // HEAD_DIM
    bj = lax.broadcasted_iota(jnp.int32, (LANES, LANES), 1) // HEAD_DIM
    bd = bi == bj
    bd_ones = bd.astype(BF16)
    tri = tri.astype(BF16)

    def head_sum(z):
        return _mm(z.astype(BF16), bd_ones)

    def load(p):
        sl = pl.ds(pl.multiple_of(p * LANES, LANES), LANES)
        f32 = lambda ref: ref[0, :, sl].astype(F32)
        return (f32(r_ref), f32(k_ref), f32(v_ref), f32(g_ref), lw_ref[:, sl],
                a_ref[:, sl], kk_ref[:, sl], ka_ref[:, sl], rk_ref[:, sl], gnw_ref[:, sl],
                gnb_ref[:, sl], ht_ref[p])

    def compute(r, k, v, g, lw, a, k_k, k_a, r_k, gn_w, gn_b, ht):
        kkp = k * k_k
        ss = head_sum(kkp * kkp)
        lw_hi = lw.astype(BF16)
        lw_lo = (lw - lw_hi.astype(F32)).astype(BF16)
        cum = _mm(tri, lw_hi) + _mm(tri, lw_lo)
        yield
        kk = kkp / jnp.maximum(jnp.sqrt(ss), 1e-12)
        kp = k * (1.0 + (a - 1.0) * k_a)
        cum_end = cum[C - 1:C, :]
        g_in = jnp.exp(cum)
        g_prev = jnp.exp(cum - lw)
        g_inv = jnp.exp(-cum)
        g_end = jnp.exp(cum_end - cum)
        kka = kk * a
        a_t = -kk * g_prev
        r_t = r * g_in
        b_t = kka * g_inv
        k_t = kp * g_inv

        x1 = jnp.concatenate([a_t, r_t], axis=0).astype(BF16)
        bk2 = jnp.concatenate([_stack2(b_t, m0), _stack2(k_t, m0)], axis=0).astype(BF16)
        gbk = _nt(x1, bk2)
        xh = _nt(x1, ht.astype(BF16))
        yield
        zero = jnp.zeros((C, LANES), F32)
        l_ab = jnp.where(strict, gbk[:C, :LANES], zero)
        l_ak = jnp.where(strict, gbk[:C, LANES:], zero)
        p_rb = jnp.where(incl, gbk[C:, :LANES], zero)
        p_rk = jnp.where(incl, gbk[C:, LANES:], zero)

        t_inv = jnp.where(tok == col, 1.0, zero) + jnp.where(
            (tok // 2 == col // 2) & (tok % 2 == 1) & (col % 2 == 0), l_ab, zero)
        v2 = _stack2(v, m0).astype(BF16)
        rhs = xh[:C] + _mm(l_ak.astype(BF16), v2)
        bonus_dot = head_sum(r * kp * r_k)
        s = 2
        while s < C:
            sel = (tok // (2 * s) == col // (2 * s)) & (tok % (2 * s) >= s) & (col % (2 * s) < s)
            l_s = jnp.where(sel, l_ab, zero)
            tl = _mm(t_inv.astype(BF16), _stack2(l_s, m0).astype(BF16))
            yield
            t_inv = t_inv + _mm(tl.astype(BF16), _stack2(t_inv, m0).astype(BF16))
            yield
            s *= 2

        u = _mm(t_inv.astype(BF16), _stack2(rhs, m0).astype(BF16))
        yield
        u2 = _stack2(u, m0).astype(BF16)
        p_cat = jnp.concatenate([p_rb, p_rk], axis=1).astype(BF16)
        y = xh[C:] + _mm(p_cat, jnp.concatenate([u2, v2], axis=0))
        uv = jnp.concatenate([u, v], axis=0).astype(BF16)
        bk_end = jnp.concatenate([kka * g_end, kp * g_end], axis=0).astype(BF16)
        upd = _tn(uv, bk_end)
        yield
        ht_new = ht * jnp.exp(cum_end) + jnp.where(bd, upd, 0.0)
        mu = head_sum(y) * (1.0 / HEAD_DIM)
        yield
        dy = y - mu
        var = head_sum(dy * dy) * (1.0 / HEAD_DIM)
        yield
        yn = dy * lax.rsqrt(var + GN_EPS) * gn_w + gn_b
        return ((yn + bonus_dot * v) * _silu(g)).astype(o_ref.dtype), ht_new

    def group(gi, carry):
        base = gi * RWKV_PAIR_UNROLL
        ins = [load(base + n) for n in range(RWKV_PAIR_UNROLL)]
        outs = _round_robin([compute(*x) for x in ins])
        for n, (o, ht_new) in enumerate(outs):
            p = base + n
            o_ref[:, pl.ds(pl.multiple_of(p * LANES, LANES), LANES)] = o
            ht_ref[p] = ht_new
        return carry

    lax.fori_loop(0, N_PAIRS // RWKV_PAIR_UNROLL, group, 0)


def _rwkv_scan(rkvg, lw, a, k_k, k_a, r_k, gn_w, gn_b, bsz, seq):
    C = RWKV_CHUNK
    nc = seq // C
    m = bsz * seq
    k = D_MODEL
    part = lambda c: pl.BlockSpec((1, C, k), lambda b, j, c=c: (c, b * nc + j, 0))
    tile = pl.BlockSpec((C, k), lambda b, j: (b * nc + j, 0))
    vec = pl.BlockSpec((1, k), lambda b, j: (0, 0))
    return pl.pallas_call(
        _rwkv_scan_body,
        grid=(bsz, nc),
        in_specs=[part(0), part(1), part(2), part(3), tile, tile, vec, vec, vec, vec, vec],
        out_specs=tile,
        out_shape=jax.ShapeDtypeStruct((m, k), BF16),
        scratch_shapes=[pltpu.VMEM((N_PAIRS, LANES, LANES), F32)],
        compiler_params=_params("parallel", "arbitrary"),
        name="rwkv_scan",
    )(rkvg, rkvg, rkvg, rkvg, lw, a, k_k.reshape(1, k), k_a.reshape(1, k), r_k.reshape(1, k),
      gn_w.reshape(1, k), gn_b.reshape(1, k))


def _rwkv_layer(x2, bsz, seq, norm, mix, w_in, w0, w_lora1, w_lora2, a0, a_lora1, a_lora2,
                k_k, k_a, r_k, gn_w, gn_b, w_out):
    k = D_MODEL
    P = RWKV_LORA_PAD
    mix8 = jnp.zeros((8, k), F32).at[:mix.shape[0]].set(mix)
    pad_cols = lambda w: jnp.zeros((k, P), BF16).at[:, :w.shape[1]].set(w.astype(BF16))
    pad_rows = lambda w: jnp.zeros((P, k), BF16).at[:w.shape[0]].set(w.astype(BF16))
    rkvg = _rwkv_in_proj(x2, seq, norm, mix8, w_in.astype(BF16))
    lw, a = _rwkv_lora(x2, seq, norm, mix8, pad_cols(w_lora1), pad_rows(w_lora2),
                       pad_cols(a_lora1), pad_rows(a_lora2), w0, a0)
    y = _rwkv_scan(rkvg, lw, a, k_k, k_a, r_k, gn_w, gn_b, bsz, seq)
    return _out_proj(y, w_out.astype(BF16), x2)


def _mlstm_body(q_ref, k_ref, v_ref, op_ref, gate_ref, gc_ref, gr_ref, bi_ref, bf_ref, hg_ref,
                o_ref, ct_ref, m_ref):
    L = MLSTM_CHUNK
    H = MLSTM_HEADS
    dk = MLSTM_QK_DIM
    dv = MLSTM_V_DIM

    @pl.when(pl.program_id(1) == 0)
    def _():
        ct_ref[...] = jnp.zeros_like(ct_ref)
        m_ref[...] = jnp.zeros_like(m_ref)

    ti = lax.broadcasted_iota(jnp.int32, (L, L), 0)
    tj = lax.broadcasted_iota(jnp.int32, (L, L), 1)
    tril = ti >= tj
    tri_f = tril.astype(F32)
    gc = gc_ref[...] + bi_ref[...]
    gr = gr_ref[...] + bf_ref[...]
    lane = lax.broadcasted_iota(jnp.int32, gc.shape, 1)
    f_cols = jnp.where((lane >= H) & (lane < 2 * H), jax.nn.log_sigmoid(gc), 0.0)
    b_cols = _mm(tri_f, f_cols, HIGHEST)
    f_rows = jax.nn.log_sigmoid(gr)
    b_rows = _nt(f_rows, tri_f, HIGHEST)
    ones_blk = jnp.ones((L, LANES), BF16)

    def head(h):
        q = q_ref[:, h * dk:(h + 1) * dk]
        k = k_ref[:, h * dk:(h + 1) * dk]
        v = v_ref[:, h * dv:(h + 1) * dv]
        v_aug = jnp.concatenate([v, ones_blk], axis=-1)
        ct = ct_ref[h]
        m_prev = m_ref[h][:, :1]
        qk = _nt(q, k)
        q_ct = _mm(q, ct.astype(BF16))
        yield
        i_col = gc[:, h:h + 1]
        i_row = gr[h:h + 1, :]
        b_col = b_cols[:, H + h:H + h + 1]
        b_row = b_rows[H + h:H + h + 1, :]
        g_tot = b_col[L - 1:L, :]
        dmat = jnp.where(tril, b_col - b_row + i_row, -jnp.inf)
        inter = b_col + m_prev
        m_t = jnp.maximum(inter, jnp.max(dmat, axis=-1, keepdims=True))
        w_intra = jnp.exp(dmat - m_t) * dk ** -0.5
        w_inter = jnp.exp(inter - m_t) * dk ** -0.5
        sw = qk * w_intra
        num = w_inter * q_ct + _mm(sw.astype(BF16), v_aug)
        decay = g_tot - b_col + i_col
        m_new = jnp.maximum(g_tot + m_prev, jnp.max(decay, axis=0, keepdims=True))
        ws = jnp.exp(decay - m_new)
        carry_scale = jnp.exp(g_tot + m_prev - m_new)
        wv = (ws * v_aug.astype(F32)).astype(BF16)
        upd = _tn(k, wv)
        yield
        den = num[:, dv:dv + 1]
        h_t = num[:, :dv] / jnp.maximum(jnp.abs(den), jnp.exp(-m_t))
        hn = _rms_rows(h_t, hg_ref[:, h * dv:(h + 1) * dv])
        osl = slice(h * dv, (h + 1) * dv)
        out = hn * jax.nn.sigmoid(op_ref[:, osl].astype(F32)) * _silu(gate_ref[:, osl].astype(F32))
        return out.astype(o_ref.dtype), carry_scale * ct + upd, jnp.broadcast_to(m_new, (1, LANES))

    for h0 in range(0, H, MLSTM_HEAD_UNROLL):
        heads = range(h0, h0 + MLSTM_HEAD_UNROLL)
        for h, (out, ct_new, m_new) in zip(heads, _round_robin([head(h) for h in heads])):
            o_ref[:, h * dv:(h + 1) * dv] = out
            ct_ref[h] = ct_new
            m_ref[h] = m_new


def _mlstm_scan(proj, g_cols, g_rows, b_i, b_f, h_gain, bsz, seq):
    L = MLSTM_CHUNK
    H = MLSTM_HEADS
    nc = seq // L
    m = bsz * seq
    qkw = H * MLSTM_QK_DIM
    vw = H * MLSTM_V_DIM
    bias = jnp.concatenate([b_i, b_f]).astype(F32)
    bias_row = jnp.zeros((1, LANES), F32).at[0, :2 * H].set(bias)
    bias_col = bias.reshape(2 * H, 1)
    row = lambda b, j: b * nc + j
    return pl.pallas_call(
        _mlstm_body,
        grid=(bsz, nc),
        in_specs=[
            pl.BlockSpec((L, qkw), lambda b, j: (row(b, j), 0)),
            pl.BlockSpec((L, qkw), lambda b, j: (row(b, j), 1)),
            pl.BlockSpec((L, vw), lambda b, j: (row(b, j), 1)),
            pl.BlockSpec((L, vw), lambda b, j: (row(b, j), 2)),
            pl.BlockSpec((L, vw), lambda b, j: (row(b, j), 3)),
            pl.BlockSpec((L, LANES), lambda b, j: (row(b, j), 0)),
            pl.BlockSpec((2 * H, L), lambda b, j: (0, row(b, j))),
            pl.BlockSpec((1, LANES), lambda b, j: (0, 0)),
            pl.BlockSpec((2 * H, 1), lambda b, j: (0, 0)),
            pl.BlockSpec((1, vw), lambda b, j: (0, 0)),
        ],
        out_specs=pl.BlockSpec((L, vw), lambda b, j: (row(b, j), 0)),
        out_shape=jax.ShapeDtypeStruct((m, vw), BF16),
        scratch_shapes=[
            pltpu.VMEM((H, MLSTM_QK_DIM, MLSTM_V_DIM + LANES), F32),
            pltpu.VMEM((H, 1, LANES), F32),
        ],
        compiler_params=_params("parallel", "arbitrary"),
        name="mlstm_scan",
    )(proj, proj, proj, proj, proj, g_cols, g_rows, bias_row, bias_col, h_gain.reshape(1, vw))


def _mlstm_layer(x2, bsz, seq, norm, w_in, b_i, b_f, h_gain, w_out):
    proj = _norm_matmul(x2, norm, w_in[:, :MLSTM_MAIN].astype(BF16), name="mlstm_in_proj")
    g_cols, g_rows = _gate_proj(x2, norm, w_in[:, MLSTM_MAIN:])
    a = _mlstm_scan(proj, g_cols, g_rows, b_i, b_f, h_gain, bsz, seq)
    return _out_proj(a, w_out.astype(BF16), x2)


def _head_norm_body(x_ref, g_ref, o_ref):
    x = x_ref[...].astype(F32)
    bi = lax.broadcasted_iota(jnp.int32, (LANES, LANES), 0) // HEAD_DIM
    bj = lax.broadcasted_iota(jnp.int32, (LANES, LANES), 1) // HEAD_DIM
    bd_mean = jnp.where(bi == bj, 1.0 / HEAD_DIM, 0.0).astype(BF16)
    for c in range(x.shape[1] // LANES):
        xc = x[:, c * LANES:(c + 1) * LANES]
        ms = _mm((xc * xc).astype(BF16), bd_mean)
        o_ref[:, c * LANES:(c + 1) * LANES] = (
            xc * lax.rsqrt(ms + RMS_EPS) * g_ref[:, c * LANES:(c + 1) * LANES]).astype(o_ref.dtype)


def _head_norm(proj, col_block, gain_row):
    m = proj.shape[0]
    tm = min(512, m)
    return pl.pallas_call(
        _head_norm_body,
        grid=(m // tm,),
        in_specs=[
            pl.BlockSpec((tm, D_MODEL), lambda i: (i, col_block)),
            pl.BlockSpec((1, D_MODEL), lambda i: (0, 0)),
        ],
        out_specs=pl.BlockSpec((tm, D_MODEL), lambda i: (i, 0)),
        out_shape=jax.ShapeDtypeStruct((m, D_MODEL), BF16),
        compiler_params=_params("parallel"),
        name="head_norm",
    )(proj, gain_row)


def _fox_forget_body(f_ref, bf_ref, o_ref):
    B = LANES
    seq = f_ref.shape[1]
    lf = jax.nn.log_sigmoid(f_ref[...] + bf_ref[...])
    ti = lax.broadcasted_iota(jnp.int32, (B, B), 0)
    tj = lax.broadcasted_iota(jnp.int32, (B, B), 1)
    triu = (ti <= tj).astype(F32)
    carry = jnp.zeros((lf.shape[0], 1), F32)
    for c in range(seq // B):
        blk = _mm(lf[:, c * B:(c + 1) * B], triu, HIGHEST) + carry
        o_ref[:, c * B:(c + 1) * B] = blk * LOG2_E
        carry = blk[:, B - 1:B]


def _fox_forget(f_rows, b_f, bsz, seq):
    H = f_rows.shape[0]
    return pl.pallas_call(
        _fox_forget_body,
        grid=(bsz,),
        in_specs=[pl.BlockSpec((H, seq), lambda b: (0, b)), pl.BlockSpec((H, 1), lambda b: (0, 0))],
        out_specs=pl.BlockSpec((H, seq), lambda b: (0, b)),
        out_shape=jax.ShapeDtypeStruct((H, bsz * seq), F32),
        compiler_params=_params("parallel"),
        name="fox_forget",
    )(f_rows, b_f.reshape(H, 1).astype(F32))


def _fox_body(q_ref, k_ref, v_ref, f_ref, gate_ref, o_ref):
    T = q_ref.shape[0]
    KV = min(FOX_KV, k_ref.shape[0])
    G = FOX_PAIR_UNROLL
    i = pl.program_id(1)
    n_full = (i * T) // KV
    assert KV % T == 0
    SUB = LANES
    k0 = lax.broadcasted_iota(jnp.int32, (SUB, LANES), 1).astype(F32).astype(BF16) < HEAD_DIM
    t0 = lax.broadcasted_iota(jnp.int32, (T, LANES), 1) < HEAD_DIM
    one = jnp.ones((SUB, LANES), BF16)
    zero = jnp.zeros((SUB, LANES), BF16)
    ones_sub = jnp.concatenate([jnp.where(k0, one, zero), jnp.where(k0, zero, one)], axis=0)
    ones_bd = jnp.concatenate([ones_sub] * (KV // SUB), axis=0)
    q_pos = i * T + lax.broadcasted_iota(jnp.int32, (T, 2 * SUB), 0)
    k_off = lax.broadcasted_iota(jnp.int32, (T, 2 * SUB), 1) % SUB

    def split2(x):
        return jnp.concatenate([jnp.where(k0, x, zero), jnp.where(k0, zero, x)], axis=0)

    def pair_step(p, q2, j, state, masked):
        m_a, m_b, acc = state
        sl = pl.ds(pl.multiple_of(p * LANES, LANES), LANES)
        parts, v_rows = [], []
        mx = None
        for c in range(KV // SUB):
            ks = pl.ds(pl.multiple_of(j * KV + c * SUB, SUB), SUB)
            f2 = jnp.concatenate([f_ref[p, 0:1, ks], f_ref[p, 1:2, ks]], axis=1)
            s = _nt(q2, split2(k_ref[ks, sl])) - f2
            if masked:
                s = jnp.where(j * KV + c * SUB + k_off <= q_pos, s, -jnp.inf)
            parts.append(s)
            mx = s if mx is None else jnp.maximum(mx, s)
            v_rows.append(split2(v_ref[ks, sl]))
        yield
        new_a = jnp.maximum(m_a, jnp.max(mx[:, :SUB], axis=-1, keepdims=True))
        new_b = jnp.maximum(m_b, jnp.max(mx[:, SUB:], axis=-1, keepdims=True))
        pm = jnp.concatenate(
            [jnp.concatenate([jnp.exp2(s[:, :SUB] - new_a), jnp.exp2(s[:, SUB:] - new_b)],
                             axis=1).astype(BF16) for s in parts], axis=1)
        rhs = jnp.concatenate([jnp.concatenate(v_rows, axis=0), ones_bd], axis=1)
        pv = _mm(pm, rhs)
        yield
        alpha = jnp.where(t0, jnp.exp2(m_a - new_a), jnp.exp2(m_b - new_b))
        acc = acc * jnp.concatenate([alpha, alpha], axis=1) + pv
        return new_a, new_b, acc

    def group(gi, carry):
        pairs = [gi * G + n for n in range(G)]
        qs = [q_ref[:, pl.ds(pl.multiple_of(p * LANES, LANES), LANES)] for p in pairs]

        def kv_step(j, states, masked):
            return tuple(_round_robin(
                [pair_step(p, q2, j, st, masked) for p, q2, st in zip(pairs, qs, states)]))

        init = (jnp.full((T, 1), -jnp.inf, F32), jnp.full((T, 1), -jnp.inf, F32),
                jnp.zeros((T, 2 * LANES), F32))
        states = lax.fori_loop(0, n_full, lambda j, st: kv_step(j, st, False), (init,) * G)
        states = kv_step(n_full, states, True)
        for p, (_, _, acc) in zip(pairs, states):
            sl = pl.ds(pl.multiple_of(p * LANES, LANES), LANES)
            o = acc[:, :LANES] / acc[:, LANES:]
            o_ref[:, sl] = (o * _silu(gate_ref[:, sl].astype(F32))).astype(o_ref.dtype)
        return carry

    lax.fori_loop(0, N_PAIRS // G, group, 0)


def _fox_attention(qn, kn, f_cum, proj, bsz, seq):
    T = min(FOX_BLOCK, seq)
    nq = seq // T
    m = bsz * seq
    return pl.pallas_call(
        _fox_body,
        grid=(bsz, nq),
        in_specs=[
            pl.BlockSpec((T, D_MODEL), lambda b, i: (b * nq + i, 0)),
            pl.BlockSpec((seq, D_MODEL), lambda b, i: (b, 0)),
            pl.BlockSpec((seq, D_MODEL), lambda b, i: (b, 2)),
            pl.BlockSpec((N_PAIRS, 8, seq), lambda b, i: (0, 0, b)),
            pl.BlockSpec((T, D_MODEL), lambda b, i: (b * nq + i, 3)),
        ],
        out_specs=pl.BlockSpec((T, D_MODEL), lambda b, i: (b * nq + i, 0)),
        out_shape=jax.ShapeDtypeStruct((m, D_MODEL), BF16),
        compiler_params=_params("parallel", "arbitrary"),
        name="fox_attention",
    )(qn, kn, proj, f_cum, proj)


def _fox_decay_body(f_ref, bf_ref, o_ref):
    B = LANES
    seq = f_ref.shape[0]
    lf = jax.nn.log_sigmoid(f_ref[...] + bf_ref[...])
    ti = lax.broadcasted_iota(jnp.int32, (B, B), 0)
    tj = lax.broadcasted_iota(jnp.int32, (B, B), 1)
    tril = (ti >= tj).astype(F32)
    carry = jnp.zeros((1, LANES), F32)
    for c in range(seq // B):
        blk = _mm(tril, lf[c * B:(c + 1) * B, :], HIGHEST) + carry
        carry = blk[B - 1:B, :]
        neg = blk * (-LOG2_E)
        hi = neg.astype(BF16)
        r1 = neg - hi.astype(F32)
        mid = r1.astype(BF16)
        lo = (r1 - mid.astype(F32)).astype(BF16)
        rows = slice(c * B, (c + 1) * B)
        o_ref[rows, 0:LANES] = hi
        o_ref[rows, LANES:2 * LANES] = mid
        o_ref[rows, 2 * LANES:3 * LANES] = lo


def _fox_decay(f_cols, b_f, bsz, seq):
    bias = jnp.zeros((1, LANES), F32).at[0, :N_HEADS].set(b_f.astype(F32))
    return pl.pallas_call(
        _fox_decay_body,
        grid=(bsz,),
        in_specs=[pl.BlockSpec((seq, LANES), lambda b: (b, 0)), pl.BlockSpec((1, LANES), lambda b: (0, 0))],
        out_specs=pl.BlockSpec((seq, 3 * LANES), lambda b: (b, 0)),
        out_shape=jax.ShapeDtypeStruct((bsz * seq, 3 * LANES), BF16),
        compiler_params=_params("parallel"),
        name="fox_decay",
    )(f_cols, bias)


def _fox_prep_body(q_ref, k_ref, v_ref, f3_ref, qg_ref, kg_ref, ka_ref, qt_ref, vt_ref):
    S = 2 * LANES
    r = lax.broadcasted_iota(jnp.int32, (LANES, S), 0)
    c = lax.broadcasted_iota(jnp.int32, (LANES, S), 1)
    spread = ((r < HEAD_DIM) & (c == r)) | ((r >= HEAD_DIM) & (c == r + HEAD_DIM))
    spread = spread.astype(BF16)
    rt = lax.broadcasted_iota(jnp.int32, (S, LANES), 0)
    ct = lax.broadcasted_iota(jnp.int32, (S, LANES), 1)
    spread_t = (((ct < HEAD_DIM) & (rt == ct)) | ((ct >= HEAD_DIM) & (rt == ct + HEAD_DIM))).astype(BF16)
    bi = lax.broadcasted_iota(jnp.int32, (LANES, LANES), 0) // HEAD_DIM
    bj = lax.broadcasted_iota(jnp.int32, (LANES, LANES), 1) // HEAD_DIM
    bd_mean = jnp.where(bi == bj, 1.0 / HEAD_DIM, 0.0).astype(BF16)
    row = lax.broadcasted_iota(jnp.int32, (S, 1), 0) % LANES
    q_ones = ((row >= HEAD_DIM) & (row < HEAD_DIM + 3)).astype(F32)
    v_ones = (row == HEAD_DIM).astype(F32)
    fr = lax.broadcasted_iota(jnp.int32, (3 * LANES, S), 0)
    fc = lax.broadcasted_iota(jnp.int32, (3 * LANES, S), 1)

    def head_norm(x, g):
        ms = _mm((x * x).astype(BF16), bd_mean)
        return (x * lax.rsqrt(ms + RMS_EPS) * g).astype(BF16)

    f3 = f3_ref[...]
    for p in range(N_PAIRS):
        sl = slice(p * LANES, (p + 1) * LANES)
        so = slice(p * S, (p + 1) * S)
        qn = head_norm(q_ref[:, sl].astype(F32), qg_ref[:, sl])
        kn = head_norm(k_ref[:, sl].astype(F32), kg_ref[:, sl])
        term = fr // LANES
        pick = ((fr % LANES == 2 * p) & (fc == HEAD_DIM + term)) | \
               ((fr % LANES == 2 * p + 1) & (fc == LANES + HEAD_DIM + term))
        ka_ref[:, so] = (_mm(kn, spread) + _mm(f3, pick.astype(BF16))).astype(BF16)
        qt_ref[so, :] = (_nt(spread_t, qn) + q_ones).astype(BF16)
        vt_ref[so, :] = (_nt(spread_t, v_ref[:, sl]) + v_ones).astype(BF16)


def _fox_prep(proj, f3, q_gain_row, k_gain_row):
    m = proj.shape[0]
    tm = min(512, m)
    wide = N_HEADS * LANES
    return pl.pallas_call(
        _fox_prep_body,
        grid=(m // tm,),
        in_specs=[
            pl.BlockSpec((tm, D_MODEL), lambda i: (i, 0)),
            pl.BlockSpec((tm, D_MODEL), lambda i: (i, 1)),
            pl.BlockSpec((tm, D_MODEL), lambda i: (i, 2)),
            pl.BlockSpec((tm, 3 * LANES), lambda i: (i, 0)),
            pl.BlockSpec((1, D_MODEL), lambda i: (0, 0)),
            pl.BlockSpec((1, D_MODEL), lambda i: (0, 0)),
        ],
        out_specs=[
            pl.BlockSpec((tm, wide), lambda i: (i, 0)),
            pl.BlockSpec((wide, tm), lambda i: (0, i)),
            pl.BlockSpec((wide, tm), lambda i: (0, i)),
        ],
        out_shape=[
            jax.ShapeDtypeStruct((m, wide), BF16),
            jax.ShapeDtypeStruct((wide, m), BF16),
            jax.ShapeDtypeStruct((wide, m), BF16),
        ],
        compiler_params=_params("parallel"),
        name="fox_prep",
    )(proj, proj, proj, f3, q_gain_row, k_gain_row)


def _fox_t_body(ka_ref, qt_ref, vt_ref, gate_ref, o_ref):
    T = qt_ref.shape[1]
    KV = min(FOXT_KV, ka_ref.shape[0])
    assert KV % T == 0
    i = pl.program_id(2)
    n_full = (i * T) // KV
    k_off = lax.broadcasted_iota(jnp.int32, (KV, T), 0)
    q_pos = i * T + lax.broadcasted_iota(jnp.int32, (KV, T), 1)
    VR = FOXT_V_ROWS

    def head_step(h, j, state, masked):
        m, acc = state
        ks = pl.ds(pl.multiple_of(j * KV, KV), KV)
        hs = slice(h * LANES, (h + 1) * LANES)
        s = _mm(ka_ref[ks, hs], qt_ref[hs, :])
        if masked:
            s = jnp.where(j * KV + k_off <= q_pos, s, -jnp.inf)
        yield
        m_new = jnp.maximum(m, jnp.max(s, axis=0, keepdims=True))
        pt = jnp.exp2(s - m_new).astype(BF16)
        pv = _mm(vt_ref[h * LANES:h * LANES + VR, ks], pt)
        yield
        return m_new, acc * jnp.exp2(m - m_new) + pv

    for h0 in range(0, FOXT_HEADS, FOXT_UNROLL):
        heads = list(range(h0, h0 + FOXT_UNROLL))

        def kv_step(j, states, masked):
            return tuple(_round_robin([head_step(h, j, st, masked) for h, st in zip(heads, states)]))

        init = (jnp.full((1, T), -jnp.inf, F32), jnp.zeros((VR, T), F32))
        states = lax.fori_loop(0, n_full, lambda j, st: kv_step(j, st, False), (init,) * len(heads))
        states = kv_step(n_full, states, True)
        outs = [acc[:HEAD_DIM] / acc[HEAD_DIM:HEAD_DIM + 1] for _, acc in states]
        for n in range(0, len(heads), 2):
            pair_t = jnp.concatenate([outs[n], outs[n + 1]], axis=0)
            sl = slice((h0 + n) * HEAD_DIM, (h0 + n + 2) * HEAD_DIM)
            o_ref[:, sl] = (pair_t.T * _silu(gate_ref[:, sl].astype(F32))).astype(o_ref.dtype)


def _fox_t_attention(ka, qt, vt, proj, bsz, seq):
    T = min(FOXT_BLOCK, seq)
    nq = seq // T
    m = bsz * seq
    gw = FOXT_HEADS * HEAD_DIM
    sw = FOXT_HEADS * LANES
    n_groups = N_HEADS // FOXT_HEADS
    gate_col0 = 3 * D_MODEL // gw
    return pl.pallas_call(
        _fox_t_body,
        grid=(bsz, n_groups, nq),
        in_specs=[
            pl.BlockSpec((seq, sw), lambda b, g, i: (b, g)),
            pl.BlockSpec((sw, T), lambda b, g, i: (g, b * nq + i)),
            pl.BlockSpec((sw, seq), lambda b, g, i: (g, b)),
            pl.BlockSpec((T, gw), lambda b, g, i: (b * nq + i, gate_col0 + g)),
        ],
        out_specs=pl.BlockSpec((T, gw), lambda b, g, i: (b * nq + i, g)),
        out_shape=jax.ShapeDtypeStruct((m, D_MODEL), BF16),
        compiler_params=_params("parallel", "parallel", "arbitrary"),
        name="fox_attention",
    )(ka, qt, vt, proj)


def _fox_layer(x2, bsz, seq, norm, w_in, b_f, q_gain, k_gain, w_out):
    W = D_MODEL
    proj = _norm_matmul(x2, norm, w_in[:, :4 * W].astype(BF16), name="fox_in_proj")
    f_cols, _ = _gate_proj(x2, norm, w_in[:, 4 * W:])
    f3 = _fox_decay(f_cols, b_f, bsz, seq)
    scale = HEAD_DIM ** -0.5 * LOG2_E
    q_gain_row = (jnp.tile(q_gain, N_HEADS) * scale).reshape(1, W).astype(F32)
    k_gain_row = jnp.tile(k_gain, N_HEADS).reshape(1, W).astype(F32)
    ka, qt, vt = _fox_prep(proj, f3, q_gain_row, k_gain_row)
    a = _fox_t_attention(ka, qt, vt, proj, bsz, seq)
    return _out_proj(a, w_out.astype(BF16), x2)


def kernel(x, rel_bias, swa_norm, swa_w_in, swa_q_gain, swa_k_gain, swa_sinks, swa_w_out, rwkv_norm, rwkv_mix, rwkv_w_in, rwkv_w0, rwkv_w_lora1, rwkv_w_lora2, rwkv_a0, rwkv_a_lora1, rwkv_a_lora2, rwkv_k_k, rwkv_k_a, rwkv_r_k, rwkv_gn_w, rwkv_gn_b, rwkv_w_out, mlstm_norm, mlstm_w_in, mlstm_b_i, mlstm_b_f, mlstm_h_gain, mlstm_w_out, fox_norm, fox_w_in, fox_b_f, fox_q_gain, fox_k_gain, fox_w_out):
    bsz, seq, d = x.shape
    depth = swa_norm.shape[0] + rwkv_norm.shape[0] + mlstm_norm.shape[0] + fox_norm.shape[0]
    x2 = x.reshape(bsz * seq, d)
    for layer in range(depth):
        kind, idx = layer % 4, layer // 4
        if kind == 0:
            x2 = _swa_layer(x2, bsz, seq, rel_bias, swa_norm[idx], swa_w_in[idx], swa_q_gain[idx],
                            swa_k_gain[idx], swa_sinks[idx], swa_w_out[idx])
        elif kind == 1:
            x2 = _rwkv_layer(x2, bsz, seq, rwkv_norm[idx], rwkv_mix[idx], rwkv_w_in[idx],
                             rwkv_w0[idx], rwkv_w_lora1[idx], rwkv_w_lora2[idx], rwkv_a0[idx],
                             rwkv_a_lora1[idx], rwkv_a_lora2[idx], rwkv_k_k[idx], rwkv_k_a[idx],
                             rwkv_r_k[idx], rwkv_gn_w[idx], rwkv_gn_b[idx], rwkv_w_out[idx])
        elif kind == 2:
            x2 = _mlstm_layer(x2, bsz, seq, mlstm_norm[idx], mlstm_w_in[idx], mlstm_b_i[idx],
                              mlstm_b_f[idx], mlstm_h_gain[idx], mlstm_w_out[idx])
        else:
            x2 = _fox_layer(x2, bsz, seq, fox_norm[idx], fox_w_in[idx], fox_b_f[idx],
                            fox_q_gain[idx], fox_k_gain[idx], fox_w_out[idx])
    return x2.reshape(bsz, seq, d)
```

```python
import functools
import math

import jax
import jax.numpy as jnp
from jax import lax
from jax.experimental import pallas as pl
from jax.experimental.pallas import tpu as pltpu

F32 = jnp.float32
BF16 = jnp.bfloat16
HIGHEST = lax.Precision.HIGHEST

D_MODEL = 2048
RMS_EPS = 1e-6
GN_EPS = 64e-5
LOG2_E = math.log2(math.e)

HEAD_DIM = 64
N_HEADS = D_MODEL // HEAD_DIM
LANES = 128
N_PAIRS = D_MODEL // LANES

SWA_KV_HEADS = 4
SWA_GROUP = N_HEADS // SWA_KV_HEADS
SWA_KV_WIDTH = SWA_KV_HEADS * HEAD_DIM
SWA_BLOCK = 128
SWA_PAIR_UNROLL = 8
REL_BUCKETS = 32
REL_MAX_DIST = 128

RWKV_CHUNK = 64
RWKV_TM = 512
RWKV_TN = 1024
RWKV_N_MIX = 6
RWKV_LORA_PAD = 128
RWKV_PAIR_UNROLL = 16

MLSTM_HEADS = 8
MLSTM_V_DIM = 256
MLSTM_QK_DIM = 128
MLSTM_CHUNK = 128
MLSTM_HEAD_UNROLL = 4
MLSTM_MAIN = 2 * MLSTM_HEADS * MLSTM_QK_DIM + 3 * MLSTM_HEADS * MLSTM_V_DIM

FOXT_BLOCK = 256
FOXT_HEADS = 8
FOXT_KV = 256
FOXT_UNROLL = 8
FOXT_V_ROWS = 80
FOX_BLOCK = 128
FOX_KV = 512
FOX_PAIR_UNROLL = 4

VMEM_LIMIT = 56 * 1024 * 1024
PROJ_TM = 1024
PROJ_TN_CHOICES = (1536, 1024, 512, 256, 128)


def _params(*sem):
    return pltpu.CompilerParams(dimension_semantics=sem, vmem_limit_bytes=VMEM_LIMIT)


def _nt(a, b, precision=None):
    return lax.dot_general(a, b, (((1,), (1,)), ((), ())), precision=precision,
                           preferred_element_type=F32)


def _tn(a, b, precision=None):
    return lax.dot_general(a, b, (((0,), (0,)), ((), ())), precision=precision,
                           preferred_element_type=F32)


def _mm(a, b, precision=None):
    return jnp.dot(a, b, precision=precision, preferred_element_type=F32)


def _rms_rows(x, g):
    return x * lax.rsqrt(jnp.mean(x * x, axis=-1, keepdims=True) + RMS_EPS) * g


def _silu(x):
    return x * jax.nn.sigmoid(x)


def _round_robin(gens):
    results = [None] * len(gens)
    live = list(range(len(gens)))
    while live:
        for n in list(live):
            try:
                next(gens[n])
            except StopIteration as done:
                results[n] = done.value
                live.remove(n)
    return results


def _norm_matmul_body(has_gates, x_ref, g_ref, w_ref, *refs):
    if has_gates:
        wc_ref, wr_ref, o_ref, cols_ref, rows_ref, xn_ref = refs
    else:
        o_ref, xn_ref = refs

    @pl.when(pl.program_id(1) == 0)
    def _():
        xn = _rms_rows(x_ref[...], g_ref[...]).astype(BF16)
        xn_ref[...] = xn
        if has_gates:
            cols_ref[...] = _mm(xn, wc_ref[...])
            rows_ref[...] = _nt(wr_ref[...], xn)

    o_ref[...] = _mm(xn_ref[...], w_ref[...]).astype(o_ref.dtype)


def _proj_tiles(m, n):
    tm = min(PROJ_TM, m)
    tn = next(t for t in PROJ_TN_CHOICES if n % t == 0)
    return tm, tn


def _norm_matmul(x, g, w, w_gates=None, *, name="norm_matmul"):
    m, k = x.shape
    n = w.shape[1]
    tm, tn = _proj_tiles(m, n)
    has_gates = w_gates is not None
    in_specs = [
        pl.BlockSpec((tm, k), lambda i, j: (i, 0)),
        pl.BlockSpec((1, k), lambda i, j: (0, 0)),
        pl.BlockSpec((k, tn), lambda i, j: (0, j)),
    ]
    out_specs = [pl.BlockSpec((tm, tn), lambda i, j: (i, j))]
    out_shape = [jax.ShapeDtypeStruct((m, n), BF16)]
    args = [x, g.reshape(1, k), w]
    if has_gates:
        n_g = w_gates.shape[1]
        in_specs += [pl.BlockSpec((k, LANES), lambda i, j: (0, 0)),
                     pl.BlockSpec((n_g, k), lambda i, j: (0, 0))]
        out_specs += [pl.BlockSpec((tm, LANES), lambda i, j: (i, 0)),
                      pl.BlockSpec((n_g, tm), lambda i, j: (0, i))]
        out_shape += [jax.ShapeDtypeStruct((m, LANES), F32), jax.ShapeDtypeStruct((n_g, m), F32)]
        args += [jnp.zeros((k, LANES), BF16).at[:, :n_g].set(w_gates.astype(BF16)),
                 w_gates.T.astype(BF16)]
    outs = pl.pallas_call(
        functools.partial(_norm_matmul_body, has_gates),
        grid=(m // tm, n // tn),
        in_specs=in_specs,
        out_specs=out_specs,
        out_shape=out_shape,
        scratch_shapes=[pltpu.VMEM((tm, k), BF16)],
        compiler_params=_params("parallel", "arbitrary"),
        name=name,
    )(*args)
    return outs if has_gates else outs[0]


def _out_proj_body(a_ref, w_ref, r_ref, o_ref):
    o_ref[...] = r_ref[...] + _mm(a_ref[...], w_ref[...])


def _out_proj(a, w, resid):
    m, k = a.shape
    n = w.shape[1]
    tm = min(512, m)
    return pl.pallas_call(
        _out_proj_body,
        grid=(m // tm,),
        in_specs=[
            pl.BlockSpec((tm, k), lambda i: (i, 0)),
            pl.BlockSpec((k, n), lambda i: (0, 0)),
            pl.BlockSpec((tm, n), lambda i: (i, 0)),
        ],
        out_specs=pl.BlockSpec((tm, n), lambda i: (i, 0)),
        out_shape=jax.ShapeDtypeStruct((m, n), F32),
        compiler_params=_params("parallel"),
        name="out_proj",
    )(a, w, resid)


def _t5_bucket(dist):
    max_exact = REL_BUCKETS // 2
    d_f = jnp.maximum(dist, 1).astype(F32)
    large = max_exact + (jnp.log(d_f / max_exact) / math.log(REL_MAX_DIST / max_exact)
                         * (REL_BUCKETS - max_exact)).astype(jnp.int32)
    large = jnp.minimum(large, REL_BUCKETS - 1)
    return jnp.where(dist < max_exact, dist, large)


def _swa_body(q_ref, gate_ref, kc_ref, kp_ref, vc_ref, vp_ref, bias_ref, sink_ref, qg_ref, kg_ref,
              o_ref):
    L = SWA_BLOCK
    k0 = lax.broadcasted_iota(jnp.int32, (2 * L, LANES), 1) < HEAD_DIM
    t0 = lax.broadcasted_iota(jnp.int32, (L, LANES), 1) < HEAD_DIM
    bi = lax.broadcasted_iota(jnp.int32, (LANES, LANES), 0) // HEAD_DIM
    bj = lax.broadcasted_iota(jnp.int32, (LANES, LANES), 1) // HEAD_DIM
    bd_ones = (bi == bj).astype(BF16)
    zero = jnp.zeros((2 * L, LANES), F32)
    one = jnp.ones((2 * L, LANES), F32)
    ones_bd = jnp.concatenate([jnp.where(k0, one, zero), jnp.where(k0, zero, one)], axis=0).astype(BF16)

    def head_mean_sq(z):
        return _mm((z * z).astype(BF16), bd_ones) * (1.0 / HEAD_DIM)

    def both_halves(x, head_in_low_lanes):
        if head_in_low_lanes:
            lo = jnp.where(k0, x, zero)
            hi = pltpu.roll(lo, HEAD_DIM, axis=1)
        else:
            hi = jnp.where(k0, zero, x)
            lo = pltpu.roll(hi, HEAD_DIM, axis=1)
        return jnp.concatenate([lo, hi], axis=0)

    k_all = jnp.concatenate([kp_ref[...], kc_ref[...]], axis=0).astype(F32)
    v_all = jnp.concatenate([vp_ref[...], vc_ref[...]], axis=0).astype(F32)
    kbd, rhs = [], []
    for t in range(SWA_KV_WIDTH // LANES):
        kt = k_all[:, t * LANES:(t + 1) * LANES]
        kn = kt * lax.rsqrt(head_mean_sq(kt) + RMS_EPS) * kg_ref[...]
        vt = v_all[:, t * LANES:(t + 1) * LANES]
        for low in (True, False):
            kbd.append(both_halves(kn, low).astype(BF16))
            rhs.append(jnp.concatenate([both_halves(vt, low).astype(BF16), ones_bd], axis=1))

    def pair(p):
        kh = p // (SWA_GROUP // 2)
        sl = slice(p * LANES, (p + 1) * LANES)
        q2 = q_ref[:, sl].astype(F32)
        ms = head_mean_sq(q2)
        yield
        qn = (q2 * lax.rsqrt(ms + RMS_EPS) * qg_ref[...]).astype(BF16)
        s = _nt(qn, kbd[kh]) + bias_ref[0, p]
        yield
        sink_a = sink_ref[2 * p]
        sink_b = sink_ref[2 * p + 1]
        m_a = jnp.maximum(jnp.max(s[:, :2 * L], axis=-1, keepdims=True), sink_a)
        m_b = jnp.maximum(jnp.max(s[:, 2 * L:], axis=-1, keepdims=True), sink_b)
        pm = jnp.concatenate([jnp.exp(s[:, :2 * L] - m_a), jnp.exp(s[:, 2 * L:] - m_b)], axis=1)
        pv = _mm(pm.astype(BF16), rhs[kh])
        yield
        sink_t = jnp.where(t0, jnp.exp(sink_a - m_a), jnp.exp(sink_b - m_b))
        o = pv[:, :LANES] / (pv[:, LANES:] + sink_t)
        return (o * _silu(gate_ref[:, sl].astype(F32))).astype(o_ref.dtype)

    for g0 in range(0, N_PAIRS, SWA_PAIR_UNROLL):
        outs = _round_robin([pair(p) for p in range(g0, g0 + SWA_PAIR_UNROLL)])
        for n, o in enumerate(outs):
            o_ref[:, (g0 + n) * LANES:(g0 + n + 1) * LANES] = o


def _swa_attention(proj, bias, sinks, q_gain2, k_gain2, bsz, seq):
    L = SWA_BLOCK
    nb = seq // L
    m = bsz * seq
    kcol = 2 * D_MODEL // SWA_KV_WIDTH
    prev = lambda r: jnp.where(r % nb == 0, r, r - 1)
    return pl.pallas_call(
        _swa_body,
        grid=(bsz * nb,),
        in_specs=[
            pl.BlockSpec((L, D_MODEL), lambda r: (r, 0)),
            pl.BlockSpec((L, D_MODEL), lambda r: (r, 1)),
            pl.BlockSpec((L, SWA_KV_WIDTH), lambda r: (r, kcol)),
            pl.BlockSpec((L, SWA_KV_WIDTH), lambda r: (prev(r), kcol)),
            pl.BlockSpec((L, SWA_KV_WIDTH), lambda r: (r, kcol + 1)),
            pl.BlockSpec((L, SWA_KV_WIDTH), lambda r: (prev(r), kcol + 1)),
            pl.BlockSpec((1, N_PAIRS, L, 4 * L), lambda r: (jnp.where(r % nb == 0, 0, 1), 0, 0, 0)),
            pl.BlockSpec(memory_space=pltpu.SMEM),
            pl.BlockSpec((1, LANES), lambda r: (0, 0)),
            pl.BlockSpec((1, LANES), lambda r: (0, 0)),
        ],
        out_specs=pl.BlockSpec((L, D_MODEL), lambda r: (r, 0)),
        out_shape=jax.ShapeDtypeStruct((m, D_MODEL), BF16),
        compiler_params=_params("parallel"),
        name="swa_attention",
    )(proj, proj, proj, proj, proj, proj, bias, sinks, q_gain2, k_gain2)


def _swa_bias_tables(rel_bias):
    L = SWA_BLOCK
    qi = jnp.arange(L)[:, None]
    kj = jnp.arange(2 * L)[None, :]
    dist = qi + L - kj
    in_band = (dist >= 0) & (dist < L)
    bias = rel_bias.astype(F32)[_t5_bucket(jnp.maximum(dist, 0))].transpose(2, 0, 1)
    general = jnp.where(in_band[None], bias, -jnp.inf)
    first = jnp.where((in_band & (kj >= L))[None], bias, -jnp.inf)
    pairs = lambda t: t.reshape(N_PAIRS, 2, L, 2 * L).transpose(0, 2, 1, 3).reshape(N_PAIRS, L, 4 * L)
    return jnp.stack([pairs(first), pairs(general)])


def _swa_layer(x2, bsz, seq, rel_bias, norm, w_in, q_gain, k_gain, sinks, w_out):
    W = D_MODEL
    kvw = SWA_KV_WIDTH
    w_perm = jnp.concatenate([w_in[:, :W], w_in[:, W + 2 * kvw:], w_in[:, W:W + 2 * kvw]], axis=1)
    proj = _norm_matmul(x2, norm, w_perm.astype(BF16), name="swa_in_proj")
    q_gain2 = (jnp.tile(q_gain, 2) * HEAD_DIM ** -0.5).reshape(1, LANES).astype(F32)
    k_gain2 = jnp.tile(k_gain, 2).reshape(1, LANES).astype(F32)
    a = _swa_attention(proj, _swa_bias_tables(rel_bias), sinks.astype(F32), q_gain2, k_gain2, bsz, seq)
    return _out_proj(a, w_out.astype(BF16), x2)


def _shifted_mix(x_ref, xp_ref, g_ref, seq_start):
    h = _rms_rows(x_ref[...], g_ref[...])
    h_prev_row = _rms_rows(xp_ref[7:8, :], g_ref[...])
    h_prev_row = jnp.where(seq_start, 0.0, h_prev_row)
    rolled = pltpu.roll(h, 1, axis=0)
    row = lax.broadcasted_iota(jnp.int32, h.shape, 0)
    shifted = jnp.where(row == 0, h_prev_row, rolled)
    return h, shifted - h


def _rwkv_in_body(tiles_per_seq, x_ref, xp_ref, g_ref, mix_ref, w_ref, w1_ref, w2_ref, a1_ref, a2_ref,
                  w0_ref, a0_ref, o_ref, lw_ref, a_ref, xm_ref, t_ref):
    i = pl.program_id(0)
    c = pl.program_id(1)
    j = pl.program_id(2)
    tn = o_ref.shape[2]
    cols = pl.ds(pl.multiple_of(j * tn, tn), tn)

    @pl.when((c == 0) & (j == 0))
    def _():
        h, xx = _shifted_mix(x_ref, xp_ref, g_ref, i % tiles_per_seq == 0)
        for n in range(RWKV_N_MIX):
            xm_ref[n] = (h + xx * mix_ref[n:n + 1, :]).astype(BF16)

    @pl.when(c < 4)
    def _():
        o_ref[0] = _mm(xm_ref[c], w_ref[0]).astype(o_ref.dtype)

    @pl.when(c == 4)
    def _():
        @pl.when(j == 0)
        def _():
            t_ref[...] = jnp.tanh(_mm(xm_ref[4], w1_ref[...])).astype(BF16)

        w_pre = w0_ref[:, cols] + _mm(t_ref[...], w2_ref[:, cols])
        lw_ref[...] = -jnp.exp(-jax.nn.softplus(-w_pre) - 0.5)

    @pl.when(c == 5)
    def _():
        @pl.when(j == 0)
        def _():
            t_ref[...] = _mm(xm_ref[5], a1_ref[...]).astype(BF16)

        a_ref[...] = jax.nn.sigmoid(a0_ref[:, cols] + _mm(t_ref[...], a2_ref[:, cols]))


def _rwkv_in_proj(x2, seq, norm, mix8, w4, w1, w2, a1, a2, w0, a0):
    m, k = x2.shape
    tm = min(RWKV_TM, seq)
    tn = RWKV_TN
    nj = k // tn
    rows8 = tm // 8
    P = RWKV_LORA_PAD
    full = lambda shape: pl.BlockSpec(shape, lambda i, c, j: tuple(0 for _ in shape))
    w_idx = lambda i, c, j: (jnp.minimum(c, 3), 0, jnp.where(c < 4, j, nj - 1))
    o_idx = lambda i, c, j: (jnp.minimum(c, 3), i, jnp.where(c < 4, j, nj - 1))
    lw_idx = lambda i, c, j: (i, jnp.where(c == 4, j, jnp.where(c > 4, nj - 1, 0)))
    a_idx = lambda i, c, j: (i, jnp.where(c == 5, j, 0))
    return pl.pallas_call(
        functools.partial(_rwkv_in_body, seq // tm),
        grid=(m // tm, RWKV_N_MIX, nj),
        in_specs=[
            pl.BlockSpec((tm, k), lambda i, c, j: (i, 0)),
            pl.BlockSpec((8, k), lambda i, c, j: (jnp.maximum(i * rows8 - 1, 0), 0)),
            full((1, k)), full((8, k)),
            pl.BlockSpec((1, k, tn), w_idx),
            full((k, P)), full((P, k)), full((k, P)), full((P, k)),
            full((1, k)), full((1, k)),
        ],
        out_specs=[
            pl.BlockSpec((1, tm, tn), o_idx),
            pl.BlockSpec((tm, tn), lw_idx),
            pl.BlockSpec((tm, tn), a_idx),
        ],
        out_shape=[
            jax.ShapeDtypeStruct((4, m, k), BF16),
            jax.ShapeDtypeStruct((m, k), F32),
            jax.ShapeDtypeStruct((m, k), F32),
        ],
        scratch_shapes=[pltpu.VMEM((RWKV_N_MIX, tm, k), BF16), pltpu.VMEM((tm, P), BF16)],
        compiler_params=_params("parallel", "arbitrary", "arbitrary"),
        name="rwkv_in_proj",
    )(x2, x2, norm.reshape(1, k), mix8, w4, w1, w2, a1, a2, w0.reshape(1, k), a0.reshape(1, k))


def _stack2(z, m0):
    zero = jnp.zeros_like(z)
    return jnp.concatenate([jnp.where(m0, z, zero), jnp.where(m0, zero, z)], axis=0)


def _rwkv_scan_body(r_ref, k_ref, v_ref, g_ref, lw_ref, a_ref, kk_ref, ka_ref, rk_ref,
                    gnw_ref, gnb_ref, o_ref, ht_ref):
    C = RWKV_CHUNK

    @pl.when(pl.program_id(1) == 0)
    def _():
        ht_ref[...] = jnp.zeros_like(ht_ref)

    lane = lax.broadcasted_iota(jnp.int32, (C, LANES), 1)
    tok = lax.broadcasted_iota(jnp.int32, (C, LANES), 0)
    m0 = lane < HEAD_DIM
    col = lane % HEAD_DIM
    strict = tok > col
    incl = tok >= col
    ti = lax.broadcasted_iota(jnp.int32, (C, C), 0)
    tj = lax.broadcasted_iota(jnp.int32, (C, C), 1)
    tri = (ti >= tj).astype(F32)
    bi = lax.broadcasted_iota(jnp.int32, (LANES, LANES), 0) // HEAD_DIM
    bj = lax.broadcasted_iota(jnp.int32, (LANES, LANES), 1) // HEAD_DIM
    bd = bi == bj
    bd_ones = bd.astype(BF16)
    tri2 = jnp.concatenate([tri, tri], axis=1).astype(BF16)

    def head_sum(z):
        return _mm(z.astype(BF16), bd_ones)

    def load(p):
        sl = pl.ds(pl.multiple_of(p * LANES, LANES), LANES)
        f32 = lambda ref: ref[0, :, sl].astype(F32)
        return (f32(r_ref), f32(k_ref), f32(v_ref), f32(g_ref), lw_ref[:, sl],
                a_ref[:, sl], kk_ref[:, sl], ka_ref[:, sl], rk_ref[:, sl], gnw_ref[:, sl],
                gnb_ref[:, sl], ht_ref[p])

    def compute(r, k, v, g, lw, a, k_k, k_a, r_k, gn_w, gn_b, ht):
        kkp = k * k_k
        kp = k * (1.0 + (a - 1.0) * k_a)
        sums = head_sum(jnp.concatenate([kkp * kkp, r * kp * r_k], axis=0))
        ss, bonus_dot = sums[:C], sums[C:]
        lw_hi = lw.astype(BF16)
        lw_lo = (lw - lw_hi.astype(F32)).astype(BF16)
        cum = _mm(tri2, jnp.concatenate([lw_hi, lw_lo], axis=0))
        yield
        kk = kkp / jnp.maximum(jnp.sqrt(ss), 1e-12)
        cum_end = cum[C - 1:C, :]
        g_in = jnp.exp(cum)
        g_prev = jnp.exp(cum - lw)
        g_inv = jnp.exp(-cum)
        g_end = jnp.exp(cum_end - cum)
        kka = kk * a
        a_t = -kk * g_prev
        r_t = r * g_in
        b_t = kka * g_inv
        k_t = kp * g_inv

        x1 = jnp.concatenate([a_t, r_t], axis=0).astype(BF16)
        bk2 = jnp.concatenate([_stack2(b_t, m0), _stack2(k_t, m0)], axis=0).astype(BF16)
        gbk = _nt(x1, bk2)
        xh = _nt(x1, ht.astype(BF16))
        yield
        zero = jnp.zeros((C, LANES), F32)
        l_ab = jnp.where(strict, gbk[:C, :LANES], zero)
        l_ak = jnp.where(strict, gbk[:C, LANES:], zero)
        p_rb = jnp.where(incl, gbk[C:, :LANES], zero)
        p_rk = jnp.where(incl, gbk[C:, LANES:], zero)

        t_inv = jnp.where(tok == col, 1.0, zero) + jnp.where(
            (tok // 2 == col // 2) & (tok % 2 == 1) & (col % 2 == 0), l_ab, zero)
        v2 = _stack2(v, m0).astype(BF16)
        rhs = xh[:C] + _mm(l_ak.astype(BF16), v2)
        s = 2
        while s < C:
            sel = (tok // (2 * s) == col // (2 * s)) & (tok % (2 * s) >= s) & (col % (2 * s) < s)
            l_s = jnp.where(sel, l_ab, zero)
            tl = _mm(t_inv.astype(BF16), _stack2(l_s, m0).astype(BF16))
            yield
            t_inv = t_inv + _mm(tl.astype(BF16), _stack2(t_inv, m0).astype(BF16))
            yield
            s *= 2

        u = _mm(t_inv.astype(BF16), _stack2(rhs, m0).astype(BF16))
        yield
        u2 = _stack2(u, m0).astype(BF16)
        p_cat = jnp.concatenate([p_rb, p_rk], axis=1).astype(BF16)
        y = xh[C:] + _mm(p_cat, jnp.concatenate([u2, v2], axis=0))
        uv = jnp.concatenate([u, v], axis=0).astype(BF16)
        bk_end = jnp.concatenate([kka * g_end, kp * g_end], axis=0).astype(BF16)
        upd = _tn(uv, bk_end)
        yield
        ht_new = ht * jnp.exp(cum_end) + jnp.where(bd, upd, 0.0)
        mu = head_sum(y) * (1.0 / HEAD_DIM)
        yield
        dy = y - mu
        var = head_sum(dy * dy) * (1.0 / HEAD_DIM)
        yield
        yn = dy * lax.rsqrt(var + GN_EPS) * gn_w + gn_b
        return ((yn + bonus_dot * v) * _silu(g)).astype(o_ref.dtype), ht_new

    def group(gi, carry):
        base = gi * RWKV_PAIR_UNROLL
        ins = [load(base + n) for n in range(RWKV_PAIR_UNROLL)]
        outs = _round_robin([compute(*x) for x in ins])
        for n, (o, ht_new) in enumerate(outs):
            p = base + n
            o_ref[:, pl.ds(pl.multiple_of(p * LANES, LANES), LANES)] = o
            ht_ref[p] = ht_new
        return carry

    lax.fori_loop(0, N_PAIRS // RWKV_PAIR_UNROLL, group, 0)


def _rwkv_scan(rkvg, lw, a, k_k, k_a, r_k, gn_w, gn_b, bsz, seq):
    C = RWKV_CHUNK
    nc = seq // C
    m = bsz * seq
    k = D_MODEL
    part = lambda c: pl.BlockSpec((1, C, k), lambda b, j, c=c: (c, b * nc + j, 0))
    tile = pl.BlockSpec((C, k), lambda b, j: (b * nc + j, 0))
    vec = pl.BlockSpec((1, k), lambda b, j: (0, 0))
    return pl.pallas_call(
        _rwkv_scan_body,
        grid=(bsz, nc),
        in_specs=[part(0), part(1), part(2), part(3), tile, tile, vec, vec, vec, vec, vec],
        out_specs=tile,
        out_shape=jax.ShapeDtypeStruct((m, k), BF16),
        scratch_shapes=[pltpu.VMEM((N_PAIRS, LANES, LANES), F32)],
        compiler_params=_params("parallel", "arbitrary"),
        name="rwkv_scan",
    )(rkvg, rkvg, rkvg, rkvg, lw, a, k_k.reshape(1, k), k_a.reshape(1, k), r_k.reshape(1, k),
      gn_w.reshape(1, k), gn_b.reshape(1, k))


def _rwkv_layer(x2, bsz, seq, norm, mix, w_in, w0, w_lora1, w_lora2, a0, a_lora1, a_lora2,
                k_k, k_a, r_k, gn_w, gn_b, w_out):
    k = D_MODEL
    P = RWKV_LORA_PAD
    mix8 = jnp.zeros((8, k), F32).at[:mix.shape[0]].set(mix)
    pad_cols = lambda w: jnp.zeros((k, P), BF16).at[:, :w.shape[1]].set(w.astype(BF16))
    pad_rows = lambda w: jnp.zeros((P, k), BF16).at[:w.shape[0]].set(w.astype(BF16))
    rkvg, lw, a = _rwkv_in_proj(x2, seq, norm, mix8, w_in.astype(BF16), pad_cols(w_lora1),
                                pad_rows(w_lora2), pad_cols(a_lora1), pad_rows(a_lora2), w0, a0)
    y = _rwkv_scan(rkvg, lw, a, k_k, k_a, r_k, gn_w, gn_b, bsz, seq)
    return _out_proj(y, w_out.astype(BF16), x2)


def _mlstm_body(q_ref, k_ref, v_ref, op_ref, gate_ref, gc_ref, gr_ref, bi_ref, bf_ref, hg_ref,
                o_ref, ct_ref, m_ref):
    L = MLSTM_CHUNK
    H = MLSTM_HEADS
    dk = MLSTM_QK_DIM
    dv = MLSTM_V_DIM

    @pl.when(pl.program_id(1) == 0)
    def _():
        ct_ref[...] = jnp.zeros_like(ct_ref)
        m_ref[...] = jnp.zeros_like(m_ref)

    ti = lax.broadcasted_iota(jnp.int32, (L, L), 0)
    tj = lax.broadcasted_iota(jnp.int32, (L, L), 1)
    tril = ti >= tj
    tri_f = tril.astype(F32)
    gc = gc_ref[...] + bi_ref[...]
    gr = gr_ref[...] + bf_ref[...]
    lane = lax.broadcasted_iota(jnp.int32, gc.shape, 1)
    f_cols = jnp.where((lane >= H) & (lane < 2 * H), jax.nn.log_sigmoid(gc), 0.0)
    b_cols = _mm(tri_f, f_cols, HIGHEST)
    f_rows = jax.nn.log_sigmoid(gr)
    b_rows = _nt(f_rows, tri_f, HIGHEST)
    ones_blk = jnp.ones((L, LANES), BF16)

    def head(h):
        q = q_ref[:, h * dk:(h + 1) * dk]
        k = k_ref[:, h * dk:(h + 1) * dk]
        v = v_ref[:, h * dv:(h + 1) * dv]
        v_aug = jnp.concatenate([v, ones_blk], axis=-1)
        ct = ct_ref[h]
        m_prev = m_ref[h][:, :1]
        qk = _nt(q, k)
        q_ct = _mm(q, ct.astype(BF16))
        yield
        i_col = gc[:, h:h + 1]
        i_row = gr[h:h + 1, :]
        b_col = b_cols[:, H + h:H + h + 1]
        b_row = b_rows[H + h:H + h + 1, :]
        g_tot = b_col[L - 1:L, :]
        dmat = jnp.where(tril, b_col - b_row + i_row, -jnp.inf)
        inter = b_col + m_prev
        m_t = jnp.maximum(inter, jnp.max(dmat, axis=-1, keepdims=True))
        w_intra = jnp.exp(dmat - m_t) * dk ** -0.5
        w_inter = jnp.exp(inter - m_t) * dk ** -0.5
        sw = qk * w_intra
        num = w_inter * q_ct + _mm(sw.astype(BF16), v_aug)
        decay = g_tot - b_col + i_col
        m_new = jnp.maximum(g_tot + m_prev, jnp.max(decay, axis=0, keepdims=True))
        ws = jnp.exp(decay - m_new)
        carry_scale = jnp.exp(g_tot + m_prev - m_new)
        wv = (ws * v_aug.astype(F32)).astype(BF16)
        upd = _tn(k, wv)
        yield
        den = num[:, dv:dv + 1]
        h_t = num[:, :dv] / jnp.maximum(jnp.abs(den), jnp.exp(-m_t))
        hn = _rms_rows(h_t, hg_ref[:, h * dv:(h + 1) * dv])
        osl = slice(h * dv, (h + 1) * dv)
        out = hn * jax.nn.sigmoid(op_ref[:, osl].astype(F32)) * _silu(gate_ref[:, osl].astype(F32))
        return out.astype(o_ref.dtype), carry_scale * ct + upd, jnp.broadcast_to(m_new, (1, LANES))

    for h0 in range(0, H, MLSTM_HEAD_UNROLL):
        heads = range(h0, h0 + MLSTM_HEAD_UNROLL)
        for h, (out, ct_new, m_new) in zip(heads, _round_robin([head(h) for h in heads])):
            o_ref[:, h * dv:(h + 1) * dv] = out
            ct_ref[h] = ct_new
            m_ref[h] = m_new


def _mlstm_scan(proj, g_cols, g_rows, b_i, b_f, h_gain, bsz, seq):
    L = MLSTM_CHUNK
    H = MLSTM_HEADS
    nc = seq // L
    m = bsz * seq
    qkw = H * MLSTM_QK_DIM
    vw = H * MLSTM_V_DIM
    bias = jnp.concatenate([b_i, b_f]).astype(F32)
    bias_row = jnp.zeros((1, LANES), F32).at[0, :2 * H].set(bias)
    bias_col = bias.reshape(2 * H, 1)
    row = lambda b, j: b * nc + j
    return pl.pallas_call(
        _mlstm_body,
        grid=(bsz, nc),
        in_specs=[
            pl.BlockSpec((L, qkw), lambda b, j: (row(b, j), 0)),
            pl.BlockSpec((L, qkw), lambda b, j: (row(b, j), 1)),
            pl.BlockSpec((L, vw), lambda b, j: (row(b, j), 1)),
            pl.BlockSpec((L, vw), lambda b, j: (row(b, j), 2)),
            pl.BlockSpec((L, vw), lambda b, j: (row(b, j), 3)),
            pl.BlockSpec((L, LANES), lambda b, j: (row(b, j), 0)),
            pl.BlockSpec((2 * H, L), lambda b, j: (0, row(b, j))),
            pl.BlockSpec((1, LANES), lambda b, j: (0, 0)),
            pl.BlockSpec((2 * H, 1), lambda b, j: (0, 0)),
            pl.BlockSpec((1, vw), lambda b, j: (0, 0)),
        ],
        out_specs=pl.BlockSpec((L, vw), lambda b, j: (row(b, j), 0)),
        out_shape=jax.ShapeDtypeStruct((m, vw), BF16),
        scratch_shapes=[
            pltpu.VMEM((H, MLSTM_QK_DIM, MLSTM_V_DIM + LANES), F32),
            pltpu.VMEM((H, 1, LANES), F32),
        ],
        compiler_params=_params("parallel", "arbitrary"),
        name="mlstm_scan",
    )(proj, proj, proj, proj, proj, g_cols, g_rows, bias_row, bias_col, h_gain.reshape(1, vw))


def _mlstm_layer(x2, bsz, seq, norm, w_in, b_i, b_f, h_gain, w_out):
    proj, g_cols, g_rows = _norm_matmul(x2, norm, w_in[:, :MLSTM_MAIN].astype(BF16),
                                        w_in[:, MLSTM_MAIN:], name="mlstm_in_proj")
    a = _mlstm_scan(proj, g_cols, g_rows, b_i, b_f, h_gain, bsz, seq)
    return _out_proj(a, w_out.astype(BF16), x2)


def _head_norm_body(x_ref, g_ref, o_ref):
    x = x_ref[...].astype(F32)
    bi = lax.broadcasted_iota(jnp.int32, (LANES, LANES), 0) // HEAD_DIM
    bj = lax.broadcasted_iota(jnp.int32, (LANES, LANES), 1) // HEAD_DIM
    bd_mean = jnp.where(bi == bj, 1.0 / HEAD_DIM, 0.0).astype(BF16)
    for c in range(x.shape[1] // LANES):
        xc = x[:, c * LANES:(c + 1) * LANES]
        ms = _mm((xc * xc).astype(BF16), bd_mean)
        o_ref[:, c * LANES:(c + 1) * LANES] = (
            xc * lax.rsqrt(ms + RMS_EPS) * g_ref[:, c * LANES:(c + 1) * LANES]).astype(o_ref.dtype)


def _head_norm(proj, col_block, gain_row):
    m = proj.shape[0]
    tm = min(512, m)
    return pl.pallas_call(
        _head_norm_body,
        grid=(m // tm,),
        in_specs=[
            pl.BlockSpec((tm, D_MODEL), lambda i: (i, col_block)),
            pl.BlockSpec((1, D_MODEL), lambda i: (0, 0)),
        ],
        out_specs=pl.BlockSpec((tm, D_MODEL), lambda i: (i, 0)),
        out_shape=jax.ShapeDtypeStruct((m, D_MODEL), BF16),
        compiler_params=_params("parallel"),
        name="head_norm",
    )(proj, gain_row)


def _fox_forget_body(f_ref, bf_ref, o_ref):
    B = LANES
    seq = f_ref.shape[1]
    lf = jax.nn.log_sigmoid(f_ref[...] + bf_ref[...])
    ti = lax.broadcasted_iota(jnp.int32, (B, B), 0)
    tj = lax.broadcasted_iota(jnp.int32, (B, B), 1)
    triu = (ti <= tj).astype(F32)
    carry = jnp.zeros((lf.shape[0], 1), F32)
    for c in range(seq // B):
        blk = _mm(lf[:, c * B:(c + 1) * B], triu, HIGHEST) + carry
        o_ref[:, c * B:(c + 1) * B] = blk * LOG2_E
        carry = blk[:, B - 1:B]


def _fox_forget(f_rows, b_f, bsz, seq):
    H = f_rows.shape[0]
    return pl.pallas_call(
        _fox_forget_body,
        grid=(bsz,),
        in_specs=[pl.BlockSpec((H, seq), lambda b: (0, b)), pl.BlockSpec((H, 1), lambda b: (0, 0))],
        out_specs=pl.BlockSpec((H, seq), lambda b: (0, b)),
        out_shape=jax.ShapeDtypeStruct((H, bsz * seq), F32),
        compiler_params=_params("parallel"),
        name="fox_forget",
    )(f_rows, b_f.reshape(H, 1).astype(F32))


def _fox_body(q_ref, k_ref, v_ref, f_ref, gate_ref, o_ref):
    T = q_ref.shape[0]
    KV = min(FOX_KV, k_ref.shape[0])
    G = FOX_PAIR_UNROLL
    i = pl.program_id(1)
    n_full = (i * T) // KV
    assert KV % T == 0
    SUB = LANES
    k0 = lax.broadcasted_iota(jnp.int32, (SUB, LANES), 1).astype(F32).astype(BF16) < HEAD_DIM
    t0 = lax.broadcasted_iota(jnp.int32, (T, LANES), 1) < HEAD_DIM
    one = jnp.ones((SUB, LANES), BF16)
    zero = jnp.zeros((SUB, LANES), BF16)
    ones_sub = jnp.concatenate([jnp.where(k0, one, zero), jnp.where(k0, zero, one)], axis=0)
    ones_bd = jnp.concatenate([ones_sub] * (KV // SUB), axis=0)
    q_pos = i * T + lax.broadcasted_iota(jnp.int32, (T, 2 * SUB), 0)
    k_off = lax.broadcasted_iota(jnp.int32, (T, 2 * SUB), 1) % SUB

    def split2(x):
        return jnp.concatenate([jnp.where(k0, x, zero), jnp.where(k0, zero, x)], axis=0)

    def pair_step(p, q2, j, state, masked):
        m_a, m_b, acc = state
        sl = pl.ds(pl.multiple_of(p * LANES, LANES), LANES)
        parts, v_rows = [], []
        mx = None
        for c in range(KV // SUB):
            ks = pl.ds(pl.multiple_of(j * KV + c * SUB, SUB), SUB)
            f2 = jnp.concatenate([f_ref[p, 0:1, ks], f_ref[p, 1:2, ks]], axis=1)
            s = _nt(q2, split2(k_ref[ks, sl])) - f2
            if masked:
                s = jnp.where(j * KV + c * SUB + k_off <= q_pos, s, -jnp.inf)
            parts.append(s)
            mx = s if mx is None else jnp.maximum(mx, s)
            v_rows.append(split2(v_ref[ks, sl]))
        yield
        new_a = jnp.maximum(m_a, jnp.max(mx[:, :SUB], axis=-1, keepdims=True))
        new_b = jnp.maximum(m_b, jnp.max(mx[:, SUB:], axis=-1, keepdims=True))
        pm = jnp.concatenate(
            [jnp.concatenate([jnp.exp2(s[:, :SUB] - new_a), jnp.exp2(s[:, SUB:] - new_b)],
                             axis=1).astype(BF16) for s in parts], axis=1)
        rhs = jnp.concatenate([jnp.concatenate(v_rows, axis=0), ones_bd], axis=1)
        pv = _mm(pm, rhs)
        yield
        alpha = jnp.where(t0, jnp.exp2(m_a - new_a), jnp.exp2(m_b - new_b))
        acc = acc * jnp.concatenate([alpha, alpha], axis=1) + pv
        return new_a, new_b, acc

    def group(gi, carry):
        pairs = [gi * G + n for n in range(G)]
        qs = [q_ref[:, pl.ds(pl.multiple_of(p * LANES, LANES), LANES)] for p in pairs]

        def kv_step(j, states, masked):
            return tuple(_round_robin(
                [pair_step(p, q2, j, st, masked) for p, q2, st in zip(pairs, qs, states)]))

        init = (jnp.full((T, 1), -jnp.inf, F32), jnp.full((T, 1), -jnp.inf, F32),
                jnp.zeros((T, 2 * LANES), F32))
        states = lax.fori_loop(0, n_full, lambda j, st: kv_step(j, st, False), (init,) * G)
        states = kv_step(n_full, states, True)
        for p, (_, _, acc) in zip(pairs, states):
            sl = pl.ds(pl.multiple_of(p * LANES, LANES), LANES)
            o = acc[:, :LANES] / acc[:, LANES:]
            o_ref[:, sl] = (o * _silu(gate_ref[:, sl].astype(F32))).astype(o_ref.dtype)
        return carry

    lax.fori_loop(0, N_PAIRS // G, group, 0)


def _fox_attention(qn, kn, f_cum, proj, bsz, seq):
    T = min(FOX_BLOCK, seq)
    nq = seq // T
    m = bsz * seq
    return pl.pallas_call(
        _fox_body,
        grid=(bsz, nq),
        in_specs=[
            pl.BlockSpec((T, D_MODEL), lambda b, i: (b * nq + i, 0)),
            pl.BlockSpec((seq, D_MODEL), lambda b, i: (b, 0)),
            pl.BlockSpec((seq, D_MODEL), lambda b, i: (b, 2)),
            pl.BlockSpec((N_PAIRS, 8, seq), lambda b, i: (0, 0, b)),
            pl.BlockSpec((T, D_MODEL), lambda b, i: (b * nq + i, 3)),
        ],
        out_specs=pl.BlockSpec((T, D_MODEL), lambda b, i: (b * nq + i, 0)),
        out_shape=jax.ShapeDtypeStruct((m, D_MODEL), BF16),
        compiler_params=_params("parallel", "arbitrary"),
        name="fox_attention",
    )(qn, kn, proj, f_cum, proj)


def _fox_decay_body(f_ref, bf_ref, o_ref):
    B = LANES
    seq = f_ref.shape[0]
    lf = jax.nn.log_sigmoid(f_ref[...] + bf_ref[...])
    ti = lax.broadcasted_iota(jnp.int32, (B, B), 0)
    tj = lax.broadcasted_iota(jnp.int32, (B, B), 1)
    tril = (ti >= tj).astype(F32)
    carry = jnp.zeros((1, LANES), F32)
    for c in range(seq // B):
        blk = _mm(tril, lf[c * B:(c + 1) * B, :], HIGHEST) + carry
        carry = blk[B - 1:B, :]
        neg = blk * (-LOG2_E)
        hi = neg.astype(BF16)
        r1 = neg - hi.astype(F32)
        mid = r1.astype(BF16)
        lo = (r1 - mid.astype(F32)).astype(BF16)
        rows = slice(c * B, (c + 1) * B)
        o_ref[rows, 0:LANES] = hi
        o_ref[rows, LANES:2 * LANES] = mid
        o_ref[rows, 2 * LANES:3 * LANES] = lo


def _fox_decay(f_cols, b_f, bsz, seq):
    bias = jnp.zeros((1, LANES), F32).at[0, :N_HEADS].set(b_f.astype(F32))
    return pl.pallas_call(
        _fox_decay_body,
        grid=(bsz,),
        in_specs=[pl.BlockSpec((seq, LANES), lambda b: (b, 0)), pl.BlockSpec((1, LANES), lambda b: (0, 0))],
        out_specs=pl.BlockSpec((seq, 3 * LANES), lambda b: (b, 0)),
        out_shape=jax.ShapeDtypeStruct((bsz * seq, 3 * LANES), BF16),
        compiler_params=_params("parallel"),
        name="fox_decay",
    )(f_cols, bias)


def _fox_prep_body(q_ref, k_ref, v_ref, f3_ref, qg_ref, kg_ref, ka_ref, qt_ref, vt_ref):
    S = 2 * LANES
    r = lax.broadcasted_iota(jnp.int32, (LANES, S), 0)
    c = lax.broadcasted_iota(jnp.int32, (LANES, S), 1)
    spread = ((r < HEAD_DIM) & (c == r)) | ((r >= HEAD_DIM) & (c == r + HEAD_DIM))
    spread = spread.astype(BF16)
    rt = lax.broadcasted_iota(jnp.int32, (S, LANES), 0)
    ct = lax.broadcasted_iota(jnp.int32, (S, LANES), 1)
    spread_t = (((ct < HEAD_DIM) & (rt == ct)) | ((ct >= HEAD_DIM) & (rt == ct + HEAD_DIM))).astype(BF16)
    bi = lax.broadcasted_iota(jnp.int32, (LANES, LANES), 0) // HEAD_DIM
    bj = lax.broadcasted_iota(jnp.int32, (LANES, LANES), 1) // HEAD_DIM
    bd_mean = jnp.where(bi == bj, 1.0 / HEAD_DIM, 0.0).astype(BF16)
    row = lax.broadcasted_iota(jnp.int32, (S, 1), 0) % LANES
    q_ones = ((row >= HEAD_DIM) & (row < HEAD_DIM + 3)).astype(F32)
    v_ones = (row == HEAD_DIM).astype(F32)
    fr = lax.broadcasted_iota(jnp.int32, (3 * LANES, S), 0)
    fc = lax.broadcasted_iota(jnp.int32, (3 * LANES, S), 1)

    def head_norm(x, g):
        ms = _mm((x * x).astype(BF16), bd_mean)
        return (x * lax.rsqrt(ms + RMS_EPS) * g).astype(BF16)

    f3 = f3_ref[...]
    for p in range(N_PAIRS):
        sl = slice(p * LANES, (p + 1) * LANES)
        so = slice(p * S, (p + 1) * S)
        qn = head_norm(q_ref[:, sl].astype(F32), qg_ref[:, sl])
        kn = head_norm(k_ref[:, sl].astype(F32), kg_ref[:, sl])
        term = fr // LANES
        pick = ((fr % LANES == 2 * p) & (fc == HEAD_DIM + term)) | \
               ((fr % LANES == 2 * p + 1) & (fc == LANES + HEAD_DIM + term))
        ka_ref[:, so] = (_mm(kn, spread) + _mm(f3, pick.astype(BF16))).astype(BF16)
        qt_ref[so, :] = (_nt(spread_t, qn) + q_ones).astype(BF16)
        vt_ref[so, :] = (_nt(spread_t, v_ref[:, sl]) + v_ones).astype(BF16)


def _fox_prep(proj, f3, q_gain_row, k_gain_row):
    m = proj.shape[0]
    tm = min(512, m)
    wide = N_HEADS * LANES
    return pl.pallas_call(
        _fox_prep_body,
        grid=(m // tm,),
        in_specs=[
            pl.BlockSpec((tm, D_MODEL), lambda i: (i, 0)),
            pl.BlockSpec((tm, D_MODEL), lambda i: (i, 1)),
            pl.BlockSpec((tm, D_MODEL), lambda i: (i, 2)),
            pl.BlockSpec((tm, 3 * LANES), lambda i: (i, 0)),
            pl.BlockSpec((1, D_MODEL), lambda i: (0, 0)),
            pl.BlockSpec((1, D_MODEL), lambda i: (0, 0)),
        ],
        out_specs=[
            pl.BlockSpec((tm, wide), lambda i: (i, 0)),
            pl.BlockSpec((wide, tm), lambda i: (0, i)),
            pl.BlockSpec((wide, tm), lambda i: (0, i)),
        ],
        out_shape=[
            jax.ShapeDtypeStruct((m, wide), BF16),
            jax.ShapeDtypeStruct((wide, m), BF16),
            jax.ShapeDtypeStruct((wide, m), BF16),
        ],
        compiler_params=_params("parallel"),
        name="fox_prep",
    )(proj, proj, proj, f3, q_gain_row, k_gain_row)


def _fox_t_body(ka_ref, qt_ref, vt_ref, gate_ref, o_ref):
    T = qt_ref.shape[1]
    KV = min(FOXT_KV, ka_ref.shape[0])
    assert KV % T == 0
    i = pl.program_id(2)
    n_full = (i * T) // KV
    k_off = lax.broadcasted_iota(jnp.int32, (KV, T), 0)
    q_pos = i * T + lax.broadcasted_iota(jnp.int32, (KV, T), 1)
    VR = FOXT_V_ROWS

    def head_step(h, j, state, masked):
        m, acc = state
        ks = pl.ds(pl.multiple_of(j * KV, KV), KV)
        hs = slice(h * LANES, (h + 1) * LANES)
        s = _mm(ka_ref[ks, hs], qt_ref[hs, :])
        if masked:
            s = jnp.where(j * KV + k_off <= q_pos, s, -jnp.inf)
        yield
        m_new = jnp.maximum(m, jnp.max(s, axis=0, keepdims=True))
        pt = jnp.exp2(s - m_new).astype(BF16)
        pv = _mm(vt_ref[h * LANES:h * LANES + VR, ks], pt)
        yield
        return m_new, acc * jnp.exp2(m - m_new) + pv

    for h0 in range(0, FOXT_HEADS, FOXT_UNROLL):
        heads = list(range(h0, h0 + FOXT_UNROLL))

        def kv_step(j, states, masked):
            return tuple(_round_robin([head_step(h, j, st, masked) for h, st in zip(heads, states)]))

        init = (jnp.full((1, T), -jnp.inf, F32), jnp.zeros((VR, T), F32))
        states = lax.fori_loop(0, n_full, lambda j, st: kv_step(j, st, False), (init,) * len(heads))
        states = kv_step(n_full, states, True)
        outs = [acc[:HEAD_DIM] / acc[HEAD_DIM:HEAD_DIM + 1] for _, acc in states]
        for n in range(0, len(heads), 2):
            pair_t = jnp.concatenate([outs[n], outs[n + 1]], axis=0)
            sl = slice((h0 + n) * HEAD_DIM, (h0 + n + 2) * HEAD_DIM)
            o_ref[:, sl] = (pair_t.T * _silu(gate_ref[:, sl].astype(F32))).astype(o_ref.dtype)


def _fox_t_attention(ka, qt, vt, proj, bsz, seq):
    T = min(FOXT_BLOCK, seq)
    nq = seq // T
    m = bsz * seq
    gw = FOXT_HEADS * HEAD_DIM
    sw = FOXT_HEADS * LANES
    n_groups = N_HEADS // FOXT_HEADS
    gate_col0 = 3 * D_MODEL // gw
    return pl.pallas_call(
        _fox_t_body,
        grid=(bsz, n_groups, nq),
        in_specs=[
            pl.BlockSpec((seq, sw), lambda b, g, i: (b, g)),
            pl.BlockSpec((sw, T), lambda b, g, i: (g, b * nq + i)),
            pl.BlockSpec((sw, seq), lambda b, g, i: (g, b)),
            pl.BlockSpec((T, gw), lambda b, g, i: (b * nq + i, gate_col0 + g)),
        ],
        out_specs=pl.BlockSpec((T, gw), lambda b, g, i: (b * nq + i, g)),
        out_shape=jax.ShapeDtypeStruct((m, D_MODEL), BF16),
        compiler_params=_params("parallel", "parallel", "arbitrary"),
        name="fox_attention",
    )(ka, qt, vt, proj)


def _fox_layer(x2, bsz, seq, norm, w_in, b_f, q_gain, k_gain, w_out):
    W = D_MODEL
    proj, f_cols, _ = _norm_matmul(x2, norm, w_in[:, :4 * W].astype(BF16), w_in[:, 4 * W:],
                                   name="fox_in_proj")
    f3 = _fox_decay(f_cols, b_f, bsz, seq)
    scale = HEAD_DIM ** -0.5 * LOG2_E
    q_gain_row = (jnp.tile(q_gain, N_HEADS) * scale).reshape(1, W).astype(F32)
    k_gain_row = jnp.tile(k_gain, N_HEADS).reshape(1, W).astype(F32)
    ka, qt, vt = _fox_prep(proj, f3, q_gain_row, k_gain_row)
    a = _fox_t_attention(ka, qt, vt, proj, bsz, seq)
    return _out_proj(a, w_out.astype(BF16), x2)


def kernel(x, rel_bias, swa_norm, swa_w_in, swa_q_gain, swa_k_gain, swa_sinks, swa_w_out, rwkv_norm, rwkv_mix, rwkv_w_in, rwkv_w0, rwkv_w_lora1, rwkv_w_lora2, rwkv_a0, rwkv_a_lora1, rwkv_a_lora2, rwkv_k_k, rwkv_k_a, rwkv_r_k, rwkv_gn_w, rwkv_gn_b, rwkv_w_out, mlstm_norm, mlstm_w_in, mlstm_b_i, mlstm_b_f, mlstm_h_gain, mlstm_w_out, fox_norm, fox_w_in, fox_b_f, fox_q_gain, fox_k_gain, fox_w_out):
    bsz, seq, d = x.shape
    depth = swa_norm.shape[0] + rwkv_norm.shape[0] + mlstm_norm.shape[0] + fox_norm.shape[0]
    x2 = x.reshape(bsz * seq, d)
    for layer in range(depth):
        kind, idx = layer % 4, layer // 4
        if kind == 0:
            x2 = _swa_layer(x2, bsz, seq, rel_bias, swa_norm[idx], swa_w_in[idx], swa_q_gain[idx],
                            swa_k_gain[idx], swa_sinks[idx], swa_w_out[idx])
        elif kind == 1:
            x2 = _rwkv_layer(x2, bsz, seq, rwkv_norm[idx], rwkv_mix[idx], rwkv_w_in[idx],
                             rwkv_w0[idx], rwkv_w_lora1[idx], rwkv_w_lora2[idx], rwkv_a0[idx],
                             rwkv_a_lora1[idx], rwkv_a_lora2[idx], rwkv_k_k[idx], rwkv_k_a[idx],
                             rwkv_r_k[idx], rwkv_gn_w[idx], rwkv_gn_b[idx], rwkv_w_out[idx])
        elif kind == 2:
            x2 = _mlstm_layer(x2, bsz, seq, mlstm_norm[idx], mlstm_w_in[idx], mlstm_b_i[idx],
                              mlstm_b_f[idx], mlstm_h_gain[idx], mlstm_w_out[idx])
        else:
            x2 = _fox_layer(x2, bsz, seq, fox_norm[idx], fox_w_in[idx], fox_b_f[idx],
                            fox_q_gain[idx], fox_k_gain[idx], fox_w_out[idx])
    return x2.reshape(bsz, seq, d)
```

```python
import functools
import math

import jax
import jax.numpy as jnp
from jax import lax
from jax.experimental import pallas as pl
from jax.experimental.pallas import tpu as pltpu

F32 = jnp.float32
BF16 = jnp.bfloat16
HIGHEST = lax.Precision.HIGHEST

D_MODEL = 2048
RMS_EPS = 1e-6
GN_EPS = 64e-5
LOG2_E = math.log2(math.e)

HEAD_DIM = 64
N_HEADS = D_MODEL // HEAD_DIM
LANES = 128
N_PAIRS = D_MODEL // LANES

SWA_KV_HEADS = 4
SWA_GROUP = N_HEADS // SWA_KV_HEADS
SWA_KV_WIDTH = SWA_KV_HEADS * HEAD_DIM
SWA_BLOCK = 128
SWA_PAIR_UNROLL = 8
REL_BUCKETS = 32
REL_MAX_DIST = 128

RWKV_CHUNK = 64
RWKV_TM = 512
RWKV_LORA_PAD = 128
RWKV_TILE_HEADS = 2
RWKV_TILE_UNROLL = 16

MLSTM_HEADS = 8
MLSTM_V_DIM = 256
MLSTM_QK_DIM = 128
MLSTM_CHUNK = 128
MLSTM_HEAD_UNROLL = 4
MLSTM_MAIN = 2 * MLSTM_HEADS * MLSTM_QK_DIM + 3 * MLSTM_HEADS * MLSTM_V_DIM

FOXT_BLOCK = 256
FOXT_HEADS = 16
FOXT_KV = 256
FOXT_UNROLL = 8
FOXT_V_ROWS = 80

VMEM_LIMIT = 56 * 1024 * 1024
PROJ_TM = 1024
PROJ_TN_CHOICES = (1536, 1024, 512, 256, 128)


def _params(*sem):
    return pltpu.CompilerParams(dimension_semantics=sem, vmem_limit_bytes=VMEM_LIMIT)


def _nt(a, b, precision=None):
    return lax.dot_general(a, b, (((1,), (1,)), ((), ())), precision=precision,
                           preferred_element_type=F32)


def _tn(a, b, precision=None):
    return lax.dot_general(a, b, (((0,), (0,)), ((), ())), precision=precision,
                           preferred_element_type=F32)


def _mm(a, b, precision=None):
    return jnp.dot(a, b, precision=precision, preferred_element_type=F32)


def _rms_rows(x, g):
    return x * lax.rsqrt(jnp.mean(x * x, axis=-1, keepdims=True) + RMS_EPS) * g


def _silu(x):
    return x * jax.nn.sigmoid(x)


def _round_robin(gens):
    results = [None] * len(gens)
    live = list(range(len(gens)))
    while live:
        for n in list(live):
            try:
                next(gens[n])
            except StopIteration as done:
                results[n] = done.value
                live.remove(n)
    return results


def _norm_matmul_body(has_gates, x_ref, g_ref, w_ref, *refs):
    if has_gates:
        wc_ref, wr_ref, o_ref, cols_ref, rows_ref, xn_ref = refs
    else:
        o_ref, xn_ref = refs

    @pl.when(pl.program_id(1) == 0)
    def _():
        xn = _rms_rows(x_ref[...], g_ref[...]).astype(BF16)
        xn_ref[...] = xn
        if has_gates:
            cols_ref[...] = _mm(xn, wc_ref[...])
            rows_ref[...] = _nt(wr_ref[...], xn)

    o_ref[...] = _mm(xn_ref[...], w_ref[...]).astype(o_ref.dtype)


def _proj_tiles(m, n):
    tm = min(PROJ_TM, m)
    tn = next(t for t in PROJ_TN_CHOICES if n % t == 0)
    return tm, tn


def _norm_matmul(x, g, w, w_gates=None, *, name="norm_matmul"):
    m, k = x.shape
    n = w.shape[1]
    tm, tn = _proj_tiles(m, n)
    has_gates = w_gates is not None
    in_specs = [
        pl.BlockSpec((tm, k), lambda i, j: (i, 0)),
        pl.BlockSpec((1, k), lambda i, j: (0, 0)),
        pl.BlockSpec((k, tn), lambda i, j: (0, j)),
    ]
    out_specs = [pl.BlockSpec((tm, tn), lambda i, j: (i, j))]
    out_shape = [jax.ShapeDtypeStruct((m, n), BF16)]
    args = [x, g.reshape(1, k), w]
    if has_gates:
        n_g = w_gates.shape[1]
        in_specs += [pl.BlockSpec((k, LANES), lambda i, j: (0, 0)),
                     pl.BlockSpec((n_g, k), lambda i, j: (0, 0))]
        out_specs += [pl.BlockSpec((tm, LANES), lambda i, j: (i, 0)),
                      pl.BlockSpec((n_g, tm), lambda i, j: (0, i))]
        out_shape += [jax.ShapeDtypeStruct((m, LANES), F32), jax.ShapeDtypeStruct((n_g, m), F32)]
        args += [jnp.zeros((k, LANES), BF16).at[:, :n_g].set(w_gates.astype(BF16)),
                 w_gates.T.astype(BF16)]
    outs = pl.pallas_call(
        functools.partial(_norm_matmul_body, has_gates),
        grid=(m // tm, n // tn),
        in_specs=in_specs,
        out_specs=out_specs,
        out_shape=out_shape,
        scratch_shapes=[pltpu.VMEM((tm, k), BF16)],
        compiler_params=_params("parallel", "arbitrary"),
        name=name,
    )(*args)
    return outs if has_gates else outs[0]


def _out_proj_body(a_ref, w_ref, r_ref, o_ref):
    o_ref[...] = r_ref[...] + _mm(a_ref[...], w_ref[...])


def _out_proj(a, w, resid):
    m, k = a.shape
    n = w.shape[1]
    tm = min(512, m)
    return pl.pallas_call(
        _out_proj_body,
        grid=(m // tm,),
        in_specs=[
            pl.BlockSpec((tm, k), lambda i: (i, 0)),
            pl.BlockSpec((k, n), lambda i: (0, 0)),
            pl.BlockSpec((tm, n), lambda i: (i, 0)),
        ],
        out_specs=pl.BlockSpec((tm, n), lambda i: (i, 0)),
        out_shape=jax.ShapeDtypeStruct((m, n), F32),
        compiler_params=_params("parallel"),
        name="out_proj",
    )(a, w, resid)


def _t5_bucket(dist):
    max_exact = REL_BUCKETS // 2
    d_f = jnp.maximum(dist, 1).astype(F32)
    large = max_exact + (jnp.log(d_f / max_exact) / math.log(REL_MAX_DIST / max_exact)
                         * (REL_BUCKETS - max_exact)).astype(jnp.int32)
    large = jnp.minimum(large, REL_BUCKETS - 1)
    return jnp.where(dist < max_exact, dist, large)


def _swa_body(q_ref, gate_ref, kc_ref, kp_ref, vc_ref, vp_ref, bias_ref, sink_ref, qg_ref, kg_ref,
              o_ref):
    L = SWA_BLOCK
    k0 = lax.broadcasted_iota(jnp.int32, (2 * L, LANES), 1) < HEAD_DIM
    t0 = lax.broadcasted_iota(jnp.int32, (L, LANES), 1) < HEAD_DIM
    bi = lax.broadcasted_iota(jnp.int32, (LANES, LANES), 0) // HEAD_DIM
    bj = lax.broadcasted_iota(jnp.int32, (LANES, LANES), 1) // HEAD_DIM
    bd_ones = (bi == bj).astype(BF16)
    zero = jnp.zeros((2 * L, LANES), F32)
    one = jnp.ones((2 * L, LANES), F32)
    ones_bd = jnp.concatenate([jnp.where(k0, one, zero), jnp.where(k0, zero, one)], axis=0).astype(BF16)

    def head_mean_sq(z):
        return _mm((z * z).astype(BF16), bd_ones) * (1.0 / HEAD_DIM)

    def both_halves(x, head_in_low_lanes):
        if head_in_low_lanes:
            lo = jnp.where(k0, x, zero)
            hi = pltpu.roll(lo, HEAD_DIM, axis=1)
        else:
            hi = jnp.where(k0, zero, x)
            lo = pltpu.roll(hi, HEAD_DIM, axis=1)
        return jnp.concatenate([lo, hi], axis=0)

    k_all = jnp.concatenate([kp_ref[...], kc_ref[...]], axis=0).astype(F32)
    v_all = jnp.concatenate([vp_ref[...], vc_ref[...]], axis=0).astype(F32)
    kbd, rhs = [], []
    for t in range(SWA_KV_WIDTH // LANES):
        kt = k_all[:, t * LANES:(t + 1) * LANES]
        kn = kt * lax.rsqrt(head_mean_sq(kt) + RMS_EPS) * kg_ref[...]
        vt = v_all[:, t * LANES:(t + 1) * LANES]
        for low in (True, False):
            kbd.append(both_halves(kn, low).astype(BF16))
            rhs.append(jnp.concatenate([both_halves(vt, low).astype(BF16), ones_bd], axis=1))

    def pair(p):
        kh = p // (SWA_GROUP // 2)
        sl = slice(p * LANES, (p + 1) * LANES)
        q2 = q_ref[:, sl].astype(F32)
        ms = head_mean_sq(q2)
        yield
        qn = (q2 * lax.rsqrt(ms + RMS_EPS) * qg_ref[...]).astype(BF16)
        s = _nt(qn, kbd[kh]) + bias_ref[0, p]
        yield
        sink_a = sink_ref[2 * p]
        sink_b = sink_ref[2 * p + 1]
        m_a = jnp.maximum(jnp.max(s[:, :2 * L], axis=-1, keepdims=True), sink_a)
        m_b = jnp.maximum(jnp.max(s[:, 2 * L:], axis=-1, keepdims=True), sink_b)
        pm = jnp.concatenate([jnp.exp(s[:, :2 * L] - m_a), jnp.exp(s[:, 2 * L:] - m_b)], axis=1)
        pv = _mm(pm.astype(BF16), rhs[kh])
        yield
        sink_t = jnp.where(t0, jnp.exp(sink_a - m_a), jnp.exp(sink_b - m_b))
        o = pv[:, :LANES] / (pv[:, LANES:] + sink_t)
        return (o * _silu(gate_ref[:, sl].astype(F32))).astype(o_ref.dtype)

    for g0 in range(0, N_PAIRS, SWA_PAIR_UNROLL):
        outs = _round_robin([pair(p) for p in range(g0, g0 + SWA_PAIR_UNROLL)])
        for n, o in enumerate(outs):
            o_ref[:, (g0 + n) * LANES:(g0 + n + 1) * LANES] = o


def _swa_attention(proj, bias, sinks, q_gain2, k_gain2, bsz, seq):
    L = SWA_BLOCK
    nb = seq // L
    m = bsz * seq
    kcol = 2 * D_MODEL // SWA_KV_WIDTH
    prev = lambda r: jnp.where(r % nb == 0, r, r - 1)
    return pl.pallas_call(
        _swa_body,
        grid=(bsz * nb,),
        in_specs=[
            pl.BlockSpec((L, D_MODEL), lambda r: (r, 0)),
            pl.BlockSpec((L, D_MODEL), lambda r: (r, 1)),
            pl.BlockSpec((L, SWA_KV_WIDTH), lambda r: (r, kcol)),
            pl.BlockSpec((L, SWA_KV_WIDTH), lambda r: (prev(r), kcol)),
            pl.BlockSpec((L, SWA_KV_WIDTH), lambda r: (r, kcol + 1)),
            pl.BlockSpec((L, SWA_KV_WIDTH), lambda r: (prev(r), kcol + 1)),
            pl.BlockSpec((1, N_PAIRS, L, 4 * L), lambda r: (jnp.where(r % nb == 0, 0, 1), 0, 0, 0)),
            pl.BlockSpec(memory_space=pltpu.SMEM),
            pl.BlockSpec((1, LANES), lambda r: (0, 0)),
            pl.BlockSpec((1, LANES), lambda r: (0, 0)),
        ],
        out_specs=pl.BlockSpec((L, D_MODEL), lambda r: (r, 0)),
        out_shape=jax.ShapeDtypeStruct((m, D_MODEL), BF16),
        compiler_params=_params("parallel"),
        name="swa_attention",
    )(proj, proj, proj, proj, proj, proj, bias, sinks, q_gain2, k_gain2)


def _swa_bias_tables(rel_bias):
    L = SWA_BLOCK
    qi = jnp.arange(L)[:, None]
    kj = jnp.arange(2 * L)[None, :]
    dist = qi + L - kj
    in_band = (dist >= 0) & (dist < L)
    onehot = (_t5_bucket(jnp.maximum(dist, 0))[..., None] == jnp.arange(REL_BUCKETS)).astype(F32)
    bias = jnp.einsum("ijb,bh->hij", onehot, rel_bias.astype(F32), precision=HIGHEST)
    general = jnp.where(in_band[None], bias, -jnp.inf)
    first = jnp.where((in_band & (kj >= L))[None], bias, -jnp.inf)
    pairs = lambda t: t.reshape(N_PAIRS, 2, L, 2 * L).transpose(0, 2, 1, 3).reshape(N_PAIRS, L, 4 * L)
    return jnp.stack([pairs(first), pairs(general)])


def _swa_layer(x2, bsz, seq, rel_bias, norm, w_in, q_gain, k_gain, sinks, w_out):
    W = D_MODEL
    kvw = SWA_KV_WIDTH
    w_perm = jnp.concatenate([w_in[:, :W], w_in[:, W + 2 * kvw:], w_in[:, W:W + 2 * kvw]], axis=1)
    proj = _norm_matmul(x2, norm, w_perm.astype(BF16), name="swa_in_proj")
    q_gain2 = (jnp.tile(q_gain, 2) * HEAD_DIM ** -0.5).reshape(1, LANES).astype(F32)
    k_gain2 = jnp.tile(k_gain, 2).reshape(1, LANES).astype(F32)
    a = _swa_attention(proj, _swa_bias_tables(rel_bias), sinks.astype(F32), q_gain2, k_gain2, bsz, seq)
    return _out_proj(a, w_out.astype(BF16), x2)


def _shifted_mix(x_ref, xp_ref, g_ref, seq_start):
    h = _rms_rows(x_ref[...], g_ref[...])
    h_prev_row = _rms_rows(xp_ref[7:8, :], g_ref[...])
    h_prev_row = jnp.where(seq_start, 0.0, h_prev_row)
    rolled = pltpu.roll(h, 1, axis=0)
    row = lax.broadcasted_iota(jnp.int32, h.shape, 0)
    shifted = jnp.where(row == 0, h_prev_row, rolled)
    return h, shifted - h


def _rwkv_in_body(tiles_per_seq, x_ref, xp_ref, g_ref, mix_ref, w_ref, o_ref, xm_ref):
    i = pl.program_id(0)

    @pl.when((pl.program_id(1) == 0) & (pl.program_id(2) == 0))
    def _():
        h, xx = _shifted_mix(x_ref, xp_ref, g_ref, i % tiles_per_seq == 0)
        for c in range(4):
            xm_ref[c] = (h + xx * mix_ref[c:c + 1, :]).astype(BF16)

    c = pl.program_id(1)
    o_ref[0] = _mm(xm_ref[c], w_ref[0]).astype(o_ref.dtype)


def _rwkv_in_proj(x2, seq, norm, mix8, w4):
    m, k = x2.shape
    tm = min(RWKV_TM, seq)
    tn = k
    rows8 = tm // 8
    return pl.pallas_call(
        functools.partial(_rwkv_in_body, seq // tm),
        grid=(m // tm, 4, k // tn),
        in_specs=[
            pl.BlockSpec((tm, k), lambda i, c, j: (i, 0)),
            pl.BlockSpec((8, k), lambda i, c, j: (jnp.maximum(i * rows8 - 1, 0), 0)),
            pl.BlockSpec((1, k), lambda i, c, j: (0, 0)),
            pl.BlockSpec((8, k), lambda i, c, j: (0, 0)),
            pl.BlockSpec((1, k, tn), lambda i, c, j: (c, 0, j)),
        ],
        out_specs=pl.BlockSpec((1, tm, tn), lambda i, c, j: (c, i, j)),
        out_shape=jax.ShapeDtypeStruct((4, m, k), BF16),
        scratch_shapes=[pltpu.VMEM((4, tm, k), BF16)],
        compiler_params=_params("parallel", "arbitrary", "arbitrary"),
        name="rwkv_in_proj",
    )(x2, x2, norm.reshape(1, k), mix8, w4)


def _rwkv_lora_body(tiles_per_seq, x_ref, xp_ref, g_ref, mix_ref, w1_ref, w2_ref, a1_ref, a2_ref,
                    w0_ref, a0_ref, lw_ref, a_ref):
    i = pl.program_id(0)
    h, xx = _shifted_mix(x_ref, xp_ref, g_ref, i % tiles_per_seq == 0)
    xw = (h + xx * mix_ref[4:5, :]).astype(BF16)
    xa = (h + xx * mix_ref[5:6, :]).astype(BF16)
    t = jnp.tanh(_mm(xw, w1_ref[...])).astype(BF16)
    w_pre = w0_ref[...] + _mm(t, w2_ref[...])
    lw_ref[...] = -jnp.exp(-jax.nn.softplus(-w_pre) - 0.5)
    u = _mm(xa, a1_ref[...]).astype(BF16)
    a_ref[...] = jax.nn.sigmoid(a0_ref[...] + _mm(u, a2_ref[...]))


def _rwkv_lora(x2, seq, norm, mix8, w1, w2, a1, a2, w0, a0):
    m, k = x2.shape
    tm = min(RWKV_TM, seq)
    rows8 = tm // 8
    P = RWKV_LORA_PAD
    full = lambda shape: pl.BlockSpec(shape, lambda i: tuple(0 for _ in shape))
    return pl.pallas_call(
        functools.partial(_rwkv_lora_body, seq // tm),
        grid=(m // tm,),
        in_specs=[
            pl.BlockSpec((tm, k), lambda i: (i, 0)),
            pl.BlockSpec((8, k), lambda i: (jnp.maximum(i * rows8 - 1, 0), 0)),
            full((1, k)), full((8, k)),
            full((k, P)), full((P, k)), full((k, P)), full((P, k)),
            full((1, k)), full((1, k)),
        ],
        out_specs=[pl.BlockSpec((tm, k), lambda i: (i, 0)), pl.BlockSpec((tm, k), lambda i: (i, 0))],
        out_shape=[jax.ShapeDtypeStruct((m, k), F32), jax.ShapeDtypeStruct((m, k), F32)],
        compiler_params=_params("parallel"),
        name="rwkv_lora",
    )(x2, x2, norm.reshape(1, k), mix8, w1, w2, a1, a2, w0.reshape(1, k), a0.reshape(1, k))


def _rwkv_scan_body(r_ref, k_ref, v_ref, g_ref, lw_ref, a_ref, kk_ref, ka_ref, rk_ref,
                    gnw_ref, gnb_ref, o_ref, ht_ref):
    C = RWKV_CHUNK
    G = RWKV_TILE_HEADS
    W = G * HEAD_DIM
    assert C == HEAD_DIM

    @pl.when(pl.program_id(1) == 0)
    def _():
        ht_ref[...] = jnp.zeros_like(ht_ref)

    lane = lax.broadcasted_iota(jnp.int32, (C, W), 1)
    tok = lax.broadcasted_iota(jnp.int32, (C, W), 0)
    col = lane % HEAD_DIM
    strict = tok > col
    incl = tok >= col
    lane_b = lane.astype(F32).astype(BF16)
    head_masks = [(lane_b >= n * HEAD_DIM) & (lane_b < (n + 1) * HEAD_DIM) for n in range(G)]
    ti = lax.broadcasted_iota(jnp.int32, (C, C), 0)
    tj = lax.broadcasted_iota(jnp.int32, (C, C), 1)
    tri = (ti >= tj).astype(F32)
    bi = lax.broadcasted_iota(jnp.int32, (W, W), 0) // HEAD_DIM
    bj = lax.broadcasted_iota(jnp.int32, (W, W), 1) // HEAD_DIM
    bd = bi == bj
    bd_ones = bd.astype(BF16)
    tri2 = jnp.concatenate([tri, tri], axis=1).astype(BF16)

    def head_sum(z):
        return _mm(z.astype(BF16), bd_ones)

    def stack(z):
        zb = z.astype(BF16)
        zero = jnp.zeros_like(zb)
        return jnp.concatenate([jnp.where(hm, zb, zero) for hm in head_masks], axis=0)

    def load(p):
        sl = pl.ds(pl.multiple_of(p * W, W), W)
        f32 = lambda ref: ref[0, :, sl].astype(F32)
        return (f32(r_ref), f32(k_ref), f32(v_ref), f32(g_ref), lw_ref[:, sl],
                a_ref[:, sl], kk_ref[:, sl], ka_ref[:, sl], rk_ref[:, sl], gnw_ref[:, sl],
                gnb_ref[:, sl], ht_ref[p])

    def compute(r, k, v, g, lw, a, k_k, k_a, r_k, gn_w, gn_b, ht):
        kkp = k * k_k
        kp = k * (1.0 + (a - 1.0) * k_a)
        sums = head_sum(jnp.concatenate([kkp * kkp, r * kp * r_k], axis=0))
        ss, bonus_dot = sums[:C], sums[C:]
        lw_hi = lw.astype(BF16)
        lw_lo = (lw - lw_hi.astype(F32)).astype(BF16)
        cum = _mm(tri2, jnp.concatenate([lw_hi, lw_lo], axis=0))
        yield
        kk = kkp / jnp.maximum(jnp.sqrt(ss), 1e-12)
        cum_end = cum[C - 1:C, :]
        g_in = jnp.exp(cum)
        g_prev = jnp.exp(cum - lw)
        g_inv = jnp.exp(-cum)
        g_end = jnp.exp(cum_end - cum)
        kka = kk * a
        a_t = -kk * g_prev
        r_t = r * g_in
        b_t = kka * g_inv
        k_t = kp * g_inv

        x1 = jnp.concatenate([a_t, r_t], axis=0).astype(BF16)
        gbk = _nt(x1, jnp.concatenate([stack(b_t), stack(k_t)], axis=0))
        xh = _nt(x1, ht.astype(BF16))
        yield
        zero = jnp.zeros((C, W), F32)
        l_ab = jnp.where(strict, gbk[:C, :W], zero)
        l_ak = jnp.where(strict, gbk[:C, W:], zero)
        p_rb = jnp.where(incl, gbk[C:, :W], zero)
        p_rk = jnp.where(incl, gbk[C:, W:], zero)

        t_inv = jnp.where(tok == col, 1.0, zero) + jnp.where(
            (tok // 2 == col // 2) & (tok % 2 == 1) & (col % 2 == 0), l_ab, zero)
        v2 = stack(v)
        rhs = xh[:C] + _mm(l_ak.astype(BF16), v2)
        s = 2
        while s < C:
            sel = (tok // (2 * s) == col // (2 * s)) & (tok % (2 * s) >= s) & (col % (2 * s) < s)
            l_s = jnp.where(sel, l_ab, zero)
            tl = _mm(t_inv.astype(BF16), stack(l_s))
            yield
            t_inv = t_inv + _mm(tl.astype(BF16), stack(t_inv))
            yield
            s *= 2

        u = _mm(t_inv.astype(BF16), stack(rhs))
        yield
        p_cat = jnp.concatenate([p_rb, p_rk], axis=1).astype(BF16)
        y = xh[C:] + _mm(p_cat, jnp.concatenate([stack(u), v2], axis=0))
        uv = jnp.concatenate([u, v], axis=0).astype(BF16)
        bk_end = jnp.concatenate([kka * g_end, kp * g_end], axis=0).astype(BF16)
        upd = _tn(uv, bk_end)
        yield
        ht_new = ht * jnp.exp(cum_end) + jnp.where(bd, upd, 0.0)
        mu = head_sum(y) * (1.0 / HEAD_DIM)
        yield
        dy = y - mu
        var = head_sum(dy * dy) * (1.0 / HEAD_DIM)
        yield
        yn = dy * lax.rsqrt(var + GN_EPS) * gn_w + gn_b
        return ((yn + bonus_dot * v) * _silu(g)).astype(o_ref.dtype), ht_new

    def group(gi, carry):
        base = gi * RWKV_TILE_UNROLL
        ins = [load(base + n) for n in range(RWKV_TILE_UNROLL)]
        outs = _round_robin([compute(*x) for x in ins])
        for n, (o, ht_new) in enumerate(outs):
            p = base + n
            o_ref[:, pl.ds(pl.multiple_of(p * W, W), W)] = o
            ht_ref[p] = ht_new
        return carry

    lax.fori_loop(0, D_MODEL // W // RWKV_TILE_UNROLL, group, 0)


def _rwkv_scan(rkvg, lw, a, k_k, k_a, r_k, gn_w, gn_b, bsz, seq):
    C = RWKV_CHUNK
    nc = seq // C
    m = bsz * seq
    k = D_MODEL
    tile_w = RWKV_TILE_HEADS * HEAD_DIM
    part = lambda c: pl.BlockSpec((1, C, k), lambda b, j, c=c: (c, b * nc + j, 0))
    tile = pl.BlockSpec((C, k), lambda b, j: (b * nc + j, 0))
    vec = pl.BlockSpec((1, k), lambda b, j: (0, 0))
    return pl.pallas_call(
        _rwkv_scan_body,
        grid=(bsz, nc),
        in_specs=[part(0), part(1), part(2), part(3), tile, tile, vec, vec, vec, vec, vec],
        out_specs=tile,
        out_shape=jax.ShapeDtypeStruct((m, k), BF16),
        scratch_shapes=[pltpu.VMEM((k // tile_w, tile_w, tile_w), F32)],
        compiler_params=_params("parallel", "arbitrary"),
        name="rwkv_scan",
    )(rkvg, rkvg, rkvg, rkvg, lw, a, k_k.reshape(1, k), k_a.reshape(1, k), r_k.reshape(1, k),
      gn_w.reshape(1, k), gn_b.reshape(1, k))


def _rwkv_layer(x2, bsz, seq, norm, mix, w_in, w0, w_lora1, w_lora2, a0, a_lora1, a_lora2,
                k_k, k_a, r_k, gn_w, gn_b, w_out):
    k = D_MODEL
    P = RWKV_LORA_PAD
    mix8 = jnp.zeros((8, k), F32).at[:mix.shape[0]].set(mix)
    pad_cols = lambda w: jnp.zeros((k, P), BF16).at[:, :w.shape[1]].set(w.astype(BF16))
    pad_rows = lambda w: jnp.zeros((P, k), BF16).at[:w.shape[0]].set(w.astype(BF16))
    rkvg = _rwkv_in_proj(x2, seq, norm, mix8, w_in.astype(BF16))
    lw, a = _rwkv_lora(x2, seq, norm, mix8, pad_cols(w_lora1), pad_rows(w_lora2),
                       pad_cols(a_lora1), pad_rows(a_lora2), w0, a0)
    y = _rwkv_scan(rkvg, lw, a, k_k, k_a, r_k, gn_w, gn_b, bsz, seq)
    return _out_proj(y, w_out.astype(BF16), x2)


def _mlstm_body(q_ref, k_ref, v_ref, op_ref, gate_ref, gc_ref, gr_ref, bi_ref, bf_ref, hg_ref,
                o_ref, ct_ref, m_ref):
    L = MLSTM_CHUNK
    H = MLSTM_HEADS
    dk = MLSTM_QK_DIM
    dv = MLSTM_V_DIM

    @pl.when(pl.program_id(1) == 0)
    def _():
        ct_ref[...] = jnp.zeros_like(ct_ref)
        m_ref[...] = jnp.zeros_like(m_ref)

    ti = lax.broadcasted_iota(jnp.int32, (L, L), 0)
    tj = lax.broadcasted_iota(jnp.int32, (L, L), 1)
    tril = ti >= tj
    tri_f = tril.astype(F32)
    gc = gc_ref[...] + bi_ref[...]
    gr = gr_ref[...] + bf_ref[...]
    lane = lax.broadcasted_iota(jnp.int32, gc.shape, 1)
    f_cols = jnp.where((lane >= H) & (lane < 2 * H), jax.nn.log_sigmoid(gc), 0.0)
    b_cols = _mm(tri_f, f_cols, HIGHEST)
    f_rows = jax.nn.log_sigmoid(gr)
    b_rows = _nt(f_rows, tri_f, HIGHEST)
    ones_blk = jnp.ones((L, LANES), BF16)

    def head(h):
        q = q_ref[:, h * dk:(h + 1) * dk]
        k = k_ref[:, h * dk:(h + 1) * dk]
        v = v_ref[:, h * dv:(h + 1) * dv]
        v_aug = jnp.concatenate([v, ones_blk], axis=-1)
        ct = ct_ref[h]
        m_prev = m_ref[h][:, :1]
        qk = _nt(q, k)
        q_ct = _mm(q, ct.astype(BF16))
        yield
        i_col = gc[:, h:h + 1]
        i_row = gr[h:h + 1, :]
        b_col = b_cols[:, H + h:H + h + 1]
        b_row = b_rows[H + h:H + h + 1, :]
        g_tot = b_col[L - 1:L, :]
        dmat = jnp.where(tril, b_col - b_row + i_row, -jnp.inf)
        inter = b_col + m_prev
        m_t = jnp.maximum(inter, jnp.max(dmat, axis=-1, keepdims=True))
        w_intra = jnp.exp(dmat - m_t) * dk ** -0.5
        w_inter = jnp.exp(inter - m_t) * dk ** -0.5
        sw = qk * w_intra
        num = w_inter * q_ct + _mm(sw.astype(BF16), v_aug)
        decay = g_tot - b_col + i_col
        m_new = jnp.maximum(g_tot + m_prev, jnp.max(decay, axis=0, keepdims=True))
        ws = jnp.exp(decay - m_new)
        carry_scale = jnp.exp(g_tot + m_prev - m_new)
        wv = (ws * v_aug.astype(F32)).astype(BF16)
        upd = _tn(k, wv)
        yield
        den = num[:, dv:dv + 1]
        h_t = num[:, :dv] / jnp.maximum(jnp.abs(den), jnp.exp(-m_t))
        hn = _rms_rows(h_t, hg_ref[:, h * dv:(h + 1) * dv])
        osl = slice(h * dv, (h + 1) * dv)
        out = hn * jax.nn.sigmoid(op_ref[:, osl].astype(F32)) * _silu(gate_ref[:, osl].astype(F32))
        return out.astype(o_ref.dtype), carry_scale * ct + upd, jnp.broadcast_to(m_new, (1, LANES))

    for h0 in range(0, H, MLSTM_HEAD_UNROLL):
        heads = range(h0, h0 + MLSTM_HEAD_UNROLL)
        for h, (out, ct_new, m_new) in zip(heads, _round_robin([head(h) for h in heads])):
            o_ref[:, h * dv:(h + 1) * dv] = out
            ct_ref[h] = ct_new
            m_ref[h] = m_new


def _mlstm_scan(proj, g_cols, g_rows, b_i, b_f, h_gain, bsz, seq):
    L = MLSTM_CHUNK
    H = MLSTM_HEADS
    nc = seq // L
    m = bsz * seq
    qkw = H * MLSTM_QK_DIM
    vw = H * MLSTM_V_DIM
    bias = jnp.concatenate([b_i, b_f]).astype(F32)
    bias_row = jnp.zeros((1, LANES), F32).at[0, :2 * H].set(bias)
    bias_col = bias.reshape(2 * H, 1)
    row = lambda b, j: b * nc + j
    return pl.pallas_call(
        _mlstm_body,
        grid=(bsz, nc),
        in_specs=[
            pl.BlockSpec((L, qkw), lambda b, j: (row(b, j), 0)),
            pl.BlockSpec((L, qkw), lambda b, j: (row(b, j), 1)),
            pl.BlockSpec((L, vw), lambda b, j: (row(b, j), 1)),
            pl.BlockSpec((L, vw), lambda b, j: (row(b, j), 2)),
            pl.BlockSpec((L, vw), lambda b, j: (row(b, j), 3)),
            pl.BlockSpec((L, LANES), lambda b, j: (row(b, j), 0)),
            pl.BlockSpec((2 * H, L), lambda b, j: (0, row(b, j))),
            pl.BlockSpec((1, LANES), lambda b, j: (0, 0)),
            pl.BlockSpec((2 * H, 1), lambda b, j: (0, 0)),
            pl.BlockSpec((1, vw), lambda b, j: (0, 0)),
        ],
        out_specs=pl.BlockSpec((L, vw), lambda b, j: (row(b, j), 0)),
        out_shape=jax.ShapeDtypeStruct((m, vw), BF16),
        scratch_shapes=[
            pltpu.VMEM((H, MLSTM_QK_DIM, MLSTM_V_DIM + LANES), F32),
            pltpu.VMEM((H, 1, LANES), F32),
        ],
        compiler_params=_params("parallel", "arbitrary"),
        name="mlstm_scan",
    )(proj, proj, proj, proj, proj, g_cols, g_rows, bias_row, bias_col, h_gain.reshape(1, vw))


def _mlstm_layer(x2, bsz, seq, norm, w_in, b_i, b_f, h_gain, w_out):
    proj, g_cols, g_rows = _norm_matmul(x2, norm, w_in[:, :MLSTM_MAIN].astype(BF16),
                                        w_in[:, MLSTM_MAIN:], name="mlstm_in_proj")
    a = _mlstm_scan(proj, g_cols, g_rows, b_i, b_f, h_gain, bsz, seq)
    return _out_proj(a, w_out.astype(BF16), x2)


def _fox_decay_body(f_ref, bf_ref, o_ref):
    B = LANES
    seq = f_ref.shape[0]
    lf = jax.nn.log_sigmoid(f_ref[...] + bf_ref[...])
    ti = lax.broadcasted_iota(jnp.int32, (B, B), 0)
    tj = lax.broadcasted_iota(jnp.int32, (B, B), 1)
    tril = (ti >= tj).astype(F32)
    carry = jnp.zeros((1, LANES), F32)
    for c in range(seq // B):
        blk = _mm(tril, lf[c * B:(c + 1) * B, :], HIGHEST) + carry
        carry = blk[B - 1:B, :]
        neg = blk * (-LOG2_E)
        hi = neg.astype(BF16)
        r1 = neg - hi.astype(F32)
        mid = r1.astype(BF16)
        lo = (r1 - mid.astype(F32)).astype(BF16)
        rows = slice(c * B, (c + 1) * B)
        o_ref[rows, 0:LANES] = hi
        o_ref[rows, LANES:2 * LANES] = mid
        o_ref[rows, 2 * LANES:3 * LANES] = lo


def _fox_decay(f_cols, b_f, bsz, seq):
    bias = jnp.zeros((1, LANES), F32).at[0, :N_HEADS].set(b_f.astype(F32))
    return pl.pallas_call(
        _fox_decay_body,
        grid=(bsz,),
        in_specs=[pl.BlockSpec((seq, LANES), lambda b: (b, 0)), pl.BlockSpec((1, LANES), lambda b: (0, 0))],
        out_specs=pl.BlockSpec((seq, 3 * LANES), lambda b: (b, 0)),
        out_shape=jax.ShapeDtypeStruct((bsz * seq, 3 * LANES), BF16),
        compiler_params=_params("parallel"),
        name="fox_decay",
    )(f_cols, bias)


def _fox_prep_body(q_ref, k_ref, v_ref, f3_ref, qg_ref, kg_ref, ka_ref, qt_ref, vt_ref):
    S = 2 * LANES
    r = lax.broadcasted_iota(jnp.int32, (LANES, S), 0)
    c = lax.broadcasted_iota(jnp.int32, (LANES, S), 1)
    spread = ((r < HEAD_DIM) & (c == r)) | ((r >= HEAD_DIM) & (c == r + HEAD_DIM))
    spread = spread.astype(BF16)
    rt = lax.broadcasted_iota(jnp.int32, (S, LANES), 0)
    ct = lax.broadcasted_iota(jnp.int32, (S, LANES), 1)
    spread_t = (((ct < HEAD_DIM) & (rt == ct)) | ((ct >= HEAD_DIM) & (rt == ct + HEAD_DIM))).astype(BF16)
    bi = lax.broadcasted_iota(jnp.int32, (LANES, LANES), 0) // HEAD_DIM
    bj = lax.broadcasted_iota(jnp.int32, (LANES, LANES), 1) // HEAD_DIM
    bd_mean = jnp.where(bi == bj, 1.0 / HEAD_DIM, 0.0).astype(BF16)
    row = lax.broadcasted_iota(jnp.int32, (S, 1), 0) % LANES
    q_ones = ((row >= HEAD_DIM) & (row < HEAD_DIM + 3)).astype(F32)
    v_ones = (row == HEAD_DIM).astype(F32)
    fr = lax.broadcasted_iota(jnp.int32, (3 * LANES, S), 0)
    fc = lax.broadcasted_iota(jnp.int32, (3 * LANES, S), 1)

    def head_norm(x, g):
        ms = _mm((x * x).astype(BF16), bd_mean)
        return (x * lax.rsqrt(ms + RMS_EPS) * g).astype(BF16)

    f3 = f3_ref[...]
    for p in range(N_PAIRS):
        sl = slice(p * LANES, (p + 1) * LANES)
        so = slice(p * S, (p + 1) * S)
        qn = head_norm(q_ref[:, sl].astype(F32), qg_ref[:, sl])
        kn = head_norm(k_ref[:, sl].astype(F32), kg_ref[:, sl])
        term = fr // LANES
        pick = ((fr % LANES == 2 * p) & (fc == HEAD_DIM + term)) | \
               ((fr % LANES == 2 * p + 1) & (fc == LANES + HEAD_DIM + term))
        ka_ref[:, so] = (_mm(kn, spread) + _mm(f3, pick.astype(BF16))).astype(BF16)
        qt_ref[so, :] = (_nt(spread_t, qn) + q_ones).astype(BF16)
        vt_ref[so, :] = (_nt(spread_t, v_ref[:, sl]) + v_ones).astype(BF16)


def _fox_prep(proj, f3, q_gain_row, k_gain_row):
    m = proj.shape[0]
    tm = min(512, m)
    wide = N_HEADS * LANES
    return pl.pallas_call(
        _fox_prep_body,
        grid=(m // tm,),
        in_specs=[
            pl.BlockSpec((tm, D_MODEL), lambda i: (i, 0)),
            pl.BlockSpec((tm, D_MODEL), lambda i: (i, 1)),
            pl.BlockSpec((tm, D_MODEL), lambda i: (i, 2)),
            pl.BlockSpec((tm, 3 * LANES), lambda i: (i, 0)),
            pl.BlockSpec((1, D_MODEL), lambda i: (0, 0)),
            pl.BlockSpec((1, D_MODEL), lambda i: (0, 0)),
        ],
        out_specs=[
            pl.BlockSpec((tm, wide), lambda i: (i, 0)),
            pl.BlockSpec((wide, tm), lambda i: (0, i)),
            pl.BlockSpec((wide, tm), lambda i: (0, i)),
        ],
        out_shape=[
            jax.ShapeDtypeStruct((m, wide), BF16),
            jax.ShapeDtypeStruct((wide, m), BF16),
            jax.ShapeDtypeStruct((wide, m), BF16),
        ],
        compiler_params=_params("parallel"),
        name="fox_prep",
    )(proj, proj, proj, f3, q_gain_row, k_gain_row)


def _fox_t_body(ka_ref, qt_ref, vt_ref, gate_ref, o_ref):
    T = qt_ref.shape[1]
    KV = min(FOXT_KV, ka_ref.shape[0])
    assert KV % T == 0
    i = pl.program_id(2)
    n_full = (i * T) // KV
    k_off = lax.broadcasted_iota(jnp.int32, (KV, T), 0)
    q_pos = i * T + lax.broadcasted_iota(jnp.int32, (KV, T), 1)
    VR = FOXT_V_ROWS

    def head_step(h, j, state, masked):
        m, acc = state
        ks = pl.ds(pl.multiple_of(j * KV, KV), KV)
        hs = slice(h * LANES, (h + 1) * LANES)
        s = _mm(ka_ref[ks, hs], qt_ref[hs, :])
        if masked:
            s = jnp.where(j * KV + k_off <= q_pos, s, -jnp.inf)
        yield
        m_new = jnp.maximum(m, jnp.max(s, axis=0, keepdims=True))
        pt = jnp.exp2(s - m_new).astype(BF16)
        pv = _mm(vt_ref[h * LANES:h * LANES + VR, ks], pt)
        yield
        return m_new, acc * jnp.exp2(m - m_new) + pv

    for h0 in range(0, FOXT_HEADS, FOXT_UNROLL):
        heads = list(range(h0, h0 + FOXT_UNROLL))

        def kv_step(j, states, masked):
            return tuple(_round_robin([head_step(h, j, st, masked) for h, st in zip(heads, states)]))

        init = (jnp.full((1, T), -jnp.inf, F32), jnp.zeros((VR, T), F32))
        states = lax.fori_loop(0, n_full, lambda j, st: kv_step(j, st, False), (init,) * len(heads))
        states = kv_step(n_full, states, True)
        outs = [acc[:HEAD_DIM] / acc[HEAD_DIM:HEAD_DIM + 1] for _, acc in states]
        for n in range(0, len(heads), 2):
            pair_t = jnp.concatenate([outs[n], outs[n + 1]], axis=0)
            sl = slice((h0 + n) * HEAD_DIM, (h0 + n + 2) * HEAD_DIM)
            o_ref[:, sl] = (pair_t.T * _silu(gate_ref[:, sl].astype(F32))).astype(o_ref.dtype)


def _fox_t_attention(ka, qt, vt, proj, bsz, seq):
    T = min(FOXT_BLOCK, seq)
    nq = seq // T
    m = bsz * seq
    gw = FOXT_HEADS * HEAD_DIM
    sw = FOXT_HEADS * LANES
    n_groups = N_HEADS // FOXT_HEADS
    gate_col0 = 3 * D_MODEL // gw
    return pl.pallas_call(
        _fox_t_body,
        grid=(bsz, n_groups, nq),
        in_specs=[
            pl.BlockSpec((seq, sw), lambda b, g, i: (b, g)),
            pl.BlockSpec((sw, T), lambda b, g, i: (g, b * nq + i)),
            pl.BlockSpec((sw, seq), lambda b, g, i: (g, b)),
            pl.BlockSpec((T, gw), lambda b, g, i: (b * nq + i, gate_col0 + g)),
        ],
        out_specs=pl.BlockSpec((T, gw), lambda b, g, i: (b * nq + i, g)),
        out_shape=jax.ShapeDtypeStruct((m, D_MODEL), BF16),
        compiler_params=_params("parallel", "parallel", "arbitrary"),
        name="fox_attention",
    )(ka, qt, vt, proj)


def _fox_layer(x2, bsz, seq, norm, w_in, b_f, q_gain, k_gain, w_out):
    W = D_MODEL
    proj, f_cols, _ = _norm_matmul(x2, norm, w_in[:, :4 * W].astype(BF16), w_in[:, 4 * W:],
                                   name="fox_in_proj")
    f3 = _fox_decay(f_cols, b_f, bsz, seq)
    scale = HEAD_DIM ** -0.5 * LOG2_E
    q_gain_row = (jnp.tile(q_gain, N_HEADS) * scale).reshape(1, W).astype(F32)
    k_gain_row = jnp.tile(k_gain, N_HEADS).reshape(1, W).astype(F32)
    ka, qt, vt = _fox_prep(proj, f3, q_gain_row, k_gain_row)
    a = _fox_t_attention(ka, qt, vt, proj, bsz, seq)
    return _out_proj(a, w_out.astype(BF16), x2)


def kernel(x, rel_bias, swa_norm, swa_w_in, swa_q_gain, swa_k_gain, swa_sinks, swa_w_out, rwkv_norm, rwkv_mix, rwkv_w_in, rwkv_w0, rwkv_w_lora1, rwkv_w_lora2, rwkv_a0, rwkv_a_lora1, rwkv_a_lora2, rwkv_k_k, rwkv_k_a, rwkv_r_k, rwkv_gn_w, rwkv_gn_b, rwkv_w_out, mlstm_norm, mlstm_w_in, mlstm_b_i, mlstm_b_f, mlstm_h_gain, mlstm_w_out, fox_norm, fox_w_in, fox_b_f, fox_q_gain, fox_k_gain, fox_w_out):
    bsz, seq, d = x.shape
    depth = swa_norm.shape[0] + rwkv_norm.shape[0] + mlstm_norm.shape[0] + fox_norm.shape[0]
    x2 = x.reshape(bsz * seq, d)
    for layer in range(depth):
        kind, idx = layer % 4, layer // 4
        if kind == 0:
            x2 = _swa_layer(x2, bsz, seq, rel_bias, swa_norm[idx], swa_w_in[idx], swa_q_gain[idx],
                            swa_k_gain[idx], swa_sinks[idx], swa_w_out[idx])
        elif kind == 1:
            x2 = _rwkv_layer(x2, bsz, seq, rwkv_norm[idx], rwkv_mix[idx], rwkv_w_in[idx],
                             rwkv_w0[idx], rwkv_w_lora1[idx], rwkv_w_lora2[idx], rwkv_a0[idx],
                             rwkv_a_lora1[idx], rwkv_a_lora2[idx], rwkv_k_k[idx], rwkv_k_a[idx],
                             rwkv_r_k[idx], rwkv_gn_w[idx], rwkv_gn_b[idx], rwkv_w_out[idx])
        elif kind == 2:
            x2 = _mlstm_layer(x2, bsz, seq, mlstm_norm[idx], mlstm_w_in[idx], mlstm_b_i[idx],
                              mlstm_b_f[idx], mlstm_h_gain[idx], mlstm_w_out[idx])
        else:
            x2 = _fox_layer(x2, bsz, seq, fox_norm[idx], fox_w_in[idx], fox_b_f[idx],
                            fox_q_gain[idx], fox_k_gain[idx], fox_w_out[idx])
    return x2.reshape(bsz, seq, d)
```

```python
import functools
import math

import jax
import jax.numpy as jnp
from jax import lax
from jax.experimental import pallas as pl
from jax.experimental.pallas import tpu as pltpu

F32 = jnp.float32
BF16 = jnp.bfloat16
HIGHEST = lax.Precision.HIGHEST

D_MODEL = 2048
RMS_EPS = 1e-6
GN_EPS = 64e-5
LOG2_E = math.log2(math.e)

HEAD_DIM = 64
N_HEADS = D_MODEL // HEAD_DIM
LANES = 128
N_PAIRS = D_MODEL // LANES

SWA_KV_HEADS = 4
SWA_GROUP = N_HEADS // SWA_KV_HEADS
SWA_KV_WIDTH = SWA_KV_HEADS * HEAD_DIM
SWA_BLOCK = 128
SWA_PAIR_UNROLL = 8
REL_BUCKETS = 32
REL_MAX_DIST = 128

RWKV_CHUNK = 64
RWKV_TM = 512
RWKV_LORA_PAD = 128
RWKV_TILE_HEADS = 2
RWKV_TILE_UNROLL = 16

MLSTM_HEADS = 8
MLSTM_V_DIM = 256
MLSTM_QK_DIM = 128
MLSTM_CHUNK = 128
MLSTM_HEAD_UNROLL = 4
MLSTM_MAIN = 2 * MLSTM_HEADS * MLSTM_QK_DIM + 3 * MLSTM_HEADS * MLSTM_V_DIM

FOXT_BLOCK = 256
FOXT_HEADS = 16
FOXT_KV = 256
FOXT_UNROLL = 8
FOXT_V_ROWS = 80

VMEM_LIMIT = 56 * 1024 * 1024
ROW_TILE = 512
PROJ_TM = 1024
PROJ_TN_CHOICES = (1536, 1024, 512, 256, 128)


def _params(*sem):
    return pltpu.CompilerParams(dimension_semantics=sem, vmem_limit_bytes=VMEM_LIMIT)


def _nt(a, b, precision=None):
    return lax.dot_general(a, b, (((1,), (1,)), ((), ())), precision=precision,
                           preferred_element_type=F32)


def _tn(a, b, precision=None):
    return lax.dot_general(a, b, (((0,), (0,)), ((), ())), precision=precision,
                           preferred_element_type=F32)


def _mm(a, b, precision=None):
    return jnp.dot(a, b, precision=precision, preferred_element_type=F32)


def _rms_rows(x, g):
    return x * lax.rsqrt(jnp.mean(x * x, axis=-1, keepdims=True) + RMS_EPS) * g


def _silu(x):
    return x * jax.nn.sigmoid(x)


def _round_robin(gens):
    results = [None] * len(gens)
    live = list(range(len(gens)))
    while live:
        for n in list(live):
            try:
                next(gens[n])
            except StopIteration as done:
                results[n] = done.value
                live.remove(n)
    return results


def _norm_matmul_body(has_gates, x_ref, g_ref, w_ref, *refs):
    if has_gates:
        wc_ref, wr_ref, o_ref, cols_ref, rows_ref, xn_ref = refs
    else:
        o_ref, xn_ref = refs

    @pl.when(pl.program_id(1) == 0)
    def _():
        xn = _rms_rows(x_ref[...], g_ref[...]).astype(BF16)
        xn_ref[...] = xn
        if has_gates:
            cols_ref[...] = _mm(xn, wc_ref[...])
            rows_ref[...] = _nt(wr_ref[...], xn)

    o_ref[...] = _mm(xn_ref[...], w_ref[...]).astype(o_ref.dtype)


def _proj_tiles(m, n):
    tm = min(PROJ_TM, m)
    tn = next(t for t in PROJ_TN_CHOICES if n % t == 0)
    return tm, tn


def _norm_matmul(x, g, w, w_gates=None, *, name="norm_matmul"):
    m, k = x.shape
    n = w.shape[1]
    tm, tn = _proj_tiles(m, n)
    has_gates = w_gates is not None
    in_specs = [
        pl.BlockSpec((tm, k), lambda i, j: (i, 0)),
        pl.BlockSpec((1, k), lambda i, j: (0, 0)),
        pl.BlockSpec((k, tn), lambda i, j: (0, j)),
    ]
    out_specs = [pl.BlockSpec((tm, tn), lambda i, j: (i, j))]
    out_shape = [jax.ShapeDtypeStruct((m, n), BF16)]
    args = [x, g.reshape(1, k), w]
    if has_gates:
        n_g = w_gates.shape[1]
        in_specs += [pl.BlockSpec((k, LANES), lambda i, j: (0, 0)),
                     pl.BlockSpec((n_g, k), lambda i, j: (0, 0))]
        out_specs += [pl.BlockSpec((tm, LANES), lambda i, j: (i, 0)),
                      pl.BlockSpec((n_g, tm), lambda i, j: (0, i))]
        out_shape += [jax.ShapeDtypeStruct((m, LANES), F32), jax.ShapeDtypeStruct((n_g, m), F32)]
        args += [jnp.zeros((k, LANES), BF16).at[:, :n_g].set(w_gates.astype(BF16)),
                 w_gates.T.astype(BF16)]
    outs = pl.pallas_call(
        functools.partial(_norm_matmul_body, has_gates),
        grid=(m // tm, n // tn),
        in_specs=in_specs,
        out_specs=out_specs,
        out_shape=out_shape,
        scratch_shapes=[pltpu.VMEM((tm, k), BF16)],
        compiler_params=_params("parallel", "arbitrary"),
        name=name,
    )(*args)
    return outs if has_gates else outs[0]


def _out_proj_body(a_ref, w_ref, r_ref, o_ref):
    o_ref[...] = r_ref[...] + _mm(a_ref[...], w_ref[...])


def _out_proj(a, w, resid):
    m, k = a.shape
    n = w.shape[1]
    tm = min(ROW_TILE, m)
    return pl.pallas_call(
        _out_proj_body,
        grid=(m // tm,),
        in_specs=[
            pl.BlockSpec((tm, k), lambda i: (i, 0)),
            pl.BlockSpec((k, n), lambda i: (0, 0)),
            pl.BlockSpec((tm, n), lambda i: (i, 0)),
        ],
        out_specs=pl.BlockSpec((tm, n), lambda i: (i, 0)),
        out_shape=jax.ShapeDtypeStruct((m, n), F32),
        compiler_params=_params("parallel"),
        name="out_proj",
    )(a, w, resid)


def _t5_bucket(dist):
    max_exact = REL_BUCKETS // 2
    d_f = jnp.maximum(dist, 1).astype(F32)
    large = max_exact + (jnp.log(d_f / max_exact) / math.log(REL_MAX_DIST / max_exact)
                         * (REL_BUCKETS - max_exact)).astype(jnp.int32)
    large = jnp.minimum(large, REL_BUCKETS - 1)
    return jnp.where(dist < max_exact, dist, large)


def _swa_body(q_ref, gate_ref, kc_ref, kp_ref, vc_ref, vp_ref, bias_ref, sink_ref, qg_ref, kg_ref,
              o_ref):
    L = SWA_BLOCK
    k0 = lax.broadcasted_iota(jnp.int32, (2 * L, LANES), 1) < HEAD_DIM
    t0 = lax.broadcasted_iota(jnp.int32, (L, LANES), 1) < HEAD_DIM
    bi = lax.broadcasted_iota(jnp.int32, (LANES, LANES), 0) // HEAD_DIM
    bj = lax.broadcasted_iota(jnp.int32, (LANES, LANES), 1) // HEAD_DIM
    bd_ones = (bi == bj).astype(BF16)
    zero = jnp.zeros((2 * L, LANES), F32)
    one = jnp.ones((2 * L, LANES), F32)
    ones_bd = jnp.concatenate([jnp.where(k0, one, zero), jnp.where(k0, zero, one)], axis=0).astype(BF16)

    def head_mean_sq(z):
        return _mm((z * z).astype(BF16), bd_ones) * (1.0 / HEAD_DIM)

    def both_halves(x, head_in_low_lanes):
        if head_in_low_lanes:
            lo = jnp.where(k0, x, zero)
            hi = pltpu.roll(lo, HEAD_DIM, axis=1)
        else:
            hi = jnp.where(k0, zero, x)
            lo = pltpu.roll(hi, HEAD_DIM, axis=1)
        return jnp.concatenate([lo, hi], axis=0)

    k_all = jnp.concatenate([kp_ref[...], kc_ref[...]], axis=0).astype(F32)
    v_all = jnp.concatenate([vp_ref[...], vc_ref[...]], axis=0).astype(F32)
    kbd, rhs = [], []
    for t in range(SWA_KV_WIDTH // LANES):
        kt = k_all[:, t * LANES:(t + 1) * LANES]
        kn = kt * lax.rsqrt(head_mean_sq(kt) + RMS_EPS) * kg_ref[...]
        vt = v_all[:, t * LANES:(t + 1) * LANES]
        for low in (True, False):
            kbd.append(both_halves(kn, low).astype(BF16))
            rhs.append(jnp.concatenate([both_halves(vt, low).astype(BF16), ones_bd], axis=1))

    def pair(p):
        kh = p // (SWA_GROUP // 2)
        sl = slice(p * LANES, (p + 1) * LANES)
        q2 = q_ref[:, sl].astype(F32)
        ms = head_mean_sq(q2)
        yield
        qn = (q2 * lax.rsqrt(ms + RMS_EPS) * qg_ref[...]).astype(BF16)
        s = _nt(qn, kbd[kh]) + bias_ref[0, p]
        yield
        sink_a = sink_ref[2 * p]
        sink_b = sink_ref[2 * p + 1]
        m_a = jnp.maximum(jnp.max(s[:, :2 * L], axis=-1, keepdims=True), sink_a)
        m_b = jnp.maximum(jnp.max(s[:, 2 * L:], axis=-1, keepdims=True), sink_b)
        pm = jnp.concatenate([jnp.exp(s[:, :2 * L] - m_a), jnp.exp(s[:, 2 * L:] - m_b)], axis=1)
        pv = _mm(pm.astype(BF16), rhs[kh])
        yield
        sink_t = jnp.where(t0, jnp.exp(sink_a - m_a), jnp.exp(sink_b - m_b))
        o = pv[:, :LANES] / (pv[:, LANES:] + sink_t)
        return (o * _silu(gate_ref[:, sl].astype(F32))).astype(o_ref.dtype)

    for g0 in range(0, N_PAIRS, SWA_PAIR_UNROLL):
        outs = _round_robin([pair(p) for p in range(g0, g0 + SWA_PAIR_UNROLL)])
        for n, o in enumerate(outs):
            o_ref[:, (g0 + n) * LANES:(g0 + n + 1) * LANES] = o


def _swa_attention(proj, bias, sinks, q_gain2, k_gain2, bsz, seq):
    L = SWA_BLOCK
    nb = seq // L
    m = bsz * seq
    kcol = 2 * D_MODEL // SWA_KV_WIDTH
    prev = lambda r: jnp.where(r % nb == 0, r, r - 1)
    return pl.pallas_call(
        _swa_body,
        grid=(bsz * nb,),
        in_specs=[
            pl.BlockSpec((L, D_MODEL), lambda r: (r, 0)),
            pl.BlockSpec((L, D_MODEL), lambda r: (r, 1)),
            pl.BlockSpec((L, SWA_KV_WIDTH), lambda r: (r, kcol)),
            pl.BlockSpec((L, SWA_KV_WIDTH), lambda r: (prev(r), kcol)),
            pl.BlockSpec((L, SWA_KV_WIDTH), lambda r: (r, kcol + 1)),
            pl.BlockSpec((L, SWA_KV_WIDTH), lambda r: (prev(r), kcol + 1)),
            pl.BlockSpec((1, N_PAIRS, L, 4 * L), lambda r: (jnp.where(r % nb == 0, 0, 1), 0, 0, 0)),
            pl.BlockSpec(memory_space=pltpu.SMEM),
            pl.BlockSpec((1, LANES), lambda r: (0, 0)),
            pl.BlockSpec((1, LANES), lambda r: (0, 0)),
        ],
        out_specs=pl.BlockSpec((L, D_MODEL), lambda r: (r, 0)),
        out_shape=jax.ShapeDtypeStruct((m, D_MODEL), BF16),
        compiler_params=_params("parallel"),
        name="swa_attention",
    )(proj, proj, proj, proj, proj, proj, bias, sinks, q_gain2, k_gain2)


def _swa_bias_tables(rel_bias):
    L = SWA_BLOCK
    qi = jnp.arange(L)[:, None]
    kj = jnp.arange(2 * L)[None, :]
    dist = qi + L - kj
    in_band = (dist >= 0) & (dist < L)
    onehot = (_t5_bucket(jnp.maximum(dist, 0))[..., None] == jnp.arange(REL_BUCKETS)).astype(F32)
    bias = jnp.einsum("ijb,bh->hij", onehot, rel_bias.astype(F32), precision=HIGHEST)
    general = jnp.where(in_band[None], bias, -jnp.inf)
    first = jnp.where((in_band & (kj >= L))[None], bias, -jnp.inf)
    pairs = lambda t: t.reshape(N_PAIRS, 2, L, 2 * L).transpose(0, 2, 1, 3).reshape(N_PAIRS, L, 4 * L)
    return jnp.stack([pairs(first), pairs(general)])


def _swa_layer(x2, bsz, seq, rel_bias, norm, w_in, q_gain, k_gain, sinks, w_out):
    W = D_MODEL
    kvw = SWA_KV_WIDTH
    w_perm = jnp.concatenate([w_in[:, :W], w_in[:, W + 2 * kvw:], w_in[:, W:W + 2 * kvw]], axis=1)
    proj = _norm_matmul(x2, norm, w_perm.astype(BF16), name="swa_in_proj")
    q_gain2 = (jnp.tile(q_gain, 2) * HEAD_DIM ** -0.5).reshape(1, LANES).astype(F32)
    k_gain2 = jnp.tile(k_gain, 2).reshape(1, LANES).astype(F32)
    a = _swa_attention(proj, _swa_bias_tables(rel_bias), sinks.astype(F32), q_gain2, k_gain2, bsz, seq)
    return _out_proj(a, w_out.astype(BF16), x2)


def _shifted_mix(x_ref, xp_ref, g_ref, seq_start):
    h = _rms_rows(x_ref[...], g_ref[...])
    h_prev_row = _rms_rows(xp_ref[7:8, :], g_ref[...])
    h_prev_row = jnp.where(seq_start, 0.0, h_prev_row)
    rolled = pltpu.roll(h, 1, axis=0)
    row = lax.broadcasted_iota(jnp.int32, h.shape, 0)
    shifted = jnp.where(row == 0, h_prev_row, rolled)
    return h, shifted - h


def _rwkv_in_body(tiles_per_seq, x_ref, xp_ref, g_ref, mix_ref, w_ref, o_ref, xm_ref):
    i = pl.program_id(0)

    @pl.when((pl.program_id(1) == 0) & (pl.program_id(2) == 0))
    def _():
        h, xx = _shifted_mix(x_ref, xp_ref, g_ref, i % tiles_per_seq == 0)
        for c in range(4):
            xm_ref[c] = (h + xx * mix_ref[c:c + 1, :]).astype(BF16)

    c = pl.program_id(1)
    o_ref[0] = _mm(xm_ref[c], w_ref[0]).astype(o_ref.dtype)


def _rwkv_in_proj(x2, seq, norm, mix8, w4):
    m, k = x2.shape
    tm = min(RWKV_TM, seq)
    tn = k
    rows8 = tm // 8
    return pl.pallas_call(
        functools.partial(_rwkv_in_body, seq // tm),
        grid=(m // tm, 4, k // tn),
        in_specs=[
            pl.BlockSpec((tm, k), lambda i, c, j: (i, 0)),
            pl.BlockSpec((8, k), lambda i, c, j: (jnp.maximum(i * rows8 - 1, 0), 0)),
            pl.BlockSpec((1, k), lambda i, c, j: (0, 0)),
            pl.BlockSpec((8, k), lambda i, c, j: (0, 0)),
            pl.BlockSpec((1, k, tn), lambda i, c, j: (c, 0, j)),
        ],
        out_specs=pl.BlockSpec((1, tm, tn), lambda i, c, j: (c, i, j)),
        out_shape=jax.ShapeDtypeStruct((4, m, k), BF16),
        scratch_shapes=[pltpu.VMEM((4, tm, k), BF16)],
        compiler_params=_params("parallel", "arbitrary", "arbitrary"),
        name="rwkv_in_proj",
    )(x2, x2, norm.reshape(1, k), mix8, w4)


def _rwkv_lora_body(tiles_per_seq, x_ref, xp_ref, g_ref, mix_ref, w1_ref, w2_ref, a1_ref, a2_ref,
                    w0_ref, a0_ref, lw_ref, a_ref):
    i = pl.program_id(0)
    h, xx = _shifted_mix(x_ref, xp_ref, g_ref, i % tiles_per_seq == 0)
    xw = (h + xx * mix_ref[4:5, :]).astype(BF16)
    xa = (h + xx * mix_ref[5:6, :]).astype(BF16)
    t = jnp.tanh(_mm(xw, w1_ref[...])).astype(BF16)
    w_pre = w0_ref[...] + _mm(t, w2_ref[...])
    lw_ref[...] = -jnp.exp(-jax.nn.softplus(-w_pre) - 0.5)
    u = _mm(xa, a1_ref[...]).astype(BF16)
    a_ref[...] = jax.nn.sigmoid(a0_ref[...] + _mm(u, a2_ref[...])).astype(a_ref.dtype)


def _rwkv_lora(x2, seq, norm, mix8, w1, w2, a1, a2, w0, a0):
    m, k = x2.shape
    tm = min(RWKV_TM, seq)
    rows8 = tm // 8
    P = RWKV_LORA_PAD
    full = lambda shape: pl.BlockSpec(shape, lambda i: tuple(0 for _ in shape))
    return pl.pallas_call(
        functools.partial(_rwkv_lora_body, seq // tm),
        grid=(m // tm,),
        in_specs=[
            pl.BlockSpec((tm, k), lambda i: (i, 0)),
            pl.BlockSpec((8, k), lambda i: (jnp.maximum(i * rows8 - 1, 0), 0)),
            full((1, k)), full((8, k)),
            full((k, P)), full((P, k)), full((k, P)), full((P, k)),
            full((1, k)), full((1, k)),
        ],
        out_specs=[pl.BlockSpec((tm, k), lambda i: (i, 0)), pl.BlockSpec((tm, k), lambda i: (i, 0))],
        out_shape=[jax.ShapeDtypeStruct((m, k), F32), jax.ShapeDtypeStruct((m, k), BF16)],
        compiler_params=_params("parallel"),
        name="rwkv_lora",
    )(x2, x2, norm.reshape(1, k), mix8, w1, w2, a1, a2, w0.reshape(1, k), a0.reshape(1, k))


def _rwkv_scan_body(r_ref, k_ref, v_ref, g_ref, lw_ref, a_ref, kk_ref, ka_ref, rk_ref,
                    gnw_ref, gnb_ref, o_ref, ht_ref):
    C = RWKV_CHUNK
    G = RWKV_TILE_HEADS
    W = G * HEAD_DIM
    assert C == HEAD_DIM

    @pl.when(pl.program_id(1) == 0)
    def _():
        ht_ref[...] = jnp.zeros_like(ht_ref)

    lane = lax.broadcasted_iota(jnp.int32, (C, W), 1)
    tok = lax.broadcasted_iota(jnp.int32, (C, W), 0)
    col = lane % HEAD_DIM
    strict = tok > col
    incl = tok >= col
    lane_b = lane.astype(F32).astype(BF16)
    head_masks = [(lane_b >= n * HEAD_DIM) & (lane_b < (n + 1) * HEAD_DIM) for n in range(G)]
    ti = lax.broadcasted_iota(jnp.int32, (C, C), 0)
    tj = lax.broadcasted_iota(jnp.int32, (C, C), 1)
    tri = (ti >= tj).astype(F32)
    bi = lax.broadcasted_iota(jnp.int32, (W, W), 0) // HEAD_DIM
    bj = lax.broadcasted_iota(jnp.int32, (W, W), 1) // HEAD_DIM
    bd = bi == bj
    bd_ones = bd.astype(BF16)
    tri2 = jnp.concatenate([tri, tri], axis=1).astype(BF16)

    def head_sum(z):
        return _mm(z.astype(BF16), bd_ones)

    def stack(z):
        zb = z.astype(BF16)
        zero = jnp.zeros_like(zb)
        return jnp.concatenate([jnp.where(hm, zb, zero) for hm in head_masks], axis=0)

    def load(p):
        sl = pl.ds(pl.multiple_of(p * W, W), W)
        f32 = lambda ref: ref[0, :, sl].astype(F32)
        return (f32(r_ref), f32(k_ref), f32(v_ref), f32(g_ref), lw_ref[:, sl],
                a_ref[:, sl].astype(F32), kk_ref[:, sl], ka_ref[:, sl], rk_ref[:, sl], gnw_ref[:, sl],
                gnb_ref[:, sl], ht_ref[p])

    def compute(r, k, v, g, lw, a, k_k, k_a, r_k, gn_w, gn_b, ht):
        kkp = k * k_k
        kp = k * (1.0 + (a - 1.0) * k_a)
        sums = head_sum(jnp.concatenate([kkp * kkp, r * kp * r_k], axis=0))
        ss, bonus_dot = sums[:C], sums[C:]
        lw_hi = lw.astype(BF16)
        lw_lo = (lw - lw_hi.astype(F32)).astype(BF16)
        cum = _mm(tri2, jnp.concatenate([lw_hi, lw_lo], axis=0))
        yield
        kk = kkp / jnp.maximum(jnp.sqrt(ss), 1e-12)
        cum_end = cum[C - 1:C, :]
        g_in = jnp.exp(cum)
        g_prev = jnp.exp(cum - lw)
        g_inv = jnp.exp(-cum)
        g_end = jnp.exp(cum_end - cum)
        kka = kk * a
        a_t = -kk * g_prev
        r_t = r * g_in
        b_t = kka * g_inv
        k_t = kp * g_inv

        x1 = jnp.concatenate([a_t, r_t], axis=0).astype(BF16)
        gbkh = _nt(x1, jnp.concatenate([stack(b_t), stack(k_t), ht.astype(BF16)], axis=0))
        gbk, xh = gbkh[:, :2 * W], gbkh[:, 2 * W:]
        yield
        zero = jnp.zeros((C, W), F32)
        l_ab = jnp.where(strict, gbk[:C, :W], zero)
        l_ak = jnp.where(strict, gbk[:C, W:], zero)
        p_rb = jnp.where(incl, gbk[C:, :W], zero)
        p_rk = jnp.where(incl, gbk[C:, W:], zero)

        t_inv = jnp.where(tok == col, 1.0, zero) + jnp.where(
            (tok // 2 == col // 2) & (tok % 2 == 1) & (col % 2 == 0), l_ab, zero)
        v2 = stack(v)
        rhs = xh[:C] + _mm(l_ak.astype(BF16), v2)
        s = 2
        while s < C:
            sel = (tok // (2 * s) == col // (2 * s)) & (tok % (2 * s) >= s) & (col % (2 * s) < s)
            l_s = jnp.where(sel, l_ab, zero)
            tl = _mm(t_inv.astype(BF16), stack(l_s))
            yield
            t_inv = t_inv + _mm(tl.astype(BF16), stack(t_inv))
            yield
            s *= 2

        u = _mm(t_inv.astype(BF16), stack(rhs))
        yield
        p_cat = jnp.concatenate([p_rb, p_rk], axis=1).astype(BF16)
        y = xh[C:] + _mm(p_cat, jnp.concatenate([stack(u), v2], axis=0))
        uv = jnp.concatenate([u, v], axis=0).astype(BF16)
        bk_end = jnp.concatenate([kka * g_end, kp * g_end], axis=0).astype(BF16)
        upd = _tn(uv, bk_end)
        yield
        ht_new = ht * jnp.exp(cum_end) + jnp.where(bd, upd, 0.0)
        mu = head_sum(y) * (1.0 / HEAD_DIM)
        yield
        dy = y - mu
        var = head_sum(dy * dy) * (1.0 / HEAD_DIM)
        yield
        yn = dy * lax.rsqrt(var + GN_EPS) * gn_w + gn_b
        return ((yn + bonus_dot * v) * _silu(g)).astype(o_ref.dtype), ht_new

    def group(gi, carry):
        base = gi * RWKV_TILE_UNROLL
        ins = [load(base + n) for n in range(RWKV_TILE_UNROLL)]
        outs = _round_robin([compute(*x) for x in ins])
        for n, (o, ht_new) in enumerate(outs):
            p = base + n
            o_ref[:, pl.ds(pl.multiple_of(p * W, W), W)] = o
            ht_ref[p] = ht_new
        return carry

    lax.fori_loop(0, D_MODEL // W // RWKV_TILE_UNROLL, group, 0)


def _rwkv_scan(rkvg, lw, a, k_k, k_a, r_k, gn_w, gn_b, bsz, seq):
    C = RWKV_CHUNK
    nc = seq // C
    m = bsz * seq
    k = D_MODEL
    tile_w = RWKV_TILE_HEADS * HEAD_DIM
    part = lambda c: pl.BlockSpec((1, C, k), lambda b, j, c=c: (c, b * nc + j, 0))
    tile = pl.BlockSpec((C, k), lambda b, j: (b * nc + j, 0))
    vec = pl.BlockSpec((1, k), lambda b, j: (0, 0))
    return pl.pallas_call(
        _rwkv_scan_body,
        grid=(bsz, nc),
        in_specs=[part(0), part(1), part(2), part(3), tile, tile, vec, vec, vec, vec, vec],
        out_specs=tile,
        out_shape=jax.ShapeDtypeStruct((m, k), BF16),
        scratch_shapes=[pltpu.VMEM((k // tile_w, tile_w, tile_w), F32)],
        compiler_params=_params("parallel", "arbitrary"),
        name="rwkv_scan",
    )(rkvg, rkvg, rkvg, rkvg, lw, a, k_k.reshape(1, k), k_a.reshape(1, k), r_k.reshape(1, k),
      gn_w.reshape(1, k), gn_b.reshape(1, k))


def _rwkv_layer(x2, bsz, seq, norm, mix, w_in, w0, w_lora1, w_lora2, a0, a_lora1, a_lora2,
                k_k, k_a, r_k, gn_w, gn_b, w_out):
    k = D_MODEL
    P = RWKV_LORA_PAD
    mix8 = jnp.zeros((8, k), F32).at[:mix.shape[0]].set(mix)
    pad_cols = lambda w: jnp.zeros((k, P), BF16).at[:, :w.shape[1]].set(w.astype(BF16))
    pad_rows = lambda w: jnp.zeros((P, k), BF16).at[:w.shape[0]].set(w.astype(BF16))
    rkvg = _rwkv_in_proj(x2, seq, norm, mix8, w_in.astype(BF16))
    lw, a = _rwkv_lora(x2, seq, norm, mix8, pad_cols(w_lora1), pad_rows(w_lora2),
                       pad_cols(a_lora1), pad_rows(a_lora2), w0, a0)
    y = _rwkv_scan(rkvg, lw, a, k_k, k_a, r_k, gn_w, gn_b, bsz, seq)
    return _out_proj(y, w_out.astype(BF16), x2)


def _mlstm_body(q_ref, k_ref, v_ref, op_ref, gate_ref, gc_ref, gr_ref, bi_ref, bf_ref, hg_ref,
                o_ref, ct_ref, m_ref):
    L = MLSTM_CHUNK
    H = MLSTM_HEADS
    dk = MLSTM_QK_DIM
    dv = MLSTM_V_DIM

    @pl.when(pl.program_id(1) == 0)
    def _():
        ct_ref[...] = jnp.zeros_like(ct_ref)
        m_ref[...] = jnp.zeros_like(m_ref)

    ti = lax.broadcasted_iota(jnp.int32, (L, L), 0)
    tj = lax.broadcasted_iota(jnp.int32, (L, L), 1)
    tril = ti >= tj
    tri_f = tril.astype(F32)
    gc = gc_ref[...] + bi_ref[...]
    gr = gr_ref[...] + bf_ref[...]
    lane = lax.broadcasted_iota(jnp.int32, gc.shape, 1)
    f_cols = jnp.where((lane >= H) & (lane < 2 * H), jax.nn.log_sigmoid(gc), 0.0)
    b_cols = _mm(tri_f, f_cols, HIGHEST)
    f_rows = jax.nn.log_sigmoid(gr)
    b_rows = _nt(f_rows, tri_f, HIGHEST)
    ones_blk = jnp.ones((L, LANES), BF16)

    def head(h):
        q = q_ref[:, h * dk:(h + 1) * dk]
        k = k_ref[:, h * dk:(h + 1) * dk]
        v = v_ref[:, h * dv:(h + 1) * dv]
        v_aug = jnp.concatenate([v, ones_blk], axis=-1)
        ct = ct_ref[h]
        m_prev = m_ref[h][:, :1]
        qk = _nt(q, k)
        q_ct = _mm(q, ct.astype(BF16))
        yield
        i_col = gc[:, h:h + 1]
        i_row = gr[h:h + 1, :]
        b_col = b_cols[:, H + h:H + h + 1]
        b_row = b_rows[H + h:H + h + 1, :]
        g_tot = b_col[L - 1:L, :]
        dmat = jnp.where(tril, b_col - b_row + i_row, -jnp.inf)
        inter = b_col + m_prev
        m_t = jnp.maximum(inter, jnp.max(dmat, axis=-1, keepdims=True))
        w_intra = jnp.exp(dmat - m_t) * dk ** -0.5
        w_inter = jnp.exp(inter - m_t) * dk ** -0.5
        sw = qk * w_intra
        num = w_inter * q_ct + _mm(sw.astype(BF16), v_aug)
        decay = g_tot - b_col + i_col
        m_new = jnp.maximum(g_tot + m_prev, jnp.max(decay, axis=0, keepdims=True))
        ws = jnp.exp(decay - m_new)
        carry_scale = jnp.exp(g_tot + m_prev - m_new)
        wv = (ws * v_aug.astype(F32)).astype(BF16)
        upd = _tn(k, wv)
        yield
        den = num[:, dv:dv + 1]
        h_t = num[:, :dv] / jnp.maximum(jnp.abs(den), jnp.exp(-m_t))
        hn = _rms_rows(h_t, hg_ref[:, h * dv:(h + 1) * dv])
        osl = slice(h * dv, (h + 1) * dv)
        out = hn * jax.nn.sigmoid(op_ref[:, osl].astype(F32)) * _silu(gate_ref[:, osl].astype(F32))
        return out.astype(o_ref.dtype), carry_scale * ct + upd, jnp.broadcast_to(m_new, (1, LANES))

    for h0 in range(0, H, MLSTM_HEAD_UNROLL):
        heads = range(h0, h0 + MLSTM_HEAD_UNROLL)
        for h, (out, ct_new, m_new) in zip(heads, _round_robin([head(h) for h in heads])):
            o_ref[:, h * dv:(h + 1) * dv] = out
            ct_ref[h] = ct_new
            m_ref[h] = m_new


def _mlstm_scan(proj, g_cols, g_rows, b_i, b_f, h_gain, bsz, seq):
    L = MLSTM_CHUNK
    H = MLSTM_HEADS
    nc = seq // L
    m = bsz * seq
    qkw = H * MLSTM_QK_DIM
    vw = H * MLSTM_V_DIM
    bias = jnp.concatenate([b_i, b_f]).astype(F32)
    bias_row = jnp.zeros((1, LANES), F32).at[0, :2 * H].set(bias)
    bias_col = bias.reshape(2 * H, 1)
    row = lambda b, j: b * nc + j
    return pl.pallas_call(
        _mlstm_body,
        grid=(bsz, nc),
        in_specs=[
            pl.BlockSpec((L, qkw), lambda b, j: (row(b, j), 0)),
            pl.BlockSpec((L, qkw), lambda b, j: (row(b, j), 1)),
            pl.BlockSpec((L, vw), lambda b, j: (row(b, j), 1)),
            pl.BlockSpec((L, vw), lambda b, j: (row(b, j), 2)),
            pl.BlockSpec((L, vw), lambda b, j: (row(b, j), 3)),
            pl.BlockSpec((L, LANES), lambda b, j: (row(b, j), 0)),
            pl.BlockSpec((2 * H, L), lambda b, j: (0, row(b, j))),
            pl.BlockSpec((1, LANES), lambda b, j: (0, 0)),
            pl.BlockSpec((2 * H, 1), lambda b, j: (0, 0)),
            pl.BlockSpec((1, vw), lambda b, j: (0, 0)),
        ],
        out_specs=pl.BlockSpec((L, vw), lambda b, j: (row(b, j), 0)),
        out_shape=jax.ShapeDtypeStruct((m, vw), BF16),
        scratch_shapes=[
            pltpu.VMEM((H, MLSTM_QK_DIM, MLSTM_V_DIM + LANES), F32),
            pltpu.VMEM((H, 1, LANES), F32),
        ],
        compiler_params=_params("parallel", "arbitrary"),
        name="mlstm_scan",
    )(proj, proj, proj, proj, proj, g_cols, g_rows, bias_row, bias_col, h_gain.reshape(1, vw))


def _mlstm_layer(x2, bsz, seq, norm, w_in, b_i, b_f, h_gain, w_out):
    proj, g_cols, g_rows = _norm_matmul(x2, norm, w_in[:, :MLSTM_MAIN].astype(BF16),
                                        w_in[:, MLSTM_MAIN:], name="mlstm_in_proj")
    a = _mlstm_scan(proj, g_cols, g_rows, b_i, b_f, h_gain, bsz, seq)
    return _out_proj(a, w_out.astype(BF16), x2)


def _fox_decay_body(f_ref, bf_ref, o_ref):
    B = LANES
    seq = f_ref.shape[0]
    lf = jax.nn.log_sigmoid(f_ref[...] + bf_ref[...])
    ti = lax.broadcasted_iota(jnp.int32, (B, B), 0)
    tj = lax.broadcasted_iota(jnp.int32, (B, B), 1)
    tril = (ti >= tj).astype(F32)
    lane = lax.broadcasted_iota(jnp.int32, (B, LANES), 1)
    carry = jnp.zeros((1, LANES), F32)
    for c in range(seq // B):
        blk = _mm(tril, lf[c * B:(c + 1) * B, :], HIGHEST) + carry
        carry = blk[B - 1:B, :]
        neg = blk * (-LOG2_E)
        hi = neg.astype(BF16).astype(F32)
        r1 = neg - hi
        mid = r1.astype(BF16).astype(F32)
        lo = r1 - mid
        packed = jnp.where(lane < N_HEADS, hi,
                           jnp.where(lane < 2 * N_HEADS, pltpu.roll(mid, N_HEADS, axis=1),
                                     pltpu.roll(lo, 2 * N_HEADS, axis=1)))
        o_ref[c * B:(c + 1) * B, :] = packed.astype(BF16)


def _fox_decay(f_cols, b_f, bsz, seq):
    bias = jnp.zeros((1, LANES), F32).at[0, :N_HEADS].set(b_f.astype(F32))
    return pl.pallas_call(
        _fox_decay_body,
        grid=(bsz,),
        in_specs=[pl.BlockSpec((seq, LANES), lambda b: (b, 0)), pl.BlockSpec((1, LANES), lambda b: (0, 0))],
        out_specs=pl.BlockSpec((seq, LANES), lambda b: (b, 0)),
        out_shape=jax.ShapeDtypeStruct((bsz * seq, LANES), BF16),
        compiler_params=_params("parallel"),
        name="fox_decay",
    )(f_cols, bias)


def _fox_prep_body(q_ref, k_ref, v_ref, f3_ref, qg_ref, kg_ref, ka_ref, qt_ref, vt_ref):
    S = 2 * LANES
    r = lax.broadcasted_iota(jnp.int32, (LANES, S), 0)
    c = lax.broadcasted_iota(jnp.int32, (LANES, S), 1)
    spread = ((r < HEAD_DIM) & (c == r)) | ((r >= HEAD_DIM) & (c == r + HEAD_DIM))
    spread = spread.astype(BF16)
    rt = lax.broadcasted_iota(jnp.int32, (S, LANES), 0)
    ct = lax.broadcasted_iota(jnp.int32, (S, LANES), 1)
    spread_t = (((ct < HEAD_DIM) & (rt == ct)) | ((ct >= HEAD_DIM) & (rt == ct + HEAD_DIM))).astype(BF16)
    bi = lax.broadcasted_iota(jnp.int32, (LANES, LANES), 0) // HEAD_DIM
    bj = lax.broadcasted_iota(jnp.int32, (LANES, LANES), 1) // HEAD_DIM
    bd_mean = jnp.where(bi == bj, 1.0 / HEAD_DIM, 0.0).astype(BF16)
    row = lax.broadcasted_iota(jnp.int32, (S, 1), 0) % LANES
    q_ones = ((row >= HEAD_DIM) & (row < HEAD_DIM + 3)).astype(F32)
    v_ones = (row == HEAD_DIM).astype(F32)
    fr = lax.broadcasted_iota(jnp.int32, (LANES, S), 0)
    fc = lax.broadcasted_iota(jnp.int32, (LANES, S), 1)
    term = fr // N_HEADS

    def head_norm(x, g):
        ms = _mm((x * x).astype(BF16), bd_mean)
        return (x * lax.rsqrt(ms + RMS_EPS) * g).astype(BF16)

    f3 = f3_ref[...]
    for p in range(N_PAIRS):
        sl = slice(p * LANES, (p + 1) * LANES)
        so = slice(p * S, (p + 1) * S)
        qn = head_norm(q_ref[:, sl].astype(F32), qg_ref[:, sl])
        kn = head_norm(k_ref[:, sl].astype(F32), kg_ref[:, sl])
        pick = (term < 3) & (((fr % N_HEADS == 2 * p) & (fc == HEAD_DIM + term)) |
                             ((fr % N_HEADS == 2 * p + 1) & (fc == LANES + HEAD_DIM + term)))
        ka_ref[:, so] = (_mm(kn, spread) + _mm(f3, pick.astype(BF16))).astype(BF16)
        qt_ref[so, :] = (_nt(spread_t, qn) + q_ones).astype(BF16)
        vt_ref[so, :] = (_nt(spread_t, v_ref[:, sl]) + v_ones).astype(BF16)


def _fox_prep(proj, f3, q_gain_row, k_gain_row):
    m = proj.shape[0]
    tm = min(ROW_TILE, m)
    wide = N_HEADS * LANES
    return pl.pallas_call(
        _fox_prep_body,
        grid=(m // tm,),
        in_specs=[
            pl.BlockSpec((tm, D_MODEL), lambda i: (i, 0)),
            pl.BlockSpec((tm, D_MODEL), lambda i: (i, 1)),
            pl.BlockSpec((tm, D_MODEL), lambda i: (i, 2)),
            pl.BlockSpec((tm, LANES), lambda i: (i, 0)),
            pl.BlockSpec((1, D_MODEL), lambda i: (0, 0)),
            pl.BlockSpec((1, D_MODEL), lambda i: (0, 0)),
        ],
        out_specs=[
            pl.BlockSpec((tm, wide), lambda i: (i, 0)),
            pl.BlockSpec((wide, tm), lambda i: (0, i)),
            pl.BlockSpec((wide, tm), lambda i: (0, i)),
        ],
        out_shape=[
            jax.ShapeDtypeStruct((m, wide), BF16),
            jax.ShapeDtypeStruct((wide, m), BF16),
            jax.ShapeDtypeStruct((wide, m), BF16),
        ],
        compiler_params=_params("parallel"),
        name="fox_prep",
    )(proj, proj, proj, f3, q_gain_row, k_gain_row)


def _fox_t_body(ka_ref, qt_ref, vt_ref, gate_ref, o_ref):
    T = qt_ref.shape[1]
    KV = min(FOXT_KV, ka_ref.shape[0])
    assert KV % T == 0
    i = pl.program_id(2)
    n_full = (i * T) // KV
    k_off = lax.broadcasted_iota(jnp.int32, (KV, T), 0)
    q_pos = i * T + lax.broadcasted_iota(jnp.int32, (KV, T), 1)
    VR = FOXT_V_ROWS

    def head_step(h, j, state, masked):
        m, acc = state
        ks = pl.ds(pl.multiple_of(j * KV, KV), KV)
        hs = slice(h * LANES, (h + 1) * LANES)
        s = _mm(ka_ref[ks, hs], qt_ref[hs, :])
        if masked:
            s = jnp.where(j * KV + k_off <= q_pos, s, -jnp.inf)
        yield
        m_new = jnp.maximum(m, jnp.max(s, axis=0, keepdims=True))
        pt = jnp.exp2(s - m_new).astype(BF16)
        pv = _mm(vt_ref[h * LANES:h * LANES + VR, ks], pt)
        yield
        return m_new, acc * jnp.exp2(m - m_new) + pv

    for h0 in range(0, FOXT_HEADS, FOXT_UNROLL):
        heads = list(range(h0, h0 + FOXT_UNROLL))

        def kv_step(j, states, masked):
            return tuple(_round_robin([head_step(h, j, st, masked) for h, st in zip(heads, states)]))

        init = (jnp.full((1, T), -jnp.inf, F32), jnp.zeros((VR, T), F32))
        states = lax.fori_loop(0, n_full, lambda j, st: kv_step(j, st, False), (init,) * len(heads))
        states = kv_step(n_full, states, True)
        outs = [acc[:HEAD_DIM] / acc[HEAD_DIM:HEAD_DIM + 1] for _, acc in states]
        for n in range(0, len(heads), 2):
            pair_t = jnp.concatenate([outs[n], outs[n + 1]], axis=0)
            sl = slice((h0 + n) * HEAD_DIM, (h0 + n + 2) * HEAD_DIM)
            o_ref[:, sl] = (pair_t.T * _silu(gate_ref[:, sl].astype(F32))).astype(o_ref.dtype)


def _fox_t_attention(ka, qt, vt, proj, bsz, seq):
    T = min(FOXT_BLOCK, seq)
    nq = seq // T
    m = bsz * seq
    gw = FOXT_HEADS * HEAD_DIM
    sw = FOXT_HEADS * LANES
    n_groups = N_HEADS // FOXT_HEADS
    gate_col0 = 3 * D_MODEL // gw
    return pl.pallas_call(
        _fox_t_body,
        grid=(bsz, n_groups, nq),
        in_specs=[
            pl.BlockSpec((seq, sw), lambda b, g, i: (b, g)),
            pl.BlockSpec((sw, T), lambda b, g, i: (g, b * nq + i)),
            pl.BlockSpec((sw, seq), lambda b, g, i: (g, b)),
            pl.BlockSpec((T, gw), lambda b, g, i: (b * nq + i, gate_col0 + g)),
        ],
        out_specs=pl.BlockSpec((T, gw), lambda b, g, i: (b * nq + i, g)),
        out_shape=jax.ShapeDtypeStruct((m, D_MODEL), BF16),
        compiler_params=_params("parallel", "parallel", "arbitrary"),
        name="fox_attention",
    )(ka, qt, vt, proj)


def _fox_layer(x2, bsz, seq, norm, w_in, b_f, q_gain, k_gain, w_out):
    W = D_MODEL
    proj, f_cols, _ = _norm_matmul(x2, norm, w_in[:, :4 * W].astype(BF16), w_in[:, 4 * W:],
                                   name="fox_in_proj")
    f3 = _fox_decay(f_cols, b_f, bsz, seq)
    scale = HEAD_DIM ** -0.5 * LOG2_E
    q_gain_row = (jnp.tile(q_gain, N_HEADS) * scale).reshape(1, W).astype(F32)
    k_gain_row = jnp.tile(k_gain, N_HEADS).reshape(1, W).astype(F32)
    ka, qt, vt = _fox_prep(proj, f3, q_gain_row, k_gain_row)
    a = _fox_t_attention(ka, qt, vt, proj, bsz, seq)
    return _out_proj(a, w_out.astype(BF16), x2)


def kernel(x, rel_bias, swa_norm, swa_w_in, swa_q_gain, swa_k_gain, swa_sinks, swa_w_out, rwkv_norm, rwkv_mix, rwkv_w_in, rwkv_w0, rwkv_w_lora1, rwkv_w_lora2, rwkv_a0, rwkv_a_lora1, rwkv_a_lora2, rwkv_k_k, rwkv_k_a, rwkv_r_k, rwkv_gn_w, rwkv_gn_b, rwkv_w_out, mlstm_norm, mlstm_w_in, mlstm_b_i, mlstm_b_f, mlstm_h_gain, mlstm_w_out, fox_norm, fox_w_in, fox_b_f, fox_q_gain, fox_k_gain, fox_w_out):
    bsz, seq, d = x.shape
    depth = swa_norm.shape[0] + rwkv_norm.shape[0] + mlstm_norm.shape[0] + fox_norm.shape[0]
    x2 = x.reshape(bsz * seq, d)
    for layer in range(depth):
        kind, idx = layer % 4, layer // 4
        if kind == 0:
            x2 = _swa_layer(x2, bsz, seq, rel_bias, swa_norm[idx], swa_w_in[idx], swa_q_gain[idx],
                            swa_k_gain[idx], swa_sinks[idx], swa_w_out[idx])
        elif kind == 1:
            x2 = _rwkv_layer(x2, bsz, seq, rwkv_norm[idx], rwkv_mix[idx], rwkv_w_in[idx],
                             rwkv_w0[idx], rwkv_w_lora1[idx], rwkv_w_lora2[idx], rwkv_a0[idx],
                             rwkv_a_lora1[idx], rwkv_a_lora2[idx], rwkv_k_k[idx], rwkv_k_a[idx],
                             rwkv_r_k[idx], rwkv_gn_w[idx], rwkv_gn_b[idx], rwkv_w_out[idx])
        elif kind == 2:
            x2 = _mlstm_layer(x2, bsz, seq, mlstm_norm[idx], mlstm_w_in[idx], mlstm_b_i[idx],
                              mlstm_b_f[idx], mlstm_h_gain[idx], mlstm_w_out[idx])
        else:
            x2 = _fox_layer(x2, bsz, seq, fox_norm[idx], fox_w_in[idx], fox_b_f[idx],
                            fox_q_gain[idx], fox_k_gain[idx], fox_w_out[idx])
    return x2.reshape(bsz, seq, d)
```

```python
import functools
import math

import jax
import jax.numpy as jnp
from jax import lax
from jax.experimental import pallas as pl
from jax.experimental.pallas import tpu as pltpu

F32 = jnp.float32
BF16 = jnp.bfloat16
HIGHEST = lax.Precision.HIGHEST

D_MODEL = 2048
RMS_EPS = 1e-6
GN_EPS = 64e-5
LOG2_E = math.log2(math.e)

HEAD_DIM = 64
N_HEADS = D_MODEL // HEAD_DIM
LANES = 128
N_PAIRS = D_MODEL // LANES

SWA_KV_HEADS = 4
SWA_GROUP = N_HEADS // SWA_KV_HEADS
SWA_KV_WIDTH = SWA_KV_HEADS * HEAD_DIM
SWA_BLOCK = 128
SWA_PAIR_UNROLL = 8
REL_BUCKETS = 32
REL_MAX_DIST = 128

RWKV_CHUNK = 64
RWKV_TM = 512
RWKV_LORA_PAD = 128
RWKV_TILE_HEADS = 2
RWKV_TILE_UNROLL = 16

MLSTM_HEADS = 8
MLSTM_V_DIM = 256
MLSTM_QK_DIM = 128
MLSTM_CHUNK = 128
MLSTM_HEAD_UNROLL = 2
MLSTM_MAIN = 2 * MLSTM_HEADS * MLSTM_QK_DIM + 3 * MLSTM_HEADS * MLSTM_V_DIM

FOXT_BLOCK = 256
FOXT_HEADS = 16
FOXT_KV = 256
FOXT_UNROLL = 8
FOXT_V_ROWS = 80

VMEM_LIMIT = 56 * 1024 * 1024
ROW_TILE = 512
PROJ_TM = 1024
PROJ_TN_CHOICES = (1536, 1024, 512, 256, 128)


def _params(*sem):
    return pltpu.CompilerParams(dimension_semantics=sem, vmem_limit_bytes=VMEM_LIMIT)


def _nt(a, b, precision=None):
    return lax.dot_general(a, b, (((1,), (1,)), ((), ())), precision=precision,
                           preferred_element_type=F32)


def _tn(a, b, precision=None):
    return lax.dot_general(a, b, (((0,), (0,)), ((), ())), precision=precision,
                           preferred_element_type=F32)


def _mm(a, b, precision=None):
    return jnp.dot(a, b, precision=precision, preferred_element_type=F32)


def _rms_rows(x, g):
    return x * lax.rsqrt(jnp.mean(x * x, axis=-1, keepdims=True) + RMS_EPS) * g


def _silu(x):
    return x * jax.nn.sigmoid(x)


def _round_robin(gens):
    results = [None] * len(gens)
    live = list(range(len(gens)))
    while live:
        for n in list(live):
            try:
                next(gens[n])
            except StopIteration as done:
                results[n] = done.value
                live.remove(n)
    return results


def _norm_matmul_body(has_gates, x_ref, g_ref, w_ref, *refs):
    if has_gates:
        wc_ref, wr_ref, o_ref, cols_ref, rows_ref, xn_ref = refs
    else:
        o_ref, xn_ref = refs

    @pl.when(pl.program_id(1) == 0)
    def _():
        xn = _rms_rows(x_ref[...], g_ref[...]).astype(BF16)
        xn_ref[...] = xn
        if has_gates:
            cols_ref[...] = _mm(xn, wc_ref[...])
            rows_ref[...] = _nt(wr_ref[...], xn)

    o_ref[...] = _mm(xn_ref[...], w_ref[...]).astype(o_ref.dtype)


def _proj_tiles(m, n):
    tm = min(PROJ_TM, m)
    tn = next(t for t in PROJ_TN_CHOICES if n % t == 0)
    return tm, tn


def _norm_matmul(x, g, w, w_gates=None, *, name="norm_matmul"):
    m, k = x.shape
    n = w.shape[1]
    tm, tn = _proj_tiles(m, n)
    has_gates = w_gates is not None
    in_specs = [
        pl.BlockSpec((tm, k), lambda i, j: (i, 0)),
        pl.BlockSpec((1, k), lambda i, j: (0, 0)),
        pl.BlockSpec((k, tn), lambda i, j: (0, j)),
    ]
    out_specs = [pl.BlockSpec((tm, tn), lambda i, j: (i, j))]
    out_shape = [jax.ShapeDtypeStruct((m, n), BF16)]
    args = [x, g.reshape(1, k), w]
    if has_gates:
        n_g = w_gates.shape[1]
        in_specs += [pl.BlockSpec((k, LANES), lambda i, j: (0, 0)),
                     pl.BlockSpec((n_g, k), lambda i, j: (0, 0))]
        out_specs += [pl.BlockSpec((tm, LANES), lambda i, j: (i, 0)),
                      pl.BlockSpec((n_g, tm), lambda i, j: (0, i))]
        out_shape += [jax.ShapeDtypeStruct((m, LANES), F32), jax.ShapeDtypeStruct((n_g, m), F32)]
        args += [jnp.zeros((k, LANES), BF16).at[:, :n_g].set(w_gates.astype(BF16)),
                 w_gates.T.astype(BF16)]
    outs = pl.pallas_call(
        functools.partial(_norm_matmul_body, has_gates),
        grid=(m // tm, n // tn),
        in_specs=in_specs,
        out_specs=out_specs,
        out_shape=out_shape,
        scratch_shapes=[pltpu.VMEM((tm, k), BF16)],
        compiler_params=_params("parallel", "arbitrary"),
        name=name,
    )(*args)
    return outs if has_gates else outs[0]


def _out_proj_body(a_ref, w_ref, r_ref, o_ref):
    o_ref[...] = r_ref[...] + _mm(a_ref[...], w_ref[...])


def _out_proj(a, w, resid):
    m, k = a.shape
    n = w.shape[1]
    tm = min(ROW_TILE, m)
    return pl.pallas_call(
        _out_proj_body,
        grid=(m // tm,),
        in_specs=[
            pl.BlockSpec((tm, k), lambda i: (i, 0)),
            pl.BlockSpec((k, n), lambda i: (0, 0)),
            pl.BlockSpec((tm, n), lambda i: (i, 0)),
        ],
        out_specs=pl.BlockSpec((tm, n), lambda i: (i, 0)),
        out_shape=jax.ShapeDtypeStruct((m, n), F32),
        compiler_params=_params("parallel"),
        name="out_proj",
    )(a, w, resid)


def _t5_bucket(dist):
    max_exact = REL_BUCKETS // 2
    d_f = jnp.maximum(dist, 1).astype(F32)
    large = max_exact + (jnp.log(d_f / max_exact) / math.log(REL_MAX_DIST / max_exact)
                         * (REL_BUCKETS - max_exact)).astype(jnp.int32)
    large = jnp.minimum(large, REL_BUCKETS - 1)
    return jnp.where(dist < max_exact, dist, large)


def _swa_body(q_ref, gate_ref, kc_ref, kp_ref, vc_ref, vp_ref, bias_ref, sink_ref, qg_ref, kg_ref,
              o_ref):
    L = SWA_BLOCK
    k0 = lax.broadcasted_iota(jnp.int32, (2 * L, LANES), 1) < HEAD_DIM
    t0 = lax.broadcasted_iota(jnp.int32, (L, LANES), 1) < HEAD_DIM
    bi = lax.broadcasted_iota(jnp.int32, (LANES, LANES), 0) // HEAD_DIM
    bj = lax.broadcasted_iota(jnp.int32, (LANES, LANES), 1) // HEAD_DIM
    bd_ones = (bi == bj).astype(BF16)
    zero = jnp.zeros((2 * L, LANES), F32)
    one = jnp.ones((2 * L, LANES), F32)
    ones_bd = jnp.concatenate([jnp.where(k0, one, zero), jnp.where(k0, zero, one)], axis=0).astype(BF16)

    def head_mean_sq(z):
        return _mm((z * z).astype(BF16), bd_ones) * (1.0 / HEAD_DIM)

    def both_halves(x, head_in_low_lanes):
        if head_in_low_lanes:
            lo = jnp.where(k0, x, zero)
            hi = pltpu.roll(lo, HEAD_DIM, axis=1)
        else:
            hi = jnp.where(k0, zero, x)
            lo = pltpu.roll(hi, HEAD_DIM, axis=1)
        return jnp.concatenate([lo, hi], axis=0)

    k_all = jnp.concatenate([kp_ref[...], kc_ref[...]], axis=0).astype(F32)
    v_all = jnp.concatenate([vp_ref[...], vc_ref[...]], axis=0).astype(F32)
    kbd, rhs = [], []
    for t in range(SWA_KV_WIDTH // LANES):
        kt = k_all[:, t * LANES:(t + 1) * LANES]
        kn = kt * lax.rsqrt(head_mean_sq(kt) + RMS_EPS) * kg_ref[...]
        vt = v_all[:, t * LANES:(t + 1) * LANES]
        for low in (True, False):
            kbd.append(both_halves(kn, low).astype(BF16))
            rhs.append(jnp.concatenate([both_halves(vt, low).astype(BF16), ones_bd], axis=1))

    def pair(p):
        kh = p // (SWA_GROUP // 2)
        sl = slice(p * LANES, (p + 1) * LANES)
        q2 = q_ref[:, sl].astype(F32)
        ms = head_mean_sq(q2)
        yield
        qn = (q2 * lax.rsqrt(ms + RMS_EPS) * qg_ref[...]).astype(BF16)
        s = _nt(qn, kbd[kh]) + bias_ref[0, p]
        yield
        sink_a = sink_ref[2 * p]
        sink_b = sink_ref[2 * p + 1]
        m_a = jnp.maximum(jnp.max(s[:, :2 * L], axis=-1, keepdims=True), sink_a)
        m_b = jnp.maximum(jnp.max(s[:, 2 * L:], axis=-1, keepdims=True), sink_b)
        pm = jnp.concatenate([jnp.exp(s[:, :2 * L] - m_a), jnp.exp(s[:, 2 * L:] - m_b)], axis=1)
        pv = _mm(pm.astype(BF16), rhs[kh])
        yield
        sink_t = jnp.where(t0, jnp.exp(sink_a - m_a), jnp.exp(sink_b - m_b))
        o = pv[:, :LANES] / (pv[:, LANES:] + sink_t)
        return (o * _silu(gate_ref[:, sl].astype(F32))).astype(o_ref.dtype)

    for g0 in range(0, N_PAIRS, SWA_PAIR_UNROLL):
        outs = _round_robin([pair(p) for p in range(g0, g0 + SWA_PAIR_UNROLL)])
        for n, o in enumerate(outs):
            o_ref[:, (g0 + n) * LANES:(g0 + n + 1) * LANES] = o


def _swa_attention(proj, bias, sinks, q_gain2, k_gain2, bsz, seq):
    L = SWA_BLOCK
    nb = seq // L
    m = bsz * seq
    kcol = 2 * D_MODEL // SWA_KV_WIDTH
    prev = lambda r: jnp.where(r % nb == 0, r, r - 1)
    return pl.pallas_call(
        _swa_body,
        grid=(bsz * nb,),
        in_specs=[
            pl.BlockSpec((L, D_MODEL), lambda r: (r, 0)),
            pl.BlockSpec((L, D_MODEL), lambda r: (r, 1)),
            pl.BlockSpec((L, SWA_KV_WIDTH), lambda r: (r, kcol)),
            pl.BlockSpec((L, SWA_KV_WIDTH), lambda r: (prev(r), kcol)),
            pl.BlockSpec((L, SWA_KV_WIDTH), lambda r: (r, kcol + 1)),
            pl.BlockSpec((L, SWA_KV_WIDTH), lambda r: (prev(r), kcol + 1)),
            pl.BlockSpec((1, N_PAIRS, L, 4 * L), lambda r: (jnp.where(r % nb == 0, 0, 1), 0, 0, 0)),
            pl.BlockSpec(memory_space=pltpu.SMEM),
            pl.BlockSpec((1, LANES), lambda r: (0, 0)),
            pl.BlockSpec((1, LANES), lambda r: (0, 0)),
        ],
        out_specs=pl.BlockSpec((L, D_MODEL), lambda r: (r, 0)),
        out_shape=jax.ShapeDtypeStruct((m, D_MODEL), BF16),
        compiler_params=_params("parallel"),
        name="swa_attention",
    )(proj, proj, proj, proj, proj, proj, bias, sinks, q_gain2, k_gain2)


def _swa_bias_tables(rel_bias):
    L = SWA_BLOCK
    qi = jnp.arange(L)[:, None]
    kj = jnp.arange(2 * L)[None, :]
    dist = qi + L - kj
    in_band = (dist >= 0) & (dist < L)
    onehot = (_t5_bucket(jnp.maximum(dist, 0))[..., None] == jnp.arange(REL_BUCKETS)).astype(F32)
    bias = jnp.einsum("ijb,bh->hij", onehot, rel_bias.astype(F32), precision=HIGHEST)
    general = jnp.where(in_band[None], bias, -jnp.inf)
    first = jnp.where((in_band & (kj >= L))[None], bias, -jnp.inf)
    pairs = lambda t: t.reshape(N_PAIRS, 2, L, 2 * L).transpose(0, 2, 1, 3).reshape(N_PAIRS, L, 4 * L)
    return jnp.stack([pairs(first), pairs(general)])


def _swa_layer(x2, bsz, seq, rel_bias, norm, w_in, q_gain, k_gain, sinks, w_out):
    W = D_MODEL
    kvw = SWA_KV_WIDTH
    w_perm = jnp.concatenate([w_in[:, :W], w_in[:, W + 2 * kvw:], w_in[:, W:W + 2 * kvw]], axis=1)
    proj = _norm_matmul(x2, norm, w_perm.astype(BF16), name="swa_in_proj")
    q_gain2 = (jnp.tile(q_gain, 2) * HEAD_DIM ** -0.5).reshape(1, LANES).astype(F32)
    k_gain2 = jnp.tile(k_gain, 2).reshape(1, LANES).astype(F32)
    a = _swa_attention(proj, _swa_bias_tables(rel_bias), sinks.astype(F32), q_gain2, k_gain2, bsz, seq)
    return _out_proj(a, w_out.astype(BF16), x2)


def _shifted_mix(x_ref, xp_ref, g_ref, seq_start):
    h = _rms_rows(x_ref[...], g_ref[...])
    h_prev_row = _rms_rows(xp_ref[7:8, :], g_ref[...])
    h_prev_row = jnp.where(seq_start, 0.0, h_prev_row)
    rolled = pltpu.roll(h, 1, axis=0)
    row = lax.broadcasted_iota(jnp.int32, h.shape, 0)
    shifted = jnp.where(row == 0, h_prev_row, rolled)
    return h, shifted - h


def _rwkv_in_body(tiles_per_seq, x_ref, xp_ref, g_ref, mix_ref, w_ref, o_ref, xm_ref):
    i = pl.program_id(0)

    @pl.when((pl.program_id(1) == 0) & (pl.program_id(2) == 0))
    def _():
        h, xx = _shifted_mix(x_ref, xp_ref, g_ref, i % tiles_per_seq == 0)
        for c in range(4):
            xm_ref[c] = (h + xx * mix_ref[c:c + 1, :]).astype(BF16)

    c = pl.program_id(1)
    o_ref[0] = _mm(xm_ref[c], w_ref[0]).astype(o_ref.dtype)


def _rwkv_in_proj(x2, seq, norm, mix8, w4):
    m, k = x2.shape
    tm = min(RWKV_TM, seq)
    tn = k
    rows8 = tm // 8
    return pl.pallas_call(
        functools.partial(_rwkv_in_body, seq // tm),
        grid=(m // tm, 4, k // tn),
        in_specs=[
            pl.BlockSpec((tm, k), lambda i, c, j: (i, 0)),
            pl.BlockSpec((8, k), lambda i, c, j: (jnp.maximum(i * rows8 - 1, 0), 0)),
            pl.BlockSpec((1, k), lambda i, c, j: (0, 0)),
            pl.BlockSpec((8, k), lambda i, c, j: (0, 0)),
            pl.BlockSpec((1, k, tn), lambda i, c, j: (c, 0, j)),
        ],
        out_specs=pl.BlockSpec((1, tm, tn), lambda i, c, j: (c, i, j)),
        out_shape=jax.ShapeDtypeStruct((4, m, k), BF16),
        scratch_shapes=[pltpu.VMEM((4, tm, k), BF16)],
        compiler_params=_params("parallel", "arbitrary", "arbitrary"),
        name="rwkv_in_proj",
    )(x2, x2, norm.reshape(1, k), mix8, w4)


def _rwkv_lora_body(tiles_per_seq, x_ref, xp_ref, g_ref, mix_ref, w1_ref, w2_ref, a1_ref, a2_ref,
                    w0_ref, a0_ref, lw_ref, a_ref):
    i = pl.program_id(0)
    h, xx = _shifted_mix(x_ref, xp_ref, g_ref, i % tiles_per_seq == 0)
    xw = (h + xx * mix_ref[4:5, :]).astype(BF16)
    xa = (h + xx * mix_ref[5:6, :]).astype(BF16)
    t = jnp.tanh(_mm(xw, w1_ref[...])).astype(BF16)
    w_pre = w0_ref[...] + _mm(t, w2_ref[...])
    lw_ref[...] = -jnp.exp(-jax.nn.softplus(-w_pre) - 0.5)
    u = _mm(xa, a1_ref[...]).astype(BF16)
    a_ref[...] = jax.nn.sigmoid(a0_ref[...] + _mm(u, a2_ref[...])).astype(a_ref.dtype)


def _rwkv_lora(x2, seq, norm, mix8, w1, w2, a1, a2, w0, a0):
    m, k = x2.shape
    tm = min(RWKV_TM, seq)
    rows8 = tm // 8
    P = RWKV_LORA_PAD
    full = lambda shape: pl.BlockSpec(shape, lambda i: tuple(0 for _ in shape))
    return pl.pallas_call(
        functools.partial(_rwkv_lora_body, seq // tm),
        grid=(m // tm,),
        in_specs=[
            pl.BlockSpec((tm, k), lambda i: (i, 0)),
            pl.BlockSpec((8, k), lambda i: (jnp.maximum(i * rows8 - 1, 0), 0)),
            full((1, k)), full((8, k)),
            full((k, P)), full((P, k)), full((k, P)), full((P, k)),
            full((1, k)), full((1, k)),
        ],
        out_specs=[pl.BlockSpec((tm, k), lambda i: (i, 0)), pl.BlockSpec((tm, k), lambda i: (i, 0))],
        out_shape=[jax.ShapeDtypeStruct((m, k), F32), jax.ShapeDtypeStruct((m, k), BF16)],
        compiler_params=_params("parallel"),
        name="rwkv_lora",
    )(x2, x2, norm.reshape(1, k), mix8, w1, w2, a1, a2, w0.reshape(1, k), a0.reshape(1, k))


def _rwkv_scan_body(r_ref, k_ref, v_ref, g_ref, lw_ref, a_ref, kk_ref, ka_ref, rk_ref,
                    gnw_ref, gnb_ref, o_ref, ht_ref):
    C = RWKV_CHUNK
    G = RWKV_TILE_HEADS
    W = G * HEAD_DIM
    assert C == HEAD_DIM

    @pl.when(pl.program_id(1) == 0)
    def _():
        ht_ref[...] = jnp.zeros_like(ht_ref)

    lane = lax.broadcasted_iota(jnp.int32, (C, W), 1)
    tok = lax.broadcasted_iota(jnp.int32, (C, W), 0)
    col = lane % HEAD_DIM
    strict = tok > col
    incl = tok >= col
    lane_b = lane.astype(F32).astype(BF16)
    head_masks = [(lane_b >= n * HEAD_DIM) & (lane_b < (n + 1) * HEAD_DIM) for n in range(G)]
    ti = lax.broadcasted_iota(jnp.int32, (C, C), 0)
    tj = lax.broadcasted_iota(jnp.int32, (C, C), 1)
    tri = (ti >= tj).astype(F32)
    bi = lax.broadcasted_iota(jnp.int32, (W, W), 0) // HEAD_DIM
    bj = lax.broadcasted_iota(jnp.int32, (W, W), 1) // HEAD_DIM
    bd = bi == bj
    bd_ones = bd.astype(BF16)
    tri2 = jnp.concatenate([tri, tri], axis=1).astype(BF16)

    def head_sum(z):
        return _mm(z.astype(BF16), bd_ones)

    def stack(z):
        zb = z.astype(BF16)
        zero = jnp.zeros_like(zb)
        return jnp.concatenate([jnp.where(hm, zb, zero) for hm in head_masks], axis=0)

    def load(p):
        sl = pl.ds(pl.multiple_of(p * W, W), W)
        f32 = lambda ref: ref[0, :, sl].astype(F32)
        return (f32(r_ref), f32(k_ref), f32(v_ref), f32(g_ref), lw_ref[:, sl],
                a_ref[:, sl].astype(F32), kk_ref[:, sl], ka_ref[:, sl], rk_ref[:, sl], gnw_ref[:, sl],
                gnb_ref[:, sl], ht_ref[p])

    def compute(r, k, v, g, lw, a, k_k, k_a, r_k, gn_w, gn_b, ht):
        kkp = k * k_k
        kp = k * (1.0 + (a - 1.0) * k_a)
        sums = head_sum(jnp.concatenate([kkp * kkp, r * kp * r_k], axis=0))
        ss, bonus_dot = sums[:C], sums[C:]
        lw_hi = lw.astype(BF16)
        lw_lo = (lw - lw_hi.astype(F32)).astype(BF16)
        cum = _mm(tri2, jnp.concatenate([lw_hi, lw_lo], axis=0))
        yield
        kk = kkp / jnp.maximum(jnp.sqrt(ss), 1e-12)
        cum_end = cum[C - 1:C, :]
        g_in = jnp.exp(cum)
        g_prev = jnp.exp(cum - lw)
        g_inv = jnp.exp(-cum)
        g_end = jnp.exp(cum_end - cum)
        kka = kk * a
        a_t = -kk * g_prev
        r_t = r * g_in
        b_t = kka * g_inv
        k_t = kp * g_inv

        x1 = jnp.concatenate([a_t, r_t], axis=0).astype(BF16)
        gbkh = _nt(x1, jnp.concatenate([stack(b_t), stack(k_t), ht.astype(BF16)], axis=0))
        gbk, xh = gbkh[:, :2 * W], gbkh[:, 2 * W:]
        yield
        zero = jnp.zeros((C, W), F32)
        l_ab = jnp.where(strict, gbk[:C, :W], zero)
        l_ak = jnp.where(strict, gbk[:C, W:], zero)
        p_rb = jnp.where(incl, gbk[C:, :W], zero)
        p_rk = jnp.where(incl, gbk[C:, W:], zero)

        t_inv = jnp.where(tok == col, 1.0, zero) + jnp.where(
            (tok // 2 == col // 2) & (tok % 2 == 1) & (col % 2 == 0), l_ab, zero)
        v2 = stack(v)
        rhs = xh[:C] + _mm(l_ak.astype(BF16), v2)
        s = 2
        while s < C:
            sel = (tok // (2 * s) == col // (2 * s)) & (tok % (2 * s) >= s) & (col % (2 * s) < s)
            l_s = jnp.where(sel, l_ab, zero)
            tl = _mm(t_inv.astype(BF16), stack(l_s))
            yield
            t_inv = t_inv + _mm(tl.astype(BF16), stack(t_inv))
            yield
            s *= 2

        u = _mm(t_inv.astype(BF16), stack(rhs))
        yield
        p_cat = jnp.concatenate([p_rb, p_rk], axis=1).astype(BF16)
        y = xh[C:] + _mm(p_cat, jnp.concatenate([stack(u), v2], axis=0))
        uv = jnp.concatenate([u, v], axis=0).astype(BF16)
        bk_end = jnp.concatenate([kka * g_end, kp * g_end], axis=0).astype(BF16)
        upd = _tn(uv, bk_end)
        yield
        ht_new = ht * jnp.exp(cum_end) + jnp.where(bd, upd, 0.0)
        mu = head_sum(y) * (1.0 / HEAD_DIM)
        yield
        dy = y - mu
        var = head_sum(dy * dy) * (1.0 / HEAD_DIM)
        yield
        yn = dy * lax.rsqrt(var + GN_EPS) * gn_w + gn_b
        return ((yn + bonus_dot * v) * _silu(g)).astype(o_ref.dtype), ht_new

    def group(gi, carry):
        base = gi * RWKV_TILE_UNROLL
        ins = [load(base + n) for n in range(RWKV_TILE_UNROLL)]
        outs = _round_robin([compute(*x) for x in ins])
        for n, (o, ht_new) in enumerate(outs):
            p = base + n
            o_ref[:, pl.ds(pl.multiple_of(p * W, W), W)] = o
            ht_ref[p] = ht_new
        return carry

    lax.fori_loop(0, D_MODEL // W // RWKV_TILE_UNROLL, group, 0)


def _rwkv_scan(rkvg, lw, a, k_k, k_a, r_k, gn_w, gn_b, bsz, seq):
    C = RWKV_CHUNK
    nc = seq // C
    m = bsz * seq
    k = D_MODEL
    tile_w = RWKV_TILE_HEADS * HEAD_DIM
    part = lambda c: pl.BlockSpec((1, C, k), lambda b, j, c=c: (c, b * nc + j, 0))
    tile = pl.BlockSpec((C, k), lambda b, j: (b * nc + j, 0))
    vec = pl.BlockSpec((1, k), lambda b, j: (0, 0))
    return pl.pallas_call(
        _rwkv_scan_body,
        grid=(bsz, nc),
        in_specs=[part(0), part(1), part(2), part(3), tile, tile, vec, vec, vec, vec, vec],
        out_specs=tile,
        out_shape=jax.ShapeDtypeStruct((m, k), BF16),
        scratch_shapes=[pltpu.VMEM((k // tile_w, tile_w, tile_w), F32)],
        compiler_params=_params("parallel", "arbitrary"),
        name="rwkv_scan",
    )(rkvg, rkvg, rkvg, rkvg, lw, a, k_k.reshape(1, k), k_a.reshape(1, k), r_k.reshape(1, k),
      gn_w.reshape(1, k), gn_b.reshape(1, k))


def _rwkv_layer(x2, bsz, seq, norm, mix, w_in, w0, w_lora1, w_lora2, a0, a_lora1, a_lora2,
                k_k, k_a, r_k, gn_w, gn_b, w_out):
    k = D_MODEL
    P = RWKV_LORA_PAD
    mix8 = jnp.zeros((8, k), F32).at[:mix.shape[0]].set(mix)
    pad_cols = lambda w: jnp.zeros((k, P), BF16).at[:, :w.shape[1]].set(w.astype(BF16))
    pad_rows = lambda w: jnp.zeros((P, k), BF16).at[:w.shape[0]].set(w.astype(BF16))
    rkvg = _rwkv_in_proj(x2, seq, norm, mix8, w_in.astype(BF16))
    lw, a = _rwkv_lora(x2, seq, norm, mix8, pad_cols(w_lora1), pad_rows(w_lora2),
                       pad_cols(a_lora1), pad_rows(a_lora2), w0, a0)
    y = _rwkv_scan(rkvg, lw, a, k_k, k_a, r_k, gn_w, gn_b, bsz, seq)
    return _out_proj(y, w_out.astype(BF16), x2)


def _mlstm_body(q_ref, k_ref, v_ref, op_ref, gate_ref, gc_ref, gr_ref, bi_ref, bf_ref, hg_ref,
                o_ref, ct_ref, m_ref):
    L = MLSTM_CHUNK
    H = MLSTM_HEADS
    dk = MLSTM_QK_DIM
    dv = MLSTM_V_DIM

    @pl.when(pl.program_id(1) == 0)
    def _():
        ct_ref[...] = jnp.zeros_like(ct_ref)
        m_ref[...] = jnp.zeros_like(m_ref)

    ti = lax.broadcasted_iota(jnp.int32, (L, L), 0)
    tj = lax.broadcasted_iota(jnp.int32, (L, L), 1)
    tril = ti >= tj
    tri_f = tril.astype(F32)
    gc = gc_ref[...] + bi_ref[...]
    gr = gr_ref[...] + bf_ref[...]
    lane = lax.broadcasted_iota(jnp.int32, gc.shape, 1)
    f_cols = jnp.where((lane >= H) & (lane < 2 * H), jax.nn.log_sigmoid(gc), 0.0)
    b_cols = _mm(tri_f, f_cols, HIGHEST)
    f_rows = jax.nn.log_sigmoid(gr)
    b_rows = _nt(f_rows, tri_f, HIGHEST)
    ones_blk = jnp.ones((L, LANES), BF16)

    def head(h):
        q = q_ref[:, h * dk:(h + 1) * dk]
        k = k_ref[:, h * dk:(h + 1) * dk]
        v = v_ref[:, h * dv:(h + 1) * dv]
        v_aug = jnp.concatenate([v, ones_blk], axis=-1)
        ct = ct_ref[h]
        m_prev = m_ref[h][:, :1]
        qk = _nt(q, k)
        q_ct = _mm(q, ct.astype(BF16))
        yield
        i_col = gc[:, h:h + 1]
        i_row = gr[h:h + 1, :]
        b_col = b_cols[:, H + h:H + h + 1]
        b_row = b_rows[H + h:H + h + 1, :]
        g_tot = b_col[L - 1:L, :]
        dmat = jnp.where(tril, b_col - b_row + i_row, -jnp.inf)
        inter = b_col + m_prev
        m_t = jnp.maximum(inter, jnp.max(dmat, axis=-1, keepdims=True))
        w_intra = jnp.exp(dmat - m_t) * dk ** -0.5
        w_inter = jnp.exp(inter - m_t) * dk ** -0.5
        sw = qk * w_intra
        num = w_inter * q_ct + _mm(sw.astype(BF16), v_aug)
        decay = g_tot - b_col + i_col
        m_new = jnp.maximum(g_tot + m_prev, jnp.max(decay, axis=0, keepdims=True))
        ws = jnp.exp(decay - m_new)
        carry_scale = jnp.exp(g_tot + m_prev - m_new)
        wv = (ws * v_aug.astype(F32)).astype(BF16)
        upd = _tn(k, wv)
        yield
        den = num[:, dv:dv + 1]
        h_t = num[:, :dv] / jnp.maximum(jnp.abs(den), jnp.exp(-m_t))
        hn = _rms_rows(h_t, hg_ref[:, h * dv:(h + 1) * dv])
        osl = slice(h * dv, (h + 1) * dv)
        out = hn * jax.nn.sigmoid(op_ref[:, osl].astype(F32)) * _silu(gate_ref[:, osl].astype(F32))
        return out.astype(o_ref.dtype), carry_scale * ct + upd, jnp.broadcast_to(m_new, (1, LANES))

    for h0 in range(0, H, MLSTM_HEAD_UNROLL):
        heads = range(h0, h0 + MLSTM_HEAD_UNROLL)
        for h, (out, ct_new, m_new) in zip(heads, _round_robin([head(h) for h in heads])):
            o_ref[:, h * dv:(h + 1) * dv] = out
            ct_ref[h] = ct_new
            m_ref[h] = m_new


def _mlstm_scan(proj, g_cols, g_rows, b_i, b_f, h_gain, bsz, seq):
    L = MLSTM_CHUNK
    H = MLSTM_HEADS
    nc = seq // L
    m = bsz * seq
    qkw = H * MLSTM_QK_DIM
    vw = H * MLSTM_V_DIM
    bias = jnp.concatenate([b_i, b_f]).astype(F32)
    bias_row = jnp.zeros((1, LANES), F32).at[0, :2 * H].set(bias)
    bias_col = bias.reshape(2 * H, 1)
    row = lambda b, j: b * nc + j
    return pl.pallas_call(
        _mlstm_body,
        grid=(bsz, nc),
        in_specs=[
            pl.BlockSpec((L, qkw), lambda b, j: (row(b, j), 0)),
            pl.BlockSpec((L, qkw), lambda b, j: (row(b, j), 1)),
            pl.BlockSpec((L, vw), lambda b, j: (row(b, j), 1)),
            pl.BlockSpec((L, vw), lambda b, j: (row(b, j), 2)),
            pl.BlockSpec((L, vw), lambda b, j: (row(b, j), 3)),
            pl.BlockSpec((L, LANES), lambda b, j: (row(b, j), 0)),
            pl.BlockSpec((2 * H, L), lambda b, j: (0, row(b, j))),
            pl.BlockSpec((1, LANES), lambda b, j: (0, 0)),
            pl.BlockSpec((2 * H, 1), lambda b, j: (0, 0)),
            pl.BlockSpec((1, vw), lambda b, j: (0, 0)),
        ],
        out_specs=pl.BlockSpec((L, vw), lambda b, j: (row(b, j), 0)),
        out_shape=jax.ShapeDtypeStruct((m, vw), BF16),
        scratch_shapes=[
            pltpu.VMEM((H, MLSTM_QK_DIM, MLSTM_V_DIM + LANES), F32),
            pltpu.VMEM((H, 1, LANES), F32),
        ],
        compiler_params=_params("parallel", "arbitrary"),
        name="mlstm_scan",
    )(proj, proj, proj, proj, proj, g_cols, g_rows, bias_row, bias_col, h_gain.reshape(1, vw))


def _mlstm_layer(x2, bsz, seq, norm, w_in, b_i, b_f, h_gain, w_out):
    proj, g_cols, g_rows = _norm_matmul(x2, norm, w_in[:, :MLSTM_MAIN].astype(BF16),
                                        w_in[:, MLSTM_MAIN:], name="mlstm_in_proj")
    a = _mlstm_scan(proj, g_cols, g_rows, b_i, b_f, h_gain, bsz, seq)
    return _out_proj(a, w_out.astype(BF16), x2)


def _fox_decay_body(f_ref, bf_ref, o_ref):
    B = LANES
    seq = f_ref.shape[0]
    lf = jax.nn.log_sigmoid(f_ref[...] + bf_ref[...])
    ti = lax.broadcasted_iota(jnp.int32, (B, B), 0)
    tj = lax.broadcasted_iota(jnp.int32, (B, B), 1)
    tril = (ti >= tj).astype(F32)
    lane = lax.broadcasted_iota(jnp.int32, (B, LANES), 1)
    carry = jnp.zeros((1, LANES), F32)
    for c in range(seq // B):
        blk = _mm(tril, lf[c * B:(c + 1) * B, :], HIGHEST) + carry
        carry = blk[B - 1:B, :]
        neg = blk * (-LOG2_E)
        hi = neg.astype(BF16).astype(F32)
        r1 = neg - hi
        mid = r1.astype(BF16).astype(F32)
        lo = r1 - mid
        packed = jnp.where(lane < N_HEADS, hi,
                           jnp.where(lane < 2 * N_HEADS, pltpu.roll(mid, N_HEADS, axis=1),
                                     pltpu.roll(lo, 2 * N_HEADS, axis=1)))
        o_ref[c * B:(c + 1) * B, :] = packed.astype(BF16)


def _fox_decay(f_cols, b_f, bsz, seq):
    bias = jnp.zeros((1, LANES), F32).at[0, :N_HEADS].set(b_f.astype(F32))
    return pl.pallas_call(
        _fox_decay_body,
        grid=(bsz,),
        in_specs=[pl.BlockSpec((seq, LANES), lambda b: (b, 0)), pl.BlockSpec((1, LANES), lambda b: (0, 0))],
        out_specs=pl.BlockSpec((seq, LANES), lambda b: (b, 0)),
        out_shape=jax.ShapeDtypeStruct((bsz * seq, LANES), BF16),
        compiler_params=_params("parallel"),
        name="fox_decay",
    )(f_cols, bias)


def _fox_prep_body(q_ref, k_ref, v_ref, f3_ref, qg_ref, kg_ref, ka_ref, qt_ref, vt_ref):
    S = 2 * LANES
    r = lax.broadcasted_iota(jnp.int32, (LANES, S), 0)
    c = lax.broadcasted_iota(jnp.int32, (LANES, S), 1)
    spread = ((r < HEAD_DIM) & (c == r)) | ((r >= HEAD_DIM) & (c == r + HEAD_DIM))
    spread = spread.astype(BF16)
    rt = lax.broadcasted_iota(jnp.int32, (S, LANES), 0)
    ct = lax.broadcasted_iota(jnp.int32, (S, LANES), 1)
    spread_t = (((ct < HEAD_DIM) & (rt == ct)) | ((ct >= HEAD_DIM) & (rt == ct + HEAD_DIM))).astype(BF16)
    bi = lax.broadcasted_iota(jnp.int32, (LANES, LANES), 0) // HEAD_DIM
    bj = lax.broadcasted_iota(jnp.int32, (LANES, LANES), 1) // HEAD_DIM
    bd_mean = jnp.where(bi == bj, 1.0 / HEAD_DIM, 0.0).astype(BF16)
    row = lax.broadcasted_iota(jnp.int32, (S, 1), 0) % LANES
    q_ones = ((row >= HEAD_DIM) & (row < HEAD_DIM + 3)).astype(F32)
    v_ones = (row == HEAD_DIM).astype(F32)
    fr = lax.broadcasted_iota(jnp.int32, (LANES, S), 0)
    fc = lax.broadcasted_iota(jnp.int32, (LANES, S), 1)
    term = fr // N_HEADS

    def head_norm(x, g):
        ms = _mm((x * x).astype(BF16), bd_mean)
        return (x * lax.rsqrt(ms + RMS_EPS) * g).astype(BF16)

    f3 = f3_ref[...]
    for p in range(N_PAIRS):
        sl = slice(p * LANES, (p + 1) * LANES)
        so = slice(p * S, (p + 1) * S)
        qn = head_norm(q_ref[:, sl].astype(F32), qg_ref[:, sl])
        kn = head_norm(k_ref[:, sl].astype(F32), kg_ref[:, sl])
        pick = (term < 3) & (((fr % N_HEADS == 2 * p) & (fc == HEAD_DIM + term)) |
                             ((fr % N_HEADS == 2 * p + 1) & (fc == LANES + HEAD_DIM + term)))
        ka_ref[:, so] = (_mm(kn, spread) + _mm(f3, pick.astype(BF16))).astype(BF16)
        qt_ref[so, :] = (_nt(spread_t, qn) + q_ones).astype(BF16)
        vt_ref[so, :] = (_nt(spread_t, v_ref[:, sl]) + v_ones).astype(BF16)


def _fox_prep(proj, f3, q_gain_row, k_gain_row):
    m = proj.shape[0]
    tm = min(ROW_TILE, m)
    wide = N_HEADS * LANES
    return pl.pallas_call(
        _fox_prep_body,
        grid=(m // tm,),
        in_specs=[
            pl.BlockSpec((tm, D_MODEL), lambda i: (i, 0)),
            pl.BlockSpec((tm, D_MODEL), lambda i: (i, 1)),
            pl.BlockSpec((tm, D_MODEL), lambda i: (i, 2)),
            pl.BlockSpec((tm, LANES), lambda i: (i, 0)),
            pl.BlockSpec((1, D_MODEL), lambda i: (0, 0)),
            pl.BlockSpec((1, D_MODEL), lambda i: (0, 0)),
        ],
        out_specs=[
            pl.BlockSpec((tm, wide), lambda i: (i, 0)),
            pl.BlockSpec((wide, tm), lambda i: (0, i)),
            pl.BlockSpec((wide, tm), lambda i: (0, i)),
        ],
        out_shape=[
            jax.ShapeDtypeStruct((m, wide), BF16),
            jax.ShapeDtypeStruct((wide, m), BF16),
            jax.ShapeDtypeStruct((wide, m), BF16),
        ],
        compiler_params=_params("parallel"),
        name="fox_prep",
    )(proj, proj, proj, f3, q_gain_row, k_gain_row)


def _fox_t_body(ka_ref, qt_ref, vt_ref, gate_ref, o_ref):
    T = qt_ref.shape[1]
    KV = min(FOXT_KV, ka_ref.shape[0])
    assert KV % T == 0
    i = pl.program_id(2)
    n_full = (i * T) // KV
    k_off = lax.broadcasted_iota(jnp.int32, (KV, T), 0)
    q_pos = i * T + lax.broadcasted_iota(jnp.int32, (KV, T), 1)
    VR = FOXT_V_ROWS

    def head_step(h, j, state, masked):
        m, acc = state
        ks = pl.ds(pl.multiple_of(j * KV, KV), KV)
        hs = slice(h * LANES, (h + 1) * LANES)
        s = _mm(ka_ref[ks, hs], qt_ref[hs, :])
        if masked:
            s = jnp.where(j * KV + k_off <= q_pos, s, -jnp.inf)
        yield
        m_new = jnp.maximum(m, jnp.max(s, axis=0, keepdims=True))
        pt = jnp.exp2(s - m_new).astype(BF16)
        pv = _mm(vt_ref[h * LANES:h * LANES + VR, ks], pt)
        yield
        return m_new, acc * jnp.exp2(m - m_new) + pv

    for h0 in range(0, FOXT_HEADS, FOXT_UNROLL):
        heads = list(range(h0, h0 + FOXT_UNROLL))

        def kv_step(j, states, masked):
            return tuple(_round_robin([head_step(h, j, st, masked) for h, st in zip(heads, states)]))

        init = (jnp.full((1, T), -jnp.inf, F32), jnp.zeros((VR, T), F32))
        states = lax.fori_loop(0, n_full, lambda j, st: kv_step(j, st, False), (init,) * len(heads))
        states = kv_step(n_full, states, True)
        outs = [acc[:HEAD_DIM] / acc[HEAD_DIM:HEAD_DIM + 1] for _, acc in states]
        for n in range(0, len(heads), 2):
            pair_t = jnp.concatenate([outs[n], outs[n + 1]], axis=0)
            sl = slice((h0 + n) * HEAD_DIM, (h0 + n + 2) * HEAD_DIM)
            o_ref[:, sl] = (pair_t.T * _silu(gate_ref[:, sl].astype(F32))).astype(o_ref.dtype)


def _fox_t_attention(ka, qt, vt, proj, bsz, seq):
    T = min(FOXT_BLOCK, seq)
    nq = seq // T
    m = bsz * seq
    gw = FOXT_HEADS * HEAD_DIM
    sw = FOXT_HEADS * LANES
    n_groups = N_HEADS // FOXT_HEADS
    gate_col0 = 3 * D_MODEL // gw
    return pl.pallas_call(
        _fox_t_body,
        grid=(bsz, n_groups, nq),
        in_specs=[
            pl.BlockSpec((seq, sw), lambda b, g, i: (b, g)),
            pl.BlockSpec((sw, T), lambda b, g, i: (g, b * nq + i)),
            pl.BlockSpec((sw, seq), lambda b, g, i: (g, b)),
            pl.BlockSpec((T, gw), lambda b, g, i: (b * nq + i, gate_col0 + g)),
        ],
        out_specs=pl.BlockSpec((T, gw), lambda b, g, i: (b * nq + i, g)),
        out_shape=jax.ShapeDtypeStruct((m, D_MODEL), BF16),
        compiler_params=_params("parallel", "parallel", "arbitrary"),
        name="fox_attention",
    )(ka, qt, vt, proj)


def _fox_layer(x2, bsz, seq, norm, w_in, b_f, q_gain, k_gain, w_out):
    W = D_MODEL
    proj, f_cols, _ = _norm_matmul(x2, norm, w_in[:, :4 * W].astype(BF16), w_in[:, 4 * W:],
                                   name="fox_in_proj")
    f3 = _fox_decay(f_cols, b_f, bsz, seq)
    scale = HEAD_DIM ** -0.5 * LOG2_E
    q_gain_row = (jnp.tile(q_gain, N_HEADS) * scale).reshape(1, W).astype(F32)
    k_gain_row = jnp.tile(k_gain, N_HEADS).reshape(1, W).astype(F32)
    ka, qt, vt = _fox_prep(proj, f3, q_gain_row, k_gain_row)
    a = _fox_t_attention(ka, qt, vt, proj, bsz, seq)
    return _out_proj(a, w_out.astype(BF16), x2)


def kernel(x, rel_bias, swa_norm, swa_w_in, swa_q_gain, swa_k_gain, swa_sinks, swa_w_out, rwkv_norm, rwkv_mix, rwkv_w_in, rwkv_w0, rwkv_w_lora1, rwkv_w_lora2, rwkv_a0, rwkv_a_lora1, rwkv_a_lora2, rwkv_k_k, rwkv_k_a, rwkv_r_k, rwkv_gn_w, rwkv_gn_b, rwkv_w_out, mlstm_norm, mlstm_w_in, mlstm_b_i, mlstm_b_f, mlstm_h_gain, mlstm_w_out, fox_norm, fox_w_in, fox_b_f, fox_q_gain, fox_k_gain, fox_w_out):
    bsz, seq, d = x.shape
    depth = swa_norm.shape[0] + rwkv_norm.shape[0] + mlstm_norm.shape[0] + fox_norm.shape[0]
    x2 = x.reshape(bsz * seq, d)
    for layer in range(depth):
        kind, idx = layer % 4, layer // 4
        if kind == 0:
            x2 = _swa_layer(x2, bsz, seq, rel_bias, swa_norm[idx], swa_w_in[idx], swa_q_gain[idx],
                            swa_k_gain[idx], swa_sinks[idx], swa_w_out[idx])
        elif kind == 1:
            x2 = _rwkv_layer(x2, bsz, seq, rwkv_norm[idx], rwkv_mix[idx], rwkv_w_in[idx],
                             rwkv_w0[idx], rwkv_w_lora1[idx], rwkv_w_lora2[idx], rwkv_a0[idx],
                             rwkv_a_lora1[idx], rwkv_a_lora2[idx], rwkv_k_k[idx], rwkv_k_a[idx],
                             rwkv_r_k[idx], rwkv_gn_w[idx], rwkv_gn_b[idx], rwkv_w_out[idx])
        elif kind == 2:
            x2 = _mlstm_layer(x2, bsz, seq, mlstm_norm[idx], mlstm_w_in[idx], mlstm_b_i[idx],
                              mlstm_b_f[idx], mlstm_h_gain[idx], mlstm_w_out[idx])
        else:
            x2 = _fox_layer(x2, bsz, seq, fox_norm[idx], fox_w_in[idx], fox_b_f[idx],
                            fox_q_gain[idx], fox_k_gain[idx], fox_w_out[idx])
    return x2.reshape(bsz, seq, d)
```

```python
import functools
import math

import jax
import jax.numpy as jnp
from jax import lax
from jax.experimental import pallas as pl
from jax.experimental.pallas import tpu as pltpu

F32 = jnp.float32
BF16 = jnp.bfloat16
HIGHEST = lax.Precision.HIGHEST

D_MODEL = 2048
RMS_EPS = 1e-6
GN_EPS = 64e-5
LOG2_E = math.log2(math.e)

HEAD_DIM = 64
N_HEADS = D_MODEL // HEAD_DIM
LANES = 128
N_PAIRS = D_MODEL // LANES

SWA_KV_HEADS = 4
SWA_GROUP = N_HEADS // SWA_KV_HEADS
SWA_KV_WIDTH = SWA_KV_HEADS * HEAD_DIM
SWA_BLOCK = 128
SWA_PAIR_UNROLL = 8
REL_BUCKETS = 32
REL_MAX_DIST = 128

RWKV_CHUNK = 64
RWKV_TM = 512
RWKV_LORA_PAD = 128
RWKV_TILE_HEADS = 2
RWKV_TILE_UNROLL = 16

MLSTM_HEADS = 8
MLSTM_V_DIM = 256
MLSTM_QK_DIM = 128
MLSTM_CHUNK = 128
MLSTM_HEAD_UNROLL = 2
MLSTM_MAIN = 2 * MLSTM_HEADS * MLSTM_QK_DIM + 3 * MLSTM_HEADS * MLSTM_V_DIM

FOXT_BLOCK = 256
FOXT_HEADS = 16
FOXT_KV = 256
FOXT_UNROLL = 8
FOXT_V_ROWS = 80

VMEM_LIMIT = 56 * 1024 * 1024
ROW_TILE = 512
PROJ_TM = 1024
PROJ_TN_CHOICES = (1536, 1024, 512, 256, 128)


def _params(*sem):
    return pltpu.CompilerParams(dimension_semantics=sem, vmem_limit_bytes=VMEM_LIMIT)


def _nt(a, b, precision=None):
    return lax.dot_general(a, b, (((1,), (1,)), ((), ())), precision=precision,
                           preferred_element_type=F32)


def _tn(a, b, precision=None):
    return lax.dot_general(a, b, (((0,), (0,)), ((), ())), precision=precision,
                           preferred_element_type=F32)


def _mm(a, b, precision=None):
    return jnp.dot(a, b, precision=precision, preferred_element_type=F32)


def _rms_rows(x, g):
    return x * lax.rsqrt(jnp.mean(x * x, axis=-1, keepdims=True) + RMS_EPS) * g


def _silu(x):
    return x * jax.nn.sigmoid(x)


def _round_robin(gens):
    results = [None] * len(gens)
    live = list(range(len(gens)))
    while live:
        for n in list(live):
            try:
                next(gens[n])
            except StopIteration as done:
                results[n] = done.value
                live.remove(n)
    return results


def _norm_matmul_body(has_gates, x_ref, g_ref, w_ref, *refs):
    if has_gates:
        wc_ref, wr_ref, o_ref, cols_ref, rows_ref, xn_ref = refs
    else:
        o_ref, xn_ref = refs

    @pl.when(pl.program_id(1) == 0)
    def _():
        xn = _rms_rows(x_ref[...], g_ref[...]).astype(BF16)
        xn_ref[...] = xn
        if has_gates:
            cols_ref[...] = _mm(xn, wc_ref[...])
            rows_ref[...] = _nt(wr_ref[...], xn)

    o_ref[...] = _mm(xn_ref[...], w_ref[...]).astype(o_ref.dtype)


def _proj_tiles(m, n):
    tm = min(PROJ_TM, m)
    tn = next(t for t in PROJ_TN_CHOICES if n % t == 0)
    return tm, tn


def _norm_matmul(x, g, w, w_gates=None, *, name="norm_matmul"):
    m, k = x.shape
    n = w.shape[1]
    tm, tn = _proj_tiles(m, n)
    has_gates = w_gates is not None
    in_specs = [
        pl.BlockSpec((tm, k), lambda i, j: (i, 0)),
        pl.BlockSpec((1, k), lambda i, j: (0, 0)),
        pl.BlockSpec((k, tn), lambda i, j: (0, j)),
    ]
    out_specs = [pl.BlockSpec((tm, tn), lambda i, j: (i, j))]
    out_shape = [jax.ShapeDtypeStruct((m, n), BF16)]
    args = [x, g.reshape(1, k), w]
    if has_gates:
        n_g = w_gates.shape[1]
        in_specs += [pl.BlockSpec((k, LANES), lambda i, j: (0, 0)),
                     pl.BlockSpec((n_g, k), lambda i, j: (0, 0))]
        out_specs += [pl.BlockSpec((tm, LANES), lambda i, j: (i, 0)),
                      pl.BlockSpec((n_g, tm), lambda i, j: (0, i))]
        out_shape += [jax.ShapeDtypeStruct((m, LANES), F32), jax.ShapeDtypeStruct((n_g, m), F32)]
        args += [jnp.zeros((k, LANES), BF16).at[:, :n_g].set(w_gates.astype(BF16)),
                 w_gates.T.astype(BF16)]
    outs = pl.pallas_call(
        functools.partial(_norm_matmul_body, has_gates),
        grid=(m // tm, n // tn),
        in_specs=in_specs,
        out_specs=out_specs,
        out_shape=out_shape,
        scratch_shapes=[pltpu.VMEM((tm, k), BF16)],
        compiler_params=_params("parallel", "arbitrary"),
        name=name,
    )(*args)
    return outs if has_gates else outs[0]


def _out_proj_body(a_ref, w_ref, r_ref, o_ref):
    o_ref[...] = r_ref[...] + _mm(a_ref[...], w_ref[...])


def _out_proj(a, w, resid):
    m, k = a.shape
    n = w.shape[1]
    tm = min(ROW_TILE, m)
    return pl.pallas_call(
        _out_proj_body,
        grid=(m // tm,),
        in_specs=[
            pl.BlockSpec((tm, k), lambda i: (i, 0)),
            pl.BlockSpec((k, n), lambda i: (0, 0)),
            pl.BlockSpec((tm, n), lambda i: (i, 0)),
        ],
        out_specs=pl.BlockSpec((tm, n), lambda i: (i, 0)),
        out_shape=jax.ShapeDtypeStruct((m, n), F32),
        compiler_params=_params("parallel"),
        name="out_proj",
    )(a, w, resid)


def _t5_bucket(dist):
    max_exact = REL_BUCKETS // 2
    d_f = jnp.maximum(dist, 1).astype(F32)
    large = max_exact + (jnp.log(d_f / max_exact) / math.log(REL_MAX_DIST / max_exact)
                         * (REL_BUCKETS - max_exact)).astype(jnp.int32)
    large = jnp.minimum(large, REL_BUCKETS - 1)
    return jnp.where(dist < max_exact, dist, large)


def _swa_body(q_ref, gate_ref, kc_ref, kp_ref, vc_ref, vp_ref, bias_ref, sink_ref, qg_ref, kg_ref,
              o_ref):
    L = SWA_BLOCK
    k0 = lax.broadcasted_iota(jnp.int32, (2 * L, LANES), 1) < HEAD_DIM
    t0 = lax.broadcasted_iota(jnp.int32, (L, LANES), 1) < HEAD_DIM
    bi = lax.broadcasted_iota(jnp.int32, (LANES, LANES), 0) // HEAD_DIM
    bj = lax.broadcasted_iota(jnp.int32, (LANES, LANES), 1) // HEAD_DIM
    bd_ones = (bi == bj).astype(BF16)
    zero = jnp.zeros((2 * L, LANES), F32)
    one = jnp.ones((2 * L, LANES), F32)
    ones_bd = jnp.concatenate([jnp.where(k0, one, zero), jnp.where(k0, zero, one)], axis=0).astype(BF16)

    def head_mean_sq(z):
        return _mm((z * z).astype(BF16), bd_ones) * (1.0 / HEAD_DIM)

    def both_halves(x, head_in_low_lanes):
        if head_in_low_lanes:
            lo = jnp.where(k0, x, zero)
            hi = pltpu.roll(lo, HEAD_DIM, axis=1)
        else:
            hi = jnp.where(k0, zero, x)
            lo = pltpu.roll(hi, HEAD_DIM, axis=1)
        return jnp.concatenate([lo, hi], axis=0)

    k_all = jnp.concatenate([kp_ref[...], kc_ref[...]], axis=0).astype(F32)
    v_all = jnp.concatenate([vp_ref[...], vc_ref[...]], axis=0).astype(F32)
    kbd, rhs = [], []
    for t in range(SWA_KV_WIDTH // LANES):
        kt = k_all[:, t * LANES:(t + 1) * LANES]
        kn = kt * lax.rsqrt(head_mean_sq(kt) + RMS_EPS) * kg_ref[...]
        vt = v_all[:, t * LANES:(t + 1) * LANES]
        for low in (True, False):
            kbd.append(both_halves(kn, low).astype(BF16))
            rhs.append(jnp.concatenate([both_halves(vt, low).astype(BF16), ones_bd], axis=1))

    def pair(p):
        kh = p // (SWA_GROUP // 2)
        sl = slice(p * LANES, (p + 1) * LANES)
        q2 = q_ref[:, sl].astype(F32)
        ms = head_mean_sq(q2)
        yield
        qn = (q2 * lax.rsqrt(ms + RMS_EPS) * qg_ref[...]).astype(BF16)
        s = _nt(qn, kbd[kh]) + bias_ref[0, p]
        yield
        sink_a = sink_ref[2 * p]
        sink_b = sink_ref[2 * p + 1]
        m_a = jnp.maximum(jnp.max(s[:, :2 * L], axis=-1, keepdims=True), sink_a)
        m_b = jnp.maximum(jnp.max(s[:, 2 * L:], axis=-1, keepdims=True), sink_b)
        pm = jnp.concatenate([jnp.exp(s[:, :2 * L] - m_a), jnp.exp(s[:, 2 * L:] - m_b)], axis=1)
        pv = _mm(pm.astype(BF16), rhs[kh])
        yield
        sink_t = jnp.where(t0, jnp.exp(sink_a - m_a), jnp.exp(sink_b - m_b))
        o = pv[:, :LANES] / (pv[:, LANES:] + sink_t)
        return (o * _silu(gate_ref[:, sl].astype(F32))).astype(o_ref.dtype)

    for g0 in range(0, N_PAIRS, SWA_PAIR_UNROLL):
        outs = _round_robin([pair(p) for p in range(g0, g0 + SWA_PAIR_UNROLL)])
        for n, o in enumerate(outs):
            o_ref[:, (g0 + n) * LANES:(g0 + n + 1) * LANES] = o


def _swa_attention(proj, bias, sinks, q_gain2, k_gain2, bsz, seq):
    L = SWA_BLOCK
    nb = seq // L
    m = bsz * seq
    kcol = 2 * D_MODEL // SWA_KV_WIDTH
    prev = lambda r: jnp.where(r % nb == 0, r, r - 1)
    return pl.pallas_call(
        _swa_body,
        grid=(bsz * nb,),
        in_specs=[
            pl.BlockSpec((L, D_MODEL), lambda r: (r, 0)),
            pl.BlockSpec((L, D_MODEL), lambda r: (r, 1)),
            pl.BlockSpec((L, SWA_KV_WIDTH), lambda r: (r, kcol)),
            pl.BlockSpec((L, SWA_KV_WIDTH), lambda r: (prev(r), kcol)),
            pl.BlockSpec((L, SWA_KV_WIDTH), lambda r: (r, kcol + 1)),
            pl.BlockSpec((L, SWA_KV_WIDTH), lambda r: (prev(r), kcol + 1)),
            pl.BlockSpec((1, N_PAIRS, L, 4 * L), lambda r: (jnp.where(r % nb == 0, 0, 1), 0, 0, 0)),
            pl.BlockSpec(memory_space=pltpu.SMEM),
            pl.BlockSpec((1, LANES), lambda r: (0, 0)),
            pl.BlockSpec((1, LANES), lambda r: (0, 0)),
        ],
        out_specs=pl.BlockSpec((L, D_MODEL), lambda r: (r, 0)),
        out_shape=jax.ShapeDtypeStruct((m, D_MODEL), BF16),
        compiler_params=_params("parallel"),
        name="swa_attention",
    )(proj, proj, proj, proj, proj, proj, bias, sinks, q_gain2, k_gain2)


def _swa_bias_tables(rel_bias):
    L = SWA_BLOCK
    qi = jnp.arange(L)[:, None]
    kj = jnp.arange(2 * L)[None, :]
    dist = qi + L - kj
    in_band = (dist >= 0) & (dist < L)
    onehot = (_t5_bucket(jnp.maximum(dist, 0))[..., None] == jnp.arange(REL_BUCKETS)).astype(F32)
    bias = jnp.einsum("ijb,bh->hij", onehot, rel_bias.astype(F32), precision=HIGHEST)
    general = jnp.where(in_band[None], bias, -jnp.inf)
    first = jnp.where((in_band & (kj >= L))[None], bias, -jnp.inf)
    pairs = lambda t: t.reshape(N_PAIRS, 2, L, 2 * L).transpose(0, 2, 1, 3).reshape(N_PAIRS, L, 4 * L)
    return jnp.stack([pairs(first), pairs(general)])


def _swa_layer(x2, bsz, seq, rel_bias, norm, w_in, q_gain, k_gain, sinks, w_out):
    W = D_MODEL
    kvw = SWA_KV_WIDTH
    w_perm = jnp.concatenate([w_in[:, :W], w_in[:, W + 2 * kvw:], w_in[:, W:W + 2 * kvw]], axis=1)
    proj = _norm_matmul(x2, norm, w_perm.astype(BF16), name="swa_in_proj")
    q_gain2 = (jnp.tile(q_gain, 2) * HEAD_DIM ** -0.5).reshape(1, LANES).astype(F32)
    k_gain2 = jnp.tile(k_gain, 2).reshape(1, LANES).astype(F32)
    a = _swa_attention(proj, _swa_bias_tables(rel_bias), sinks.astype(F32), q_gain2, k_gain2, bsz, seq)
    return _out_proj(a, w_out.astype(BF16), x2)


def _shifted_mix(x_ref, xp_ref, g_ref, seq_start):
    h = _rms_rows(x_ref[...], g_ref[...])
    h_prev_row = _rms_rows(xp_ref[7:8, :], g_ref[...])
    h_prev_row = jnp.where(seq_start, 0.0, h_prev_row)
    rolled = pltpu.roll(h, 1, axis=0)
    row = lax.broadcasted_iota(jnp.int32, h.shape, 0)
    shifted = jnp.where(row == 0, h_prev_row, rolled)
    return h, shifted - h


def _rwkv_in_body(tiles_per_seq, x_ref, xp_ref, g_ref, mix_ref, w_ref, o_ref, xm_ref):
    i = pl.program_id(0)

    @pl.when((pl.program_id(1) == 0) & (pl.program_id(2) == 0))
    def _():
        h, xx = _shifted_mix(x_ref, xp_ref, g_ref, i % tiles_per_seq == 0)
        for c in range(4):
            xm_ref[c] = (h + xx * mix_ref[c:c + 1, :]).astype(BF16)

    c = pl.program_id(1)
    o_ref[0] = _mm(xm_ref[c], w_ref[0]).astype(o_ref.dtype)


def _rwkv_in_proj(x2, seq, norm, mix8, w4):
    m, k = x2.shape
    tm = min(RWKV_TM, seq)
    tn = k
    rows8 = tm // 8
    return pl.pallas_call(
        functools.partial(_rwkv_in_body, seq // tm),
        grid=(m // tm, 4, k // tn),
        in_specs=[
            pl.BlockSpec((tm, k), lambda i, c, j: (i, 0)),
            pl.BlockSpec((8, k), lambda i, c, j: (jnp.maximum(i * rows8 - 1, 0), 0)),
            pl.BlockSpec((1, k), lambda i, c, j: (0, 0)),
            pl.BlockSpec((8, k), lambda i, c, j: (0, 0)),
            pl.BlockSpec((1, k, tn), lambda i, c, j: (c, 0, j)),
        ],
        out_specs=pl.BlockSpec((1, tm, tn), lambda i, c, j: (c, i, j)),
        out_shape=jax.ShapeDtypeStruct((4, m, k), BF16),
        scratch_shapes=[pltpu.VMEM((4, tm, k), BF16)],
        compiler_params=_params("parallel", "arbitrary", "arbitrary"),
        name="rwkv_in_proj",
    )(x2, x2, norm.reshape(1, k), mix8, w4)


def _rwkv_lora_body(tiles_per_seq, x_ref, xp_ref, g_ref, mix_ref, w1_ref, w2_ref, a1_ref, a2_ref,
                    w0_ref, a0_ref, lw_ref, a_ref):
    i = pl.program_id(0)
    h, xx = _shifted_mix(x_ref, xp_ref, g_ref, i % tiles_per_seq == 0)
    xw = (h + xx * mix_ref[4:5, :]).astype(BF16)
    xa = (h + xx * mix_ref[5:6, :]).astype(BF16)
    t = jnp.tanh(_mm(xw, w1_ref[...])).astype(BF16)
    w_pre = w0_ref[...] + _mm(t, w2_ref[...])
    lw_ref[...] = -jnp.exp(-jax.nn.softplus(-w_pre) - 0.5)
    u = _mm(xa, a1_ref[...]).astype(BF16)
    a_ref[...] = jax.nn.sigmoid(a0_ref[...] + _mm(u, a2_ref[...])).astype(a_ref.dtype)


def _rwkv_lora(x2, seq, norm, mix8, w1, w2, a1, a2, w0, a0):
    m, k = x2.shape
    tm = min(RWKV_TM, seq)
    rows8 = tm // 8
    P = RWKV_LORA_PAD
    full = lambda shape: pl.BlockSpec(shape, lambda i: tuple(0 for _ in shape))
    return pl.pallas_call(
        functools.partial(_rwkv_lora_body, seq // tm),
        grid=(m // tm,),
        in_specs=[
            pl.BlockSpec((tm, k), lambda i: (i, 0)),
            pl.BlockSpec((8, k), lambda i: (jnp.maximum(i * rows8 - 1, 0), 0)),
            full((1, k)), full((8, k)),
            full((k, P)), full((P, k)), full((k, P)), full((P, k)),
            full((1, k)), full((1, k)),
        ],
        out_specs=[pl.BlockSpec((tm, k), lambda i: (i, 0)), pl.BlockSpec((tm, k), lambda i: (i, 0))],
        out_shape=[jax.ShapeDtypeStruct((m, k), F32), jax.ShapeDtypeStruct((m, k), BF16)],
        compiler_params=_params("parallel"),
        name="rwkv_lora",
    )(x2, x2, norm.reshape(1, k), mix8, w1, w2, a1, a2, w0.reshape(1, k), a0.reshape(1, k))


def _rwkv_scan_body(r_ref, k_ref, v_ref, g_ref, lw_ref, a_ref, kk_ref, ka_ref, rk_ref,
                    gnw_ref, gnb_ref, o_ref, ht_ref):
    C = RWKV_CHUNK
    G = RWKV_TILE_HEADS
    W = G * HEAD_DIM
    assert C == HEAD_DIM

    @pl.when(pl.program_id(1) == 0)
    def _():
        ht_ref[...] = jnp.zeros_like(ht_ref)

    lane = lax.broadcasted_iota(jnp.int32, (C, W), 1)
    tok = lax.broadcasted_iota(jnp.int32, (C, W), 0)
    col = lane % HEAD_DIM
    strict = tok > col
    incl = tok >= col
    lane_b = lane.astype(F32).astype(BF16)
    head_masks = [(lane_b >= n * HEAD_DIM) & (lane_b < (n + 1) * HEAD_DIM) for n in range(G)]
    ti = lax.broadcasted_iota(jnp.int32, (C, C), 0)
    tj = lax.broadcasted_iota(jnp.int32, (C, C), 1)
    tri = (ti >= tj).astype(F32)
    bi = lax.broadcasted_iota(jnp.int32, (W, W), 0) // HEAD_DIM
    bj = lax.broadcasted_iota(jnp.int32, (W, W), 1) // HEAD_DIM
    bd = bi == bj
    bd_ones = bd.astype(BF16)
    tri2 = jnp.concatenate([tri, tri], axis=1).astype(BF16)

    def head_sum(z):
        return _mm(z.astype(BF16), bd_ones)

    def stack(z):
        zb = z.astype(BF16)
        zero = jnp.zeros_like(zb)
        return jnp.concatenate([jnp.where(hm, zb, zero) for hm in head_masks], axis=0)

    def load(p):
        sl = pl.ds(pl.multiple_of(p * W, W), W)
        f32 = lambda ref: ref[0, :, sl].astype(F32)
        return (f32(r_ref), f32(k_ref), f32(v_ref), f32(g_ref), lw_ref[:, sl],
                a_ref[:, sl].astype(F32), kk_ref[:, sl], ka_ref[:, sl], rk_ref[:, sl], gnw_ref[:, sl],
                gnb_ref[:, sl], ht_ref[p])

    def compute(r, k, v, g, lw, a, k_k, k_a, r_k, gn_w, gn_b, ht):
        kkp = k * k_k
        kp = k * (1.0 + (a - 1.0) * k_a)
        sums = head_sum(jnp.concatenate([kkp * kkp, r * kp * r_k], axis=0))
        ss, bonus_dot = sums[:C], sums[C:]
        lw_hi = lw.astype(BF16)
        lw_lo = (lw - lw_hi.astype(F32)).astype(BF16)
        cum = _mm(tri2, jnp.concatenate([lw_hi, lw_lo], axis=0))
        yield
        kk = kkp / jnp.maximum(jnp.sqrt(ss), 1e-12)
        cum_end = cum[C - 1:C, :]
        g_in = jnp.exp(cum)
        g_prev = jnp.exp(cum - lw)
        g_inv = jnp.exp(-cum)
        g_end = jnp.exp(cum_end - cum)
        kka = kk * a
        a_t = -kk * g_prev
        r_t = r * g_in
        b_t = kka * g_inv
        k_t = kp * g_inv

        x1 = jnp.concatenate([a_t, r_t], axis=0).astype(BF16)
        gbkh = _nt(x1, jnp.concatenate([stack(b_t), stack(k_t), ht.astype(BF16)], axis=0))
        gbk, xh = gbkh[:, :2 * W], gbkh[:, 2 * W:]
        yield
        zero = jnp.zeros((C, W), F32)
        l_ab = jnp.where(strict, gbk[:C, :W], zero)
        l_ak = jnp.where(strict, gbk[:C, W:], zero)
        p_rb = jnp.where(incl, gbk[C:, :W], zero)
        p_rk = jnp.where(incl, gbk[C:, W:], zero)

        t_inv = jnp.where(tok == col, 1.0, zero) + jnp.where(
            (tok // 2 == col // 2) & (tok % 2 == 1) & (col % 2 == 0), l_ab, zero)
        v2 = stack(v)
        rhs = xh[:C] + _mm(l_ak.astype(BF16), v2)
        s = 2
        while s < C:
            sel = (tok // (2 * s) == col // (2 * s)) & (tok % (2 * s) >= s) & (col % (2 * s) < s)
            l_s = jnp.where(sel, l_ab, zero)
            tl = _mm(t_inv.astype(BF16), stack(l_s))
            yield
            t_inv = t_inv + _mm(tl.astype(BF16), stack(t_inv))
            yield
            s *= 2

        u = _mm(t_inv.astype(BF16), stack(rhs))
        yield
        p_cat = jnp.concatenate([p_rb, p_rk], axis=1).astype(BF16)
        y = xh[C:] + _mm(p_cat, jnp.concatenate([stack(u), v2], axis=0))
        uv = jnp.concatenate([u, v], axis=0).astype(BF16)
        bk_end = jnp.concatenate([kka * g_end, kp * g_end], axis=0).astype(BF16)
        upd = _tn(uv, bk_end)
        yield
        ht_new = ht * jnp.exp(cum_end) + jnp.where(bd, upd, 0.0)
        mu = head_sum(y) * (1.0 / HEAD_DIM)
        yield
        dy = y - mu
        var = head_sum(dy * dy) * (1.0 / HEAD_DIM)
        yield
        yn = dy * lax.rsqrt(var + GN_EPS) * gn_w + gn_b
        return ((yn + bonus_dot * v) * _silu(g)).astype(o_ref.dtype), ht_new

    def group(gi, carry):
        base = gi * RWKV_TILE_UNROLL
        ins = [load(base + n) for n in range(RWKV_TILE_UNROLL)]
        outs = _round_robin([compute(*x) for x in ins])
        for n, (o, ht_new) in enumerate(outs):
            p = base + n
            o_ref[:, pl.ds(pl.multiple_of(p * W, W), W)] = o
            ht_ref[p] = ht_new
        return carry

    lax.fori_loop(0, D_MODEL // W // RWKV_TILE_UNROLL, group, 0)


def _rwkv_scan(rkvg, lw, a, k_k, k_a, r_k, gn_w, gn_b, bsz, seq):
    C = RWKV_CHUNK
    nc = seq // C
    m = bsz * seq
    k = D_MODEL
    tile_w = RWKV_TILE_HEADS * HEAD_DIM
    part = lambda c: pl.BlockSpec((1, C, k), lambda b, j, c=c: (c, b * nc + j, 0))
    tile = pl.BlockSpec((C, k), lambda b, j: (b * nc + j, 0))
    vec = pl.BlockSpec((1, k), lambda b, j: (0, 0))
    return pl.pallas_call(
        _rwkv_scan_body,
        grid=(bsz, nc),
        in_specs=[part(0), part(1), part(2), part(3), tile, tile, vec, vec, vec, vec, vec],
        out_specs=tile,
        out_shape=jax.ShapeDtypeStruct((m, k), BF16),
        scratch_shapes=[pltpu.VMEM((k // tile_w, tile_w, tile_w), F32)],
        compiler_params=_params("parallel", "arbitrary"),
        name="rwkv_scan",
    )(rkvg, rkvg, rkvg, rkvg, lw, a, k_k.reshape(1, k), k_a.reshape(1, k), r_k.reshape(1, k),
      gn_w.reshape(1, k), gn_b.reshape(1, k))


def _rwkv_layer(x2, bsz, seq, norm, mix, w_in, w0, w_lora1, w_lora2, a0, a_lora1, a_lora2,
                k_k, k_a, r_k, gn_w, gn_b, w_out):
    k = D_MODEL
    P = RWKV_LORA_PAD
    mix8 = jnp.zeros((8, k), F32).at[:mix.shape[0]].set(mix)
    pad_cols = lambda w: jnp.zeros((k, P), BF16).at[:, :w.shape[1]].set(w.astype(BF16))
    pad_rows = lambda w: jnp.zeros((P, k), BF16).at[:w.shape[0]].set(w.astype(BF16))
    rkvg = _rwkv_in_proj(x2, seq, norm, mix8, w_in.astype(BF16))
    lw, a = _rwkv_lora(x2, seq, norm, mix8, pad_cols(w_lora1), pad_rows(w_lora2),
                       pad_cols(a_lora1), pad_rows(a_lora2), w0, a0)
    y = _rwkv_scan(rkvg, lw, a, k_k, k_a, r_k, gn_w, gn_b, bsz, seq)
    return _out_proj(y, w_out.astype(BF16), x2)


def _mlstm_body(q_ref, k_ref, v_ref, op_ref, gate_ref, gc_ref, gr_ref, bi_ref, bf_ref, hg_ref,
                o_ref, ct_ref, m_ref):
    L = MLSTM_CHUNK
    H = MLSTM_HEADS
    dk = MLSTM_QK_DIM
    dv = MLSTM_V_DIM

    @pl.when(pl.program_id(1) == 0)
    def _():
        ct_ref[...] = jnp.zeros_like(ct_ref)
        m_ref[...] = jnp.zeros_like(m_ref)

    ti = lax.broadcasted_iota(jnp.int32, (L, L), 0)
    tj = lax.broadcasted_iota(jnp.int32, (L, L), 1)
    tril = ti >= tj
    tri_f = tril.astype(F32)
    gc = gc_ref[...] + bi_ref[...]
    gr = gr_ref[...] + bf_ref[...]
    lane = lax.broadcasted_iota(jnp.int32, gc.shape, 1)
    f_cols = jnp.where((lane >= H) & (lane < 2 * H), jax.nn.log_sigmoid(gc), 0.0)
    b_cols = _mm(tri_f, f_cols, HIGHEST)
    f_rows = jax.nn.log_sigmoid(gr)
    b_rows = _nt(f_rows, tri_f, HIGHEST)
    ones_blk = jnp.ones((L, LANES), BF16)

    def head(h):
        q = q_ref[:, h * dk:(h + 1) * dk]
        k = k_ref[:, h * dk:(h + 1) * dk]
        v = v_ref[:, h * dv:(h + 1) * dv]
        v_aug = jnp.concatenate([v, ones_blk], axis=-1)
        ct = ct_ref[h]
        m_prev = m_ref[h][:, :1]
        qk = _nt(q, k)
        q_ct = _mm(q, ct.astype(BF16))
        yield
        i_col = gc[:, h:h + 1]
        i_row = gr[h:h + 1, :]
        b_col = b_cols[:, H + h:H + h + 1]
        b_row = b_rows[H + h:H + h + 1, :]
        g_tot = b_col[L - 1:L, :]
        dmat = jnp.where(tril, b_col - b_row + i_row, -jnp.inf)
        inter = b_col + m_prev
        m_t = jnp.maximum(inter, jnp.max(dmat, axis=-1, keepdims=True))
        w_intra = jnp.exp(dmat - m_t) * dk ** -0.5
        w_inter = jnp.exp(inter - m_t) * dk ** -0.5
        sw = qk * w_intra
        num = w_inter * q_ct + _mm(sw.astype(BF16), v_aug)
        decay = g_tot - b_col + i_col
        m_new = jnp.maximum(g_tot + m_prev, jnp.max(decay, axis=0, keepdims=True))
        ws = jnp.exp(decay - m_new)
        carry_scale = jnp.exp(g_tot + m_prev - m_new)
        wv = (ws * v_aug.astype(F32)).astype(BF16)
        upd = _tn(k, wv)
        yield
        den = num[:, dv:dv + 1]
        h_t = num[:, :dv] / jnp.maximum(jnp.abs(den), jnp.exp(-m_t))
        hn = _rms_rows(h_t, hg_ref[:, h * dv:(h + 1) * dv])
        osl = slice(h * dv, (h + 1) * dv)
        out = hn * jax.nn.sigmoid(op_ref[:, osl].astype(F32)) * _silu(gate_ref[:, osl].astype(F32))
        return out.astype(o_ref.dtype), carry_scale * ct + upd, jnp.broadcast_to(m_new, (1, LANES))

    for h0 in range(0, H, MLSTM_HEAD_UNROLL):
        heads = range(h0, h0 + MLSTM_HEAD_UNROLL)
        for h, (out, ct_new, m_new) in zip(heads, _round_robin([head(h) for h in heads])):
            o_ref[:, h * dv:(h + 1) * dv] = out
            ct_ref[h] = ct_new
            m_ref[h] = m_new


def _mlstm_scan(proj, g_cols, g_rows, b_i, b_f, h_gain, bsz, seq):
    L = MLSTM_CHUNK
    H = MLSTM_HEADS
    nc = seq // L
    m = bsz * seq
    qkw = H * MLSTM_QK_DIM
    vw = H * MLSTM_V_DIM
    bias = jnp.concatenate([b_i, b_f]).astype(F32)
    bias_row = jnp.zeros((1, LANES), F32).at[0, :2 * H].set(bias)
    bias_col = bias.reshape(2 * H, 1)
    row = lambda b, j: b * nc + j
    return pl.pallas_call(
        _mlstm_body,
        grid=(bsz, nc),
        in_specs=[
            pl.BlockSpec((L, qkw), lambda b, j: (row(b, j), 0)),
            pl.BlockSpec((L, qkw), lambda b, j: (row(b, j), 1)),
            pl.BlockSpec((L, vw), lambda b, j: (row(b, j), 1)),
            pl.BlockSpec((L, vw), lambda b, j: (row(b, j), 2)),
            pl.BlockSpec((L, vw), lambda b, j: (row(b, j), 3)),
            pl.BlockSpec((L, LANES), lambda b, j: (row(b, j), 0)),
            pl.BlockSpec((2 * H, L), lambda b, j: (0, row(b, j))),
            pl.BlockSpec((1, LANES), lambda b, j: (0, 0)),
            pl.BlockSpec((2 * H, 1), lambda b, j: (0, 0)),
            pl.BlockSpec((1, vw), lambda b, j: (0, 0)),
        ],
        out_specs=pl.BlockSpec((L, vw), lambda b, j: (row(b, j), 0)),
        out_shape=jax.ShapeDtypeStruct((m, vw), BF16),
        scratch_shapes=[
            pltpu.VMEM((H, MLSTM_QK_DIM, MLSTM_V_DIM + LANES), F32),
            pltpu.VMEM((H, 1, LANES), F32),
        ],
        compiler_params=_params("parallel", "arbitrary"),
        name="mlstm_scan",
    )(proj, proj, proj, proj, proj, g_cols, g_rows, bias_row, bias_col, h_gain.reshape(1, vw))


def _mlstm_layer(x2, bsz, seq, norm, w_in, b_i, b_f, h_gain, w_out):
    proj, g_cols, g_rows = _norm_matmul(x2, norm, w_in[:, :MLSTM_MAIN].astype(BF16),
                                        w_in[:, MLSTM_MAIN:], name="mlstm_in_proj")
    a = _mlstm_scan(proj, g_cols, g_rows, b_i, b_f, h_gain, bsz, seq)
    return _out_proj(a, w_out.astype(BF16), x2)


def _fox_decay_body(f_ref, bf_ref, o_ref):
    B = LANES
    seq = f_ref.shape[0]
    lf = jax.nn.log_sigmoid(f_ref[...] + bf_ref[...])
    ti = lax.broadcasted_iota(jnp.int32, (B, B), 0)
    tj = lax.broadcasted_iota(jnp.int32, (B, B), 1)
    tril = (ti >= tj).astype(F32)
    lane = lax.broadcasted_iota(jnp.int32, (B, LANES), 1)
    carry = jnp.zeros((1, LANES), F32)
    for c in range(seq // B):
        blk = _mm(tril, lf[c * B:(c + 1) * B, :], HIGHEST) + carry
        carry = blk[B - 1:B, :]
        neg = blk * (-LOG2_E)
        hi = neg.astype(BF16).astype(F32)
        r1 = neg - hi
        mid = r1.astype(BF16).astype(F32)
        lo = r1 - mid
        packed = jnp.where(lane < N_HEADS, hi,
                           jnp.where(lane < 2 * N_HEADS, pltpu.roll(mid, N_HEADS, axis=1),
                                     pltpu.roll(lo, 2 * N_HEADS, axis=1)))
        o_ref[c * B:(c + 1) * B, :] = packed.astype(BF16)


def _fox_decay(f_cols, b_f, bsz, seq):
    bias = jnp.zeros((1, LANES), F32).at[0, :N_HEADS].set(b_f.astype(F32))
    return pl.pallas_call(
        _fox_decay_body,
        grid=(bsz,),
        in_specs=[pl.BlockSpec((seq, LANES), lambda b: (b, 0)), pl.BlockSpec((1, LANES), lambda b: (0, 0))],
        out_specs=pl.BlockSpec((seq, LANES), lambda b: (b, 0)),
        out_shape=jax.ShapeDtypeStruct((bsz * seq, LANES), BF16),
        compiler_params=_params("parallel"),
        name="fox_decay",
    )(f_cols, bias)


def _fox_prep_body(q_ref, k_ref, v_ref, f3_ref, qg_ref, kg_ref, ka_ref, qt_ref, vt_ref):
    S = 2 * LANES
    r = lax.broadcasted_iota(jnp.int32, (LANES, S), 0)
    c = lax.broadcasted_iota(jnp.int32, (LANES, S), 1)
    spread = ((r < HEAD_DIM) & (c == r)) | ((r >= HEAD_DIM) & (c == r + HEAD_DIM))
    spread = spread.astype(BF16)
    rt = lax.broadcasted_iota(jnp.int32, (S, LANES), 0)
    ct = lax.broadcasted_iota(jnp.int32, (S, LANES), 1)
    spread_t = (((ct < HEAD_DIM) & (rt == ct)) | ((ct >= HEAD_DIM) & (rt == ct + HEAD_DIM))).astype(BF16)
    bi = lax.broadcasted_iota(jnp.int32, (LANES, LANES), 0) // HEAD_DIM
    bj = lax.broadcasted_iota(jnp.int32, (LANES, LANES), 1) // HEAD_DIM
    bd_mean = jnp.where(bi == bj, 1.0 / HEAD_DIM, 0.0).astype(BF16)
    row = lax.broadcasted_iota(jnp.int32, (S, 1), 0) % LANES
    q_ones = ((row >= HEAD_DIM) & (row < HEAD_DIM + 3)).astype(F32)
    v_ones = (row == HEAD_DIM).astype(F32)
    fr = lax.broadcasted_iota(jnp.int32, (LANES, S), 0)
    fc = lax.broadcasted_iota(jnp.int32, (LANES, S), 1)
    term = fr // N_HEADS

    def head_norm(x, g):
        ms = _mm((x * x).astype(BF16), bd_mean)
        return (x * lax.rsqrt(ms + RMS_EPS) * g).astype(BF16)

    f3 = f3_ref[...]
    for p in range(N_PAIRS):
        sl = slice(p * LANES, (p + 1) * LANES)
        so = slice(p * S, (p + 1) * S)
        qn = head_norm(q_ref[:, sl].astype(F32), qg_ref[:, sl])
        kn = head_norm(k_ref[:, sl].astype(F32), kg_ref[:, sl])
        pick = (term < 3) & (((fr % N_HEADS == 2 * p) & (fc == HEAD_DIM + term)) |
                             ((fr % N_HEADS == 2 * p + 1) & (fc == LANES + HEAD_DIM + term)))
        ka_ref[:, so] = (_mm(kn, spread) + _mm(f3, pick.astype(BF16))).astype(BF16)
        qt_ref[so, :] = (_nt(spread_t, qn) + q_ones).astype(BF16)
        vt_ref[so, :] = (_nt(spread_t, v_ref[:, sl]) + v_ones).astype(BF16)


def _fox_prep(proj, f3, q_gain_row, k_gain_row):
    m = proj.shape[0]
    tm = min(ROW_TILE, m)
    wide = N_HEADS * LANES
    return pl.pallas_call(
        _fox_prep_body,
        grid=(m // tm,),
        in_specs=[
            pl.BlockSpec((tm, D_MODEL), lambda i: (i, 0)),
            pl.BlockSpec((tm, D_MODEL), lambda i: (i, 1)),
            pl.BlockSpec((tm, D_MODEL), lambda i: (i, 2)),
            pl.BlockSpec((tm, LANES), lambda i: (i, 0)),
            pl.BlockSpec((1, D_MODEL), lambda i: (0, 0)),
            pl.BlockSpec((1, D_MODEL), lambda i: (0, 0)),
        ],
        out_specs=[
            pl.BlockSpec((tm, wide), lambda i: (i, 0)),
            pl.BlockSpec((wide, tm), lambda i: (0, i)),
            pl.BlockSpec((wide, tm), lambda i: (0, i)),
        ],
        out_shape=[
            jax.ShapeDtypeStruct((m, wide), BF16),
            jax.ShapeDtypeStruct((wide, m), BF16),
            jax.ShapeDtypeStruct((wide, m), BF16),
        ],
        compiler_params=_params("parallel"),
        name="fox_prep",
    )(proj, proj, proj, f3, q_gain_row, k_gain_row)


def _fox_t_body(ka_ref, qt_ref, vt_ref, gate_ref, o_ref):
    T = qt_ref.shape[1]
    KV = min(FOXT_KV, ka_ref.shape[0])
    assert KV % T == 0
    i = pl.program_id(2)
    n_full = (i * T) // KV
    k_off = lax.broadcasted_iota(jnp.int32, (KV, T), 0)
    q_pos = i * T + lax.broadcasted_iota(jnp.int32, (KV, T), 1)
    VR = FOXT_V_ROWS

    def head_step(h, j, state, masked):
        m, acc = state
        ks = pl.ds(pl.multiple_of(j * KV, KV), KV)
        hs = slice(h * LANES, (h + 1) * LANES)
        s = _mm(ka_ref[ks, hs], qt_ref[hs, :])
        if masked:
            s = jnp.where(j * KV + k_off <= q_pos, s, -jnp.inf)
        yield
        m_new = jnp.maximum(m, jnp.max(s, axis=0, keepdims=True))
        pt = jnp.exp2(s - m_new).astype(BF16)
        pv = _mm(vt_ref[h * LANES:h * LANES + VR, ks], pt)
        yield
        return m_new, acc * jnp.exp2(m - m_new) + pv

    H2 = T // 2
    tri = lax.broadcasted_iota(jnp.int32, (H2, T), 0) <= lax.broadcasted_iota(jnp.int32, (H2, T), 1)

    def diag_step(h, state):
        m, acc = state
        hs = slice(h * LANES, (h + 1) * LANES)
        rows = slice(h * LANES, h * LANES + VR)
        ks_a = pl.ds(pl.multiple_of(i * T, H2), H2)
        ks_b = pl.ds(pl.multiple_of(i * T + H2, H2), H2)
        s_a = jnp.where(tri, _mm(ka_ref[ks_a, hs], qt_ref[hs, :]), -jnp.inf)
        s_b = jnp.where(tri[:, :H2], _mm(ka_ref[ks_b, hs], qt_ref[hs, H2:]), -jnp.inf)
        yield
        mx_a = jnp.max(s_a, axis=0, keepdims=True)
        mx_b = jnp.max(s_b, axis=0, keepdims=True)
        mx = jnp.concatenate([mx_a[:, :H2], jnp.maximum(mx_a[:, H2:], mx_b)], axis=1)
        m_new = jnp.maximum(m, mx)
        p_a = jnp.exp2(s_a - m_new).astype(BF16)
        p_b = jnp.exp2(s_b - m_new[:, H2:]).astype(BF16)
        pv_a = _mm(vt_ref[rows, ks_a], p_a)
        pv_b = _mm(vt_ref[rows, ks_b], p_b)
        yield
        pv = pv_a + jnp.concatenate([jnp.zeros((VR, H2), F32), pv_b], axis=1)
        return m_new, acc * jnp.exp2(m - m_new) + pv

    assert KV == T
    for h0 in range(0, FOXT_HEADS, FOXT_UNROLL):
        heads = list(range(h0, h0 + FOXT_UNROLL))

        def kv_step(j, states, masked):
            return tuple(_round_robin([head_step(h, j, st, masked) for h, st in zip(heads, states)]))

        init = (jnp.full((1, T), -jnp.inf, F32), jnp.zeros((VR, T), F32))
        states = lax.fori_loop(0, n_full, lambda j, st: kv_step(j, st, False), (init,) * len(heads))
        states = tuple(_round_robin([diag_step(h, st) for h, st in zip(heads, states)]))
        outs = [acc[:HEAD_DIM] / acc[HEAD_DIM:HEAD_DIM + 1] for _, acc in states]
        for n in range(0, len(heads), 2):
            pair_t = jnp.concatenate([outs[n], outs[n + 1]], axis=0)
            sl = slice((h0 + n) * HEAD_DIM, (h0 + n + 2) * HEAD_DIM)
            o_ref[:, sl] = (pair_t.T * _silu(gate_ref[:, sl].astype(F32))).astype(o_ref.dtype)


def _fox_t_attention(ka, qt, vt, proj, bsz, seq):
    T = min(FOXT_BLOCK, seq)
    nq = seq // T
    m = bsz * seq
    gw = FOXT_HEADS * HEAD_DIM
    sw = FOXT_HEADS * LANES
    n_groups = N_HEADS // FOXT_HEADS
    gate_col0 = 3 * D_MODEL // gw
    return pl.pallas_call(
        _fox_t_body,
        grid=(bsz, n_groups, nq),
        in_specs=[
            pl.BlockSpec((seq, sw), lambda b, g, i: (b, g)),
            pl.BlockSpec((sw, T), lambda b, g, i: (g, b * nq + i)),
            pl.BlockSpec((sw, seq), lambda b, g, i: (g, b)),
            pl.BlockSpec((T, gw), lambda b, g, i: (b * nq + i, gate_col0 + g)),
        ],
        out_specs=pl.BlockSpec((T, gw), lambda b, g, i: (b * nq + i, g)),
        out_shape=jax.ShapeDtypeStruct((m, D_MODEL), BF16),
        compiler_params=_params("parallel", "parallel", "arbitrary"),
        name="fox_attention",
    )(ka, qt, vt, proj)


def _fox_layer(x2, bsz, seq, norm, w_in, b_f, q_gain, k_gain, w_out):
    W = D_MODEL
    proj, f_cols, _ = _norm_matmul(x2, norm, w_in[:, :4 * W].astype(BF16), w_in[:, 4 * W:],
                                   name="fox_in_proj")
    f3 = _fox_decay(f_cols, b_f, bsz, seq)
    scale = HEAD_DIM ** -0.5 * LOG2_E
    q_gain_row = (jnp.tile(q_gain, N_HEADS) * scale).reshape(1, W).astype(F32)
    k_gain_row = jnp.tile(k_gain, N_HEADS).reshape(1, W).astype(F32)
    ka, qt, vt = _fox_prep(proj, f3, q_gain_row, k_gain_row)
    a = _fox_t_attention(ka, qt, vt, proj, bsz, seq)
    return _out_proj(a, w_out.astype(BF16), x2)


def kernel(x, rel_bias, swa_norm, swa_w_in, swa_q_gain, swa_k_gain, swa_sinks, swa_w_out, rwkv_norm, rwkv_mix, rwkv_w_in, rwkv_w0, rwkv_w_lora1, rwkv_w_lora2, rwkv_a0, rwkv_a_lora1, rwkv_a_lora2, rwkv_k_k, rwkv_k_a, rwkv_r_k, rwkv_gn_w, rwkv_gn_b, rwkv_w_out, mlstm_norm, mlstm_w_in, mlstm_b_i, mlstm_b_f, mlstm_h_gain, mlstm_w_out, fox_norm, fox_w_in, fox_b_f, fox_q_gain, fox_k_gain, fox_w_out):
    bsz, seq, d = x.shape
    depth = swa_norm.shape[0] + rwkv_norm.shape[0] + mlstm_norm.shape[0] + fox_norm.shape[0]
    x2 = x.reshape(bsz * seq, d)
    for layer in range(depth):
        kind, idx = layer % 4, layer // 4
        if kind == 0:
            x2 = _swa_layer(x2, bsz, seq, rel_bias, swa_norm[idx], swa_w_in[idx], swa_q_gain[idx],
                            swa_k_gain[idx], swa_sinks[idx], swa_w_out[idx])
        elif kind == 1:
            x2 = _rwkv_layer(x2, bsz, seq, rwkv_norm[idx], rwkv_mix[idx], rwkv_w_in[idx],
                             rwkv_w0[idx], rwkv_w_lora1[idx], rwkv_w_lora2[idx], rwkv_a0[idx],
                             rwkv_a_lora1[idx], rwkv_a_lora2[idx], rwkv_k_k[idx], rwkv_k_a[idx],
                             rwkv_r_k[idx], rwkv_gn_w[idx], rwkv_gn_b[idx], rwkv_w_out[idx])
        elif kind == 2:
            x2 = _mlstm_layer(x2, bsz, seq, mlstm_norm[idx], mlstm_w_in[idx], mlstm_b_i[idx],
                              mlstm_b_f[idx], mlstm_h_gain[idx], mlstm_w_out[idx])
        else:
            x2 = _fox_layer(x2, bsz, seq, fox_norm[idx], fox_w_in[idx], fox_b_f[idx],
                            fox_q_gain[idx], fox_k_gain[idx], fox_w_out[idx])
    return x2.reshape(bsz, seq, d)
```
